```python
import math
import jax, jax.numpy as jnp
from jax import lax
import numpy as np

D_MODEL = 1024
BATCH = 4
SEQ = 4096
DEPTH = 1

D_MIX = D_MODEL
D_CONV = D_MIX // 2
CONV_WIDTH = 3
N_DIFF_HEADS = 4
DIFF_HEAD_DIM = 64
DIFF_V_DIM = 2 * DIFF_HEAD_DIM
D_ATTN = N_DIFF_HEADS * DIFF_V_DIM
D_QK = N_DIFF_HEADS * 2 * DIFF_HEAD_DIM
D_IN_PROJ = 3 * D_CONV + 2 * D_QK + D_ATTN
ROT_DIM = DIFF_HEAD_DIM // 4
ROPE_THETA = 500000.0
Q_BLOCK = 128
N_EXPERTS = 256
TOP_K = 8
N_GROUPS = 8
TOPK_GROUPS = 4
D_EXPERT = 256
D_SHARED = 256
ROUTED_SCALE = 2.5
EXPERT_BLOCK = 128
NORM_EPS = 1e-6
SUBLN_EPS = 1e-5

kernel_name = "hybrid_conv_diffattn_moe_adaln"


def rms_norm(x, g, eps):
    xf = x.astype(jnp.float32)
    y = xf * lax.rsqrt(jnp.mean(xf * xf, axis=-1, keepdims=True) + eps)
    return (y * g.astype(jnp.float32)).astype(x.dtype)


def rope_tables(positions):
    inv_freq = ROPE_THETA ** (-jnp.arange(0, ROT_DIM, 2, dtype=jnp.float32) / ROT_DIM)
    ang = positions.astype(jnp.float32)[..., None] * inv_freq
    return jnp.cos(ang), jnp.sin(ang)


def apply_partial_rope(t, cos, sin):
    cos = cos[:, :, None, None, :].astype(t.dtype)
    sin = sin[:, :, None, None, :].astype(t.dtype)
    half = ROT_DIM // 2
    t1, t2, rest = t[..., :half], t[..., half:ROT_DIM], t[..., ROT_DIM:]
    return jnp.concatenate([t1 * cos - t2 * sin, t2 * cos + t1 * sin, rest], axis=-1)


def short_conv_mixer(h, b_gate, c_gate, conv_w, g_out):
    u = c_gate * h
    seq = u.shape[1]
    up = jnp.pad(u, ((0, 0), (CONV_WIDTH - 1, 0), (0, 0)))
    y = sum(conv_w[j] * up[:, j:j + seq] for j in range(CONV_WIDTH))
    return rms_norm(b_gate * y, g_out, NORM_EPS)


def diff_attention(q, k, v, lam, lambda_init, g_subln):
    bsz, seq = q.shape[:2]
    nqb = seq // Q_BLOCK
    scale = DIFF_HEAD_DIM ** -0.5
    qb = q.reshape(bsz, nqb, Q_BLOCK, N_DIFF_HEADS, 2, DIFF_HEAD_DIM).transpose(1, 0, 2, 3, 4, 5)
    key_idx = jnp.arange(seq)

    def block(args):
        q_blk, blk = args
        s = jnp.einsum('bqhcd,bkhcd->bhcqk', q_blk, k, preferred_element_type=jnp.float32) * scale
        q_idx = blk * Q_BLOCK + jnp.arange(Q_BLOCK)
        mask = key_idx[None, :] <= q_idx[:, None]
        p = jax.nn.softmax(jnp.where(mask, s, -jnp.inf), axis=-1)
        a = p[:, :, 0] - lam * p[:, :, 1]
        return jnp.einsum('bhqk,bkhd->bqhd', a.astype(v.dtype), v)

    o = lax.map(block, (qb, jnp.arange(nqb)))
    o = o.transpose(1, 0, 2, 3, 4).reshape(bsz, seq, N_DIFF_HEADS, DIFF_V_DIM)
    o = rms_norm(o, g_subln, SUBLN_EPS) * (1.0 - lambda_init)
    return o.reshape(bsz, seq, D_ATTN)


def route(h, w_router, router_bias):
    tokens = h.shape[0]
    scores = jax.nn.sigmoid(jnp.matmul(h, w_router, preferred_element_type=jnp.float32))
    choice = scores + router_bias.astype(jnp.float32)
    grouped = choice.reshape(tokens, N_GROUPS, N_EXPERTS // N_GROUPS)
    group_scores = lax.top_k(grouped, 2)[0].sum(-1)
    _, gidx = lax.top_k(group_scores, TOPK_GROUPS)
    gmask = jnp.any(gidx[..., None] == jnp.arange(N_GROUPS), axis=-2)
    emask = jnp.repeat(gmask, N_EXPERTS // N_GROUPS, axis=-1)
    _, idx = lax.top_k(jnp.where(emask, choice, -jnp.inf), TOP_K)
    w = jnp.take_along_axis(scores, idx, axis=-1)
    w = w / (w.sum(-1, keepdims=True) + 1e-20) * ROUTED_SCALE
    return idx, w.astype(h.dtype)


def routed_experts(h, idx, w, w_gate, w_up, w_down):
    tokens, d = h.shape
    m = tokens * TOP_K
    nb = (m + N_EXPERTS * (EXPERT_BLOCK - 1) + EXPERT_BLOCK - 1) // EXPERT_BLOCK
    p_rows = nb * EXPERT_BLOCK
    flat_e = idx.reshape(-1)
    flat_tok = jnp.arange(m, dtype=jnp.int32) // TOP_K
    flat_w = w.reshape(-1)
    order = jnp.argsort(flat_e)
    se = flat_e[order]
    counts = jnp.bincount(flat_e, length=N_EXPERTS)
    start = jnp.cumsum(counts) - counts
    padded = ((counts + EXPERT_BLOCK - 1) // EXPERT_BLOCK) * EXPERT_BLOCK
    pad_end = jnp.cumsum(padded)
    pad_start = pad_end - padded
    dest = pad_start[se] + (jnp.arange(m) - start[se])
    row_tok = jnp.full((p_rows,), tokens, jnp.int32).at[dest].set(flat_tok[order])
    row_w = jnp.zeros((p_rows,), w.dtype).at[dest].set(flat_w[order])
    blk_e = jnp.clip(jnp.searchsorted(pad_end, jnp.arange(nb) * EXPERT_BLOCK, side='right'), 0, N_EXPERTS - 1)
    h_pad = jnp.concatenate([h, jnp.zeros((1, d), h.dtype)], axis=0)

    def expert_block(args):
        rows, rw, e = args
        xb = h_pad[rows]
        act = jax.nn.silu(xb @ w_gate[e]) * (xb @ w_up[e])
        return (act @ w_down[e]) * rw[:, None]

    out = lax.map(expert_block, (row_tok.reshape(nb, EXPERT_BLOCK), row_w.reshape(nb, EXPERT_BLOCK), blk_e))
    y = jnp.zeros((tokens + 1, d), h.dtype).at[row_tok].add(out.reshape(p_rows, d))
    return y[:tokens]


def setup_inputs(seed: int = 0) -> dict:
    key = jax.random.key(seed)
    ks = jax.random.split(key, 26)
    L = DEPTH

    def nrm(k, shape, scale):
        return jax.random.normal(k, shape, jnp.float32) * scale

    pos_offset = jax.random.randint(ks[2], (BATCH, 1), 0, 1024, dtype=jnp.int32)
    return {
        "x": nrm(ks[0], (BATCH, SEQ, D_MODEL), 1.0),
        "c": nrm(ks[1], (BATCH, D_MODEL), 1.0),
        "positions": pos_offset + jnp.arange(SEQ, dtype=jnp.int32)[None, :],
        "w_ada": nrm(ks[3], (L, D_MODEL, 6 * D_MODEL), 0.02),
        "b_ada": nrm(ks[4], (L, 6 * D_MODEL), 0.02),
        "g_norm1": 1.0 + nrm(ks[5], (L, D_MODEL), 0.05),
        "w_in": nrm(ks[6], (L, D_MODEL, D_IN_PROJ), D_MODEL ** -0.5),
        "conv_w": nrm(ks[7], (L, CONV_WIDTH, D_CONV), CONV_WIDTH ** -0.5),
        "g_conv_out": 1.0 + nrm(ks[8], (L, D_CONV), 0.05),
        "lam_q1": nrm(ks[9], (L, DIFF_HEAD_DIM), 0.1),
        "lam_k1": nrm(ks[10], (L, DIFF_HEAD_DIM), 0.1),
        "lam_q2": nrm(ks[11], (L, DIFF_HEAD_DIM), 0.1),
        "lam_k2": nrm(ks[12], (L, DIFF_HEAD_DIM), 0.1),
        "g_subln": 1.0 + nrm(ks[13], (L, DIFF_V_DIM), 0.05),
        "w_out": nrm(ks[14], (L, D_MIX, D_MODEL), D_MIX ** -0.5),
        "g_norm2": 1.0 + nrm(ks[15], (L, D_MODEL), 0.05),
        "w_router": nrm(ks[16], (L, D_MODEL, N_EXPERTS), D_MODEL ** -0.5),
        "router_bias": nrm(ks[17], (L, N_EXPERTS), 0.01),
        "w_gate_e": nrm(ks[18], (L, N_EXPERTS, D_MODEL, D_EXPERT), D_MODEL ** -0.5),
        "w_up_e": nrm(ks[19], (L, N_EXPERTS, D_MODEL, D_EXPERT), D_MODEL ** -0.5),
        "w_down_e": nrm(ks[20], (L, N_EXPERTS, D_EXPERT, D_MODEL), D_EXPERT ** -0.5),
        "w_gate_s": nrm(ks[21], (L, D_MODEL, D_SHARED), D_MODEL ** -0.5),
        "w_up_s": nrm(ks[22], (L, D_MODEL, D_SHARED), D_MODEL ** -0.5),
        "w_down_s": nrm(ks[23], (L, D_SHARED, D_MODEL), D_SHARED ** -0.5),
        "g_final": 1.0 + nrm(ks[24], (D_MODEL,), 0.05),
    }


def reference(x, c, positions, w_ada, b_ada, g_norm1, w_in, conv_w, g_conv_out, lam_q1, lam_k1, lam_q2, lam_k2,
              g_subln, w_out, g_norm2, w_router, router_bias, w_gate_e, w_up_e, w_down_e, w_gate_s, w_up_s,
              w_down_s, g_final):
    bsz, seq, d = x.shape
    cos, sin = rope_tables(positions)
    c_act = jax.nn.silu(c)
    splits = [D_CONV, 2 * D_CONV, 3 * D_CONV, 3 * D_CONV + D_QK, 3 * D_CONV + 2 * D_QK]
    for l in range(DEPTH):
        lambda_init = 0.8 - 0.6 * math.exp(-0.3 * l)
        ada = (c_act @ w_ada[l] + b_ada[l])[:, None, :]
        sh1, sc1, gt1, sh2, sc2, gt2 = jnp.split(ada, 6, axis=-1)

        hn = rms_norm(x, g_norm1[l], NORM_EPS) * (1.0 + sc1) + sh1
        proj = hn @ w_in[l]
        h_cv, b_gate, c_gate, q, k, v = jnp.split(proj, splits, axis=-1)
        y_conv = short_conv_mixer(h_cv, b_gate, c_gate, conv_w[l], g_conv_out[l])
        q = apply_partial_rope(q.reshape(bsz, seq, N_DIFF_HEADS, 2, DIFF_HEAD_DIM), cos, sin)
        k = apply_partial_rope(k.reshape(bsz, seq, N_DIFF_HEADS, 2, DIFF_HEAD_DIM), cos, sin)
        v = v.reshape(bsz, seq, N_DIFF_HEADS, DIFF_V_DIM)
        lam = (jnp.exp(jnp.sum(lam_q1[l].astype(jnp.float32) * lam_k1[l].astype(jnp.float32)))
               - jnp.exp(jnp.sum(lam_q2[l].astype(jnp.float32) * lam_k2[l].astype(jnp.float32)))
               + lambda_init)
        y_attn = diff_attention(q, k, v, lam, lambda_init, g_subln[l])
        x = x + gt1 * (jnp.concatenate([y_conv, y_attn], axis=-1) @ w_out[l])

        hn2 = rms_norm(x, g_norm2[l], NORM_EPS) * (1.0 + sc2) + sh2
        ht = hn2.reshape(bsz * seq, d)
        idx, wts = route(ht, w_router[l], router_bias[l])
        y_routed = routed_experts(ht, idx, wts, w_gate_e[l], w_up_e[l], w_down_e[l])
        y_shared = (jax.nn.silu(ht @ w_gate_s[l]) * (ht @ w_up_s[l])) @ w_down_s[l]
        x = x + gt2 * (y_routed + y_shared).reshape(bsz, seq, d)
    return rms_norm(x, g_final, NORM_EPS)
```

```python
import functools
import math

import jax
import jax.numpy as jnp
from jax import lax
from jax.experimental import pallas as pl
from jax.experimental.pallas import tpu as pltpu
from jax.experimental.pallas import tpu_sc as plsc

F32 = jnp.float32
BF16 = jnp.bfloat16
I32 = jnp.int32

D_CONV = 512
CONV_WIDTH = 3
N_HEADS = 4
HEAD_DIM = 64
V_DIM = 2 * HEAD_DIM
D_ATTN = N_HEADS * V_DIM
D_QK = N_HEADS * 2 * HEAD_DIM
ROT_DIM = HEAD_DIM // 4
ROPE_THETA = 500000.0
N_EXPERTS = 256
TOP_K = 8
N_GROUPS = 8
GROUP_SIZE = N_EXPERTS // N_GROUPS
TOPK_GROUPS = 4
D_EXPERT = 256
ROUTED_SCALE = 2.5
NORM_EPS = 1e-6
SUBLN_EPS = 1e-5
LAMBDA_INIT = 0.8 - 0.6 * math.exp(-0.3 * 0)

LANES = 128
SC_CORES = 2
SC_SUBCORES = 16
SC_WORKERS = SC_CORES * SC_SUBCORES
SC_CHUNK = 128

ROW_TILE = 256
NEG_BIG = -1e30
VMEM_LIMIT = 56 * 1024 * 1024


def _cparams(sem):
    return pltpu.CompilerParams(dimension_semantics=sem, vmem_limit_bytes=VMEM_LIMIT)


def _rms(x, eps):
    return x * lax.rsqrt(jnp.mean(x * x, axis=-1, keepdims=True) + eps)


def _pack_bf16_pairs(x):
    n = x.shape[1] // 2
    bits = lax.bitcast_convert_type(x.astype(BF16).astype(F32), I32)
    lo = lax.shift_right_logical(bits[:, :n], 16)
    return lo | bits[:, n:]


def _unpack_bf16_pairs(p):
    lo = lax.bitcast_convert_type(lax.shift_left(p, 16), F32)
    hi = lax.bitcast_convert_type(p & jnp.int32(-65536), F32)
    return jnp.concatenate([lo, hi], axis=1)


def _ada_kernel(c_ref, w_ref, b_ref, o_ref):
    ca = jax.nn.silu(c_ref[...])
    o_ref[...] = jnp.dot(ca.astype(BF16), w_ref[...].astype(BF16), preferred_element_type=F32) + b_ref[...]


def _ada(c, w_ada, b_ada):
    bsz, d = c.shape
    n = w_ada.shape[1]
    tn = n // 4
    return pl.pallas_call(
        _ada_kernel,
        grid=(n // tn,),
        in_specs=[pl.BlockSpec((bsz, d), lambda j: (0, 0)),
                  pl.BlockSpec((d, tn), lambda j: (0, j)),
                  pl.BlockSpec((1, tn), lambda j: (0, j))],
        out_specs=pl.BlockSpec((bsz, tn), lambda j: (0, j)),
        out_shape=jax.ShapeDtypeStruct((bsz, n), F32),
        compiler_params=_cparams(("arbitrary",)),
        name="ada",
    )(c, w_ada, b_ada.reshape(1, n))


def _inproj_kernel(tiles_per_seq, x_ref, pos_ref, ada_ref, g1_ref, win_ref, convw_ref, gconv_ref, invf_ref, sgn_ref,
                   yconv_ref, q_ref, k_ref, v_ref, ubuf):
    i = pl.program_id(0)
    tm = x_ref.shape[0]
    x = x_ref[...]
    ada = ada_ref[0]
    sh1, sc1 = ada[0:1], ada[1:2]
    hn = _rms(x, NORM_EPS) * g1_ref[...] * (1.0 + sc1) + sh1
    hb = hn.astype(BF16)

    pc = jnp.dot(hb, win_ref[:, 0:3 * D_CONV], preferred_element_type=F32)
    u = pc[:, 2 * D_CONV:3 * D_CONV] * pc[:, 0:D_CONV]
    first = (i % tiles_per_seq) == 0

    @pl.when(first)
    def _():
        ubuf[0:8, :] = jnp.zeros((8, D_CONV), F32)

    @pl.when(jnp.logical_not(first))
    def _():
        ubuf[0:8, :] = ubuf[tm:tm + 8, :]

    ubuf[8:8 + tm, :] = u
    u1 = ubuf[7:7 + tm, :]
    u2 = ubuf[6:6 + tm, :]
    cw = convw_ref[...]
    yc = pc[:, D_CONV:2 * D_CONV] * (cw[0:1] * u2 + cw[1:2] * u1 + cw[2:3] * u)
    yconv_ref[...] = (_rms(yc, NORM_EPS) * gconv_ref[...]).astype(BF16)

    ang = pos_ref[...].astype(F32) * invf_ref[...]
    cos = jnp.cos(ang)
    sin = jnp.sin(ang)
    s_up = sin * sgn_ref[0:1]
    s_dn = sin * sgn_ref[1:2]
    half = ROT_DIM // 2

    def rope(t):
        return t * cos + pltpu.roll(t, LANES - half, axis=1) * s_up + pltpu.roll(t, half, axis=1) * s_dn

    pq = jnp.dot(hb, win_ref[:, 3 * D_CONV:3 * D_CONV + D_QK], preferred_element_type=F32)
    pk = jnp.dot(hb, win_ref[:, 3 * D_CONV + D_QK:3 * D_CONV + 2 * D_QK], preferred_element_type=F32)
    scale = HEAD_DIM ** -0.5
    for h in range(N_HEADS):
        sl = slice(h * LANES, (h + 1) * LANES)
        q_ref[:, sl] = (rope(pq[:, sl]) * scale).astype(BF16)
        k_ref[:, sl] = rope(pk[:, sl]).astype(BF16)
    v_ref[...] = jnp.dot(hb, win_ref[:, 3 * D_CONV + 2 * D_QK:], preferred_element_type=F32).astype(BF16)


def _inproj(xf, pos, ada3, g1, win_b, conv_w, g_conv, invf, sgn, seq):
    t, d = xf.shape
    tm = min(512, seq)
    tiles_per_seq = seq // tm
    n_in = win_b.shape[1]
    full = lambda shape: pl.BlockSpec(shape, lambda i: (0,) * len(shape))
    row = lambda w: pl.BlockSpec((tm, w), lambda i: (i, 0))
    return pl.pallas_call(
        functools.partial(_inproj_kernel, tiles_per_seq),
        grid=(t // tm,),
        in_specs=[row(d), row(1),
                  pl.BlockSpec((1, 6, d), lambda i: (i // tiles_per_seq, 0, 0)),
                  full((1, d)), full((d, n_in)), full((CONV_WIDTH, D_CONV)), full((1, D_CONV)),
                  full((1, LANES)), full((2, LANES))],
        out_specs=[row(D_CONV), row(D_QK), row(D_QK), row(D_ATTN)],
        out_shape=[jax.ShapeDtypeStruct((t, D_CONV), BF16), jax.ShapeDtypeStruct((t, D_QK), BF16),
                   jax.ShapeDtypeStruct((t, D_QK), BF16), jax.ShapeDtypeStruct((t, D_ATTN), BF16)],
        scratch_shapes=[pltpu.VMEM((tm + 8, D_CONV), F32)],
        compiler_params=_cparams(("arbitrary",)),
        name="inproj",
    )(xf, pos, ada3, g1, win_b, conv_w, g_conv, invf, sgn)


def _attn_kernel(q_ref, k_ref, v_ref, lamp_ref, gs_ref, o_ref, qq_ref, m_ref, l_ref, acc_ref):
    i = pl.program_id(2)
    tq = q_ref.shape[0]
    tk = tq
    q = q_ref[...]
    lane = lax.broadcasted_iota(I32, q.shape, 1)
    zero = jnp.zeros_like(q)
    qq_ref[0:tq, :] = jnp.where(lane < HEAD_DIM, q, zero)
    qq_ref[tq:2 * tq, :] = jnp.where(lane >= HEAD_DIM, q, zero)
    m_ref[...] = jnp.full(m_ref.shape, NEG_BIG, F32)
    l_ref[...] = jnp.zeros(l_ref.shape, F32)
    acc_ref[...] = jnp.zeros(acc_ref.shape, F32)

    def step(j, masked):
        start = pl.multiple_of(j * tk, tk)
        kb = k_ref[pl.ds(start, tk), :]
        vb = v_ref[pl.ds(start, tk), :]
        s = lax.dot_general(qq_ref[...], kb, (((1,), (1,)), ((), ())), preferred_element_type=F32)
        if masked:
            row = lax.broadcasted_iota(I32, s.shape, 0)
            col = lax.broadcasted_iota(I32, s.shape, 1)
            qpos = jnp.where(row >= tq, row - tq, row)
            s = jnp.where(col <= qpos, s, NEG_BIG)
        m_prev = m_ref[...]
        m_new = jnp.maximum(m_prev, jnp.max(s, axis=1, keepdims=True))
        alpha = jnp.exp(m_prev - m_new)
        p = jnp.exp(s - m_new)
        l_ref[...] = alpha * l_ref[...] + jnp.sum(p, axis=1, keepdims=True)
        acc_ref[...] = alpha * acc_ref[...] + jnp.dot(p.astype(BF16), vb, preferred_element_type=F32)
        m_ref[...] = m_new

    def body(j, carry):
        step(j, False)
        return carry

    lax.fori_loop(0, i, body, 0)
    step(i, True)

    o = acc_ref[...] / l_ref[...]
    lamp = lamp_ref[...]
    lam = (jnp.exp(jnp.sum(lamp[0:1] * lamp[1:2], axis=1, keepdims=True))
           - jnp.exp(jnp.sum(lamp[2:3] * lamp[3:4], axis=1, keepdims=True)) + LAMBDA_INIT)
    od = o[0:tq] - lam * o[tq:2 * tq]
    o_ref[...] = (_rms(od, SUBLN_EPS) * gs_ref[...] * (1.0 - LAMBDA_INIT)).astype(BF16)


def _attention(q, k, v, lamp, g_subln, bsz, seq):
    t = q.shape[0]
    tq = min(512, seq)
    nq = seq // tq
    return pl.pallas_call(
        _attn_kernel,
        grid=(bsz, N_HEADS, nq),
        in_specs=[pl.BlockSpec((tq, LANES), lambda b, h, i: (b * nq + i, h)),
                  pl.BlockSpec((seq, LANES), lambda b, h, i: (b, h)),
                  pl.BlockSpec((seq, LANES), lambda b, h, i: (b, h)),
                  pl.BlockSpec((4, HEAD_DIM), lambda b, h, i: (0, 0)),
                  pl.BlockSpec((1, V_DIM), lambda b, h, i: (0, 0))],
        out_specs=pl.BlockSpec((tq, LANES), lambda b, h, i: (b * nq + i, h)),
        out_shape=jax.ShapeDtypeStruct((t, D_ATTN), BF16),
        scratch_shapes=[pltpu.VMEM((2 * tq, LANES), BF16), pltpu.VMEM((2 * tq, 1), F32),
                        pltpu.VMEM((2 * tq, 1), F32), pltpu.VMEM((2 * tq, V_DIM), F32)],
        compiler_params=_cparams(("arbitrary", "arbitrary", "arbitrary")),
        name="attention",
    )(q, k, v, lamp, g_subln)


def _first_index(hit, idx, size, axis):
    return jnp.min(jnp.where(hit, idx, size), axis=axis, keepdims=True)


def _outproj_router_kernel(yc_ref, ya_ref, x_ref, ada_ref, g2_ref, wout_ref, wrt_ref, rb_ref,
                           x1_ref, hp_ref, idx_ref, wts_ref, rank_ref, cnt_ref, base_ref):
    i = pl.program_id(0)
    tm = x_ref.shape[0]
    ada = ada_ref[0]
    gt1, sh2, sc2 = ada[2:3], ada[3:4], ada[4:5]
    mix = (jnp.dot(yc_ref[...], wout_ref[0:D_CONV, :], preferred_element_type=F32)
           + jnp.dot(ya_ref[...], wout_ref[D_CONV:, :], preferred_element_type=F32))
    x1 = x_ref[...] + gt1 * mix
    x1_ref[...] = x1
    hn2 = _rms(x1, NORM_EPS) * g2_ref[...] * (1.0 + sc2) + sh2
    hp_ref[...] = _pack_bf16_pairs(hn2)
    hb = hn2.astype(BF16)

    logits = lax.dot_general(wrt_ref[...], hb, (((1,), (1,)), ((), ())), preferred_element_type=F32)
    scores = jax.nn.sigmoid(logits)
    choice = scores + rb_ref[...]
    neg_inf = jnp.float32(-jnp.inf)

    ch3 = choice.reshape(N_GROUPS, GROUP_SIZE, tm)
    i3 = lax.broadcasted_iota(I32, ch3.shape, 1)
    m1 = jnp.max(ch3, axis=1, keepdims=True)
    f1 = _first_index(ch3 == m1, i3, GROUP_SIZE, 1)
    m2 = jnp.max(jnp.where(i3 == f1, neg_inf, ch3), axis=1, keepdims=True)
    gs = (m1 + m2).reshape(N_GROUPS, tm)

    gi = lax.broadcasted_iota(I32, gs.shape, 0)
    gkeep = jnp.zeros(gs.shape, F32)
    for _ in range(TOPK_GROUPS):
        m = jnp.max(gs, axis=0, keepdims=True)
        f = _first_index(gs == m, gi, N_GROUPS, 0)
        sel = gi == f
        gkeep = jnp.where(sel, 1.0, gkeep)
        gs = jnp.where(sel, neg_inf, gs)
    ekeep = jnp.broadcast_to(gkeep.reshape(N_GROUPS, 1, tm), (N_GROUPS, GROUP_SIZE, tm)).reshape(N_EXPERTS, tm)
    masked = jnp.where(ekeep > 0.0, choice, neg_inf)

    ei = lax.broadcasted_iota(I32, masked.shape, 0)
    picked = jnp.zeros(masked.shape, F32)
    idxs, ws = [], []
    for _ in range(TOP_K):
        m = jnp.max(masked, axis=0, keepdims=True)
        f = _first_index(masked == m, ei, N_EXPERTS, 0)
        sel = ei == f
        idxs.append(f)
        ws.append(jnp.sum(jnp.where(sel, scores, 0.0), axis=0, keepdims=True))
        picked = jnp.where(sel, 1.0, picked)
        masked = jnp.where(sel, neg_inf, masked)
    wsum = ws[0]
    for wk in ws[1:]:
        wsum = wsum + wk
    denom = wsum + 1e-20
    for kk in range(TOP_K):
        idx_ref[kk:kk + 1, :] = idxs[kk]
        wts_ref[kk:kk + 1, :] = ws[kk] / denom * ROUTED_SCALE

    @pl.when(i == 0)
    def _():
        base_ref[...] = jnp.zeros(base_ref.shape, F32)

    si = lax.broadcasted_iota(I32, (tm, tm), 0)
    ti = lax.broadcasted_iota(I32, (tm, tm), 1)
    earlier = jnp.where(si < ti, 1.0, 0.0).astype(BF16)
    before = jnp.dot(picked.astype(BF16), earlier, preferred_element_type=F32) + base_ref[...]
    for kk in range(TOP_K):
        rank_ref[kk:kk + 1, :] = jnp.sum(jnp.where(ei == idxs[kk], before, 0.0), axis=0, keepdims=True).astype(I32)
    base_ref[...] = base_ref[...] + jnp.sum(picked, axis=1, keepdims=True)
    cnt_ref[...] = base_ref[...]


def _outproj_router(yconv, yattn, xf, ada3, g2, wout_b, wrt_b, rbias, seq):
    t, d = xf.shape
    tm = min(512, seq)
    tiles_per_seq = seq // tm
    full = lambda shape: pl.BlockSpec(shape, lambda i: (0,) * len(shape))
    row = lambda w: pl.BlockSpec((tm, w), lambda i: (i, 0))
    col = pl.BlockSpec((TOP_K, tm), lambda i: (0, i))
    return pl.pallas_call(
        _outproj_router_kernel,
        grid=(t // tm,),
        in_specs=[row(D_CONV), row(D_ATTN), row(d),
                  pl.BlockSpec((1, 6, d), lambda i: (i // tiles_per_seq, 0, 0)),
                  full((1, d)), full((D_CONV + D_ATTN, d)), full((N_EXPERTS, d)), full((N_EXPERTS, 1))],
        out_specs=[row(d), row(d // 2), col, col, col, full((N_EXPERTS, 1))],
        out_shape=[jax.ShapeDtypeStruct((t, d), F32), jax.ShapeDtypeStruct((t, d // 2), I32),
                   jax.ShapeDtypeStruct((TOP_K, t), I32), jax.ShapeDtypeStruct((TOP_K, t), F32),
                   jax.ShapeDtypeStruct((TOP_K, t), I32), jax.ShapeDtypeStruct((N_EXPERTS, 1), F32)],
        scratch_shapes=[pltpu.VMEM((N_EXPERTS, 1), F32)],
        compiler_params=_cparams(("arbitrary",)),
        name="outproj_router",
    )(yconv, yattn, xf, ada3, g2, wout_b, wrt_b, rbias)


def _dest_kernel(idx_ref, rank_ref, ps_ref, dest_ref):
    tm = idx_ref.shape[1]
    ei = lax.broadcasted_iota(I32, (N_EXPERTS, tm), 0)
    ps = ps_ref[...]
    for kk in range(TOP_K):
        start = jnp.sum(jnp.where(ei == idx_ref[kk:kk + 1, :], ps, 0.0), axis=0, keepdims=True)
        dest_ref[kk:kk + 1, :] = start.astype(I32) + rank_ref[kk:kk + 1, :]


def _dest(idx_t, rank_t, pad_start_f):
    t = idx_t.shape[1]
    tm = min(2048, t)
    col = pl.BlockSpec((TOP_K, tm), lambda i: (0, i))
    return pl.pallas_call(
        _dest_kernel,
        grid=(t // tm,),
        in_specs=[col, col, pl.BlockSpec((N_EXPERTS, 1), lambda i: (0, 0))],
        out_specs=col,
        out_shape=jax.ShapeDtypeStruct((TOP_K, t), I32),
        compiler_params=_cparams(("arbitrary",)),
        name="dest",
    )(idx_t, rank_t, pad_start_f)


def _sc_dispatch(hp, dest_ck, n_rows):
    t, w = hp.shape
    nk = dest_ck.shape[1]
    per_worker = t // SC_CHUNK // SC_WORKERS
    mesh = plsc.VectorSubcoreMesh(core_axis_name="c", subcore_axis_name="s")

    @functools.partial(
        pl.kernel, mesh=mesh,
        out_type=jax.ShapeDtypeStruct((n_rows, w), hp.dtype),
        scratch_types=[pltpu.VMEM((nk, SC_CHUNK), I32), pltpu.VMEM((SC_CHUNK, w), hp.dtype), pltpu.SemaphoreType.DMA],
    )
    def k(hp_hbm, dest_hbm, xs_hbm, idx_v, rows_v, sem):
        wid = lax.axis_index("s") * SC_CORES + lax.axis_index("c")

        @pl.loop(0, per_worker)
        def _(j):
            c = wid * per_worker + j
            pltpu.sync_copy(dest_hbm.at[c], idx_v)
            pltpu.sync_copy(hp_hbm.at[pl.ds(c * SC_CHUNK, SC_CHUNK)], rows_v)
            for kk in range(nk):
                pltpu.async_copy(rows_v, xs_hbm.at[idx_v.at[kk]], sem)
            for kk in range(nk):
                pltpu.make_async_copy(rows_v, xs_hbm.at[idx_v.at[kk]], sem).wait()

    return k(hp, dest_ck)


def _sc_gather(src, idx_c):
    n_chunks = idx_c.shape[0]
    w = src.shape[1]
    per_worker = n_chunks // SC_WORKERS
    mesh = plsc.VectorSubcoreMesh(core_axis_name="c", subcore_axis_name="s")

    @functools.partial(
        pl.kernel, mesh=mesh,
        out_type=jax.ShapeDtypeStruct((n_chunks * SC_CHUNK, w), src.dtype),
        scratch_types=[pltpu.VMEM((1, SC_CHUNK), I32), pltpu.VMEM((SC_CHUNK, w), src.dtype), pltpu.SemaphoreType.DMA],
    )
    def k(src_hbm, idx_hbm, out_hbm, idx_v, rows_v, sem):
        wid = lax.axis_index("s") * SC_CORES + lax.axis_index("c")

        @pl.loop(0, per_worker)
        def _(j):
            c = wid * per_worker + j
            pltpu.sync_copy(idx_hbm.at[c], idx_v)
            pltpu.async_copy(src_hbm.at[idx_v.at[0]], rows_v, sem).wait()
            pltpu.sync_copy(rows_v, out_hbm.at[pl.ds(c * SC_CHUNK, SC_CHUNK)])

    return k(src, idx_c)


def _experts_kernel(blk_e_ref, nused_ref, xs_ref, wg_ref, wu_ref, wd_ref, o_ref):
    b = pl.program_id(0)

    @pl.when(b < nused_ref[0])
    def _():
        xb = _unpack_bf16_pairs(xs_ref[...]).astype(BF16)
        g = jnp.dot(xb, wg_ref[...].astype(BF16), preferred_element_type=F32)
        u = jnp.dot(xb, wu_ref[...].astype(BF16), preferred_element_type=F32)
        act = (jax.nn.silu(g) * u).astype(BF16)
        o_ref[...] = _pack_bf16_pairs(jnp.dot(act, wd_ref[...].astype(BF16), preferred_element_type=F32))


def _experts(xs, blk_e, n_used, w_gate, w_up, w_down):
    n_rows, half = xs.shape
    d = 2 * half
    nb = n_rows // ROW_TILE
    rows = pl.BlockSpec((ROW_TILE, half), lambda b, be, nu: (jnp.minimum(b, nu[0] - 1), 0))
    return pl.pallas_call(
        _experts_kernel,
        grid_spec=pltpu.PrefetchScalarGridSpec(
            num_scalar_prefetch=2,
            grid=(nb,),
            in_specs=[rows,
                      pl.BlockSpec((None, d, D_EXPERT), lambda b, be, nu: (be[b], 0, 0)),
                      pl.BlockSpec((None, d, D_EXPERT), lambda b, be, nu: (be[b], 0, 0)),
                      pl.BlockSpec((None, D_EXPERT, d), lambda b, be, nu: (be[b], 0, 0))],
            out_specs=rows,
        ),
        out_shape=jax.ShapeDtypeStruct((n_rows, half), I32),
        compiler_params=_cparams(("arbitrary",)),
        name="experts",
    )(blk_e, n_used, xs, w_gate, w_up, w_down)


def _final_kernel(x1_ref, hp_ref, ou_ref, wts_ref, ada_ref, wgs_ref, wus_ref, wds_ref, gf_ref, o_ref):
    nchunk = ou_ref.shape[0]
    gt2 = ada_ref[0][5:6]
    hb = _unpack_bf16_pairs(hp_ref[...]).astype(BF16)
    g = jnp.dot(hb, wgs_ref[...], preferred_element_type=F32)
    u = jnp.dot(hb, wus_ref[...], preferred_element_type=F32)
    y = jnp.dot((jax.nn.silu(g) * u).astype(BF16), wds_ref[...], preferred_element_type=F32)
    wts = wts_ref[...]
    parts = []
    for c in range(nchunk):
        rs = slice(c * SC_CHUNK, (c + 1) * SC_CHUNK)
        yr = wts[rs, 0:1] * _unpack_bf16_pairs(ou_ref[c, 0])
        for kk in range(1, TOP_K):
            yr = yr + wts[rs, kk:kk + 1] * _unpack_bf16_pairs(ou_ref[c, kk])
        parts.append(yr)
    y = y + (parts[0] if nchunk == 1 else jnp.concatenate(parts, axis=0))
    x2 = x1_ref[...] + gt2 * y
    o_ref[...] = _rms(x2, NORM_EPS) * gf_ref[...]


def _final(x1, hp, outu4, wts, ada3, wgs_b, wus_b, wds_b, g_final, seq):
    t, d = x1.shape
    tm = min(256, seq)
    nchunk = tm // SC_CHUNK
    tiles_per_seq = seq // tm
    full = lambda shape: pl.BlockSpec(shape, lambda i: (0,) * len(shape))
    row = lambda w: pl.BlockSpec((tm, w), lambda i: (i, 0))
    return pl.pallas_call(
        _final_kernel,
        grid=(t // tm,),
        in_specs=[row(d), row(d // 2),
                  pl.BlockSpec((nchunk, TOP_K, SC_CHUNK, d // 2), lambda i: (i, 0, 0, 0)),
                  row(TOP_K),
                  pl.BlockSpec((1, 6, d), lambda i: (i // tiles_per_seq, 0, 0)),
                  full((d, D_EXPERT)), full((d, D_EXPERT)), full((D_EXPERT, d)), full((1, d))],
        out_specs=row(d),
        out_shape=jax.ShapeDtypeStruct((t, d), F32),
        compiler_params=_cparams(("arbitrary",)),
        name="final",
    )(x1, hp, outu4, wts, ada3, wgs_b, wus_b, wds_b, g_final)


def kernel(x, c, positions, w_ada, b_ada, g_norm1, w_in, conv_w, g_conv_out, lam_q1, lam_k1, lam_q2, lam_k2, g_subln, w_out, g_norm2, w_router, router_bias, w_gate_e, w_up_e, w_down_e, w_gate_s, w_up_s, w_down_s, g_final):
    bsz, seq, d = x.shape
    t = bsz * seq
    xf = x.reshape(t, d)
    pos = positions.reshape(t, 1)

    inv_freq = ROPE_THETA ** (-jnp.arange(0, ROT_DIM, 2, dtype=F32) / ROT_DIM)
    half = ROT_DIM // 2
    comp = jnp.concatenate([inv_freq, inv_freq, jnp.zeros((HEAD_DIM - ROT_DIM,), F32)])
    invf = jnp.concatenate([comp, comp]).reshape(1, LANES)
    one = jnp.ones((half,), F32)
    zero = jnp.zeros((half,), F32)
    rest = jnp.zeros((HEAD_DIM - ROT_DIM,), F32)
    up = jnp.concatenate([-one, zero, rest])
    dn = jnp.concatenate([zero, one, rest])
    sgn = jnp.stack([jnp.concatenate([up, up]), jnp.concatenate([dn, dn])])

    ada3 = _ada(c, w_ada[0], b_ada[0]).reshape(bsz, 6, d)
    yconv, q, k, v = _inproj(xf, pos, ada3, g_norm1[0].reshape(1, d), w_in[0].astype(BF16), conv_w[0],
                             g_conv_out[0].reshape(1, D_CONV), invf, sgn, seq)
    lamp = jnp.stack([lam_q1[0], lam_k1[0], lam_q2[0], lam_k2[0]]).astype(F32)
    yattn = _attention(q, k, v, lamp, g_subln[0].reshape(1, V_DIM), bsz, seq)

    x1, hp, idx_t, wts_t, rank_t, counts_f = _outproj_router(
        yconv, yattn, xf, ada3, g_norm2[0].reshape(1, d), w_out[0].astype(BF16),
        w_router[0].T.astype(BF16), router_bias[0].reshape(N_EXPERTS, 1), seq)

    counts = counts_f[:, 0].astype(I32)
    padded = ((counts + ROW_TILE - 1) // ROW_TILE) * ROW_TILE
    pad_end = jnp.cumsum(padded)
    pad_start = pad_end - padded
    nb = (t * TOP_K + N_EXPERTS * (ROW_TILE - 1)) // ROW_TILE
    n_used = pad_end[-1] // ROW_TILE
    blk = jnp.minimum(jnp.arange(nb, dtype=I32), n_used - 1) * ROW_TILE
    blk_e = jnp.clip(jnp.searchsorted(pad_end, blk, side='right'), 0, N_EXPERTS - 1).astype(I32)

    dest_t = _dest(idx_t, rank_t, pad_start.astype(F32).reshape(N_EXPERTS, 1))
    n_chunks = t // SC_CHUNK
    dest_ck = dest_t.reshape(TOP_K, n_chunks, SC_CHUNK).transpose(1, 0, 2)

    xs = _sc_dispatch(hp, dest_ck, nb * ROW_TILE)
    outs = _experts(xs, blk_e, n_used.reshape(1).astype(I32), w_gate_e[0], w_up_e[0], w_down_e[0])
    outu = _sc_gather(outs, dest_ck.reshape(n_chunks * TOP_K, 1, SC_CHUNK))

    out = _final(x1, hp, outu.reshape(n_chunks, TOP_K, SC_CHUNK, d // 2), wts_t.T, ada3,
                 w_gate_s[0].astype(BF16), w_up_s[0].astype(BF16), w_down_s[0].astype(BF16),
                 g_final.reshape(1, d), seq)
    return out.reshape(bsz, seq, d)
```

```python
import functools
import math

import jax
import jax.numpy as jnp
from jax import lax
from jax.experimental import pallas as pl
from jax.experimental.pallas import tpu as pltpu
from jax.experimental.pallas import tpu_sc as plsc

F32 = jnp.float32
BF16 = jnp.bfloat16
I32 = jnp.int32

D_CONV = 512
CONV_WIDTH = 3
N_HEADS = 4
HEAD_DIM = 64
V_DIM = 2 * HEAD_DIM
D_ATTN = N_HEADS * V_DIM
D_QK = N_HEADS * 2 * HEAD_DIM
ROT_DIM = HEAD_DIM // 4
ROPE_THETA = 500000.0
N_EXPERTS = 256
TOP_K = 8
N_GROUPS = 8
GROUP_SIZE = N_EXPERTS // N_GROUPS
TOPK_GROUPS = 4
D_EXPERT = 256
ROUTED_SCALE = 2.5
NORM_EPS = 1e-6
SUBLN_EPS = 1e-5
LAMBDA_INIT = 0.8 - 0.6 * math.exp(-0.3 * 0)

LANES = 128
SC_CORES = 2
SC_SUBCORES = 16
SC_WORKERS = SC_CORES * SC_SUBCORES
SC_CHUNK = 128

ROW_TILE = 256
NEG_BIG = -1e30
VMEM_LIMIT = 56 * 1024 * 1024


def _cparams(sem):
    return pltpu.CompilerParams(dimension_semantics=sem, vmem_limit_bytes=VMEM_LIMIT)


def _rms(x, eps):
    return x * lax.rsqrt(jnp.mean(x * x, axis=-1, keepdims=True) + eps)


def _pack_bf16_pairs(x):
    n = x.shape[1] // 2
    bits = lax.bitcast_convert_type(x.astype(BF16).astype(F32), I32)
    lo = lax.shift_right_logical(bits[:, :n], 16)
    return lo | bits[:, n:]


def _unpack_bf16_pairs(p):
    lo = lax.bitcast_convert_type(lax.shift_left(p, 16), F32)
    hi = lax.bitcast_convert_type(p & jnp.int32(-65536), F32)
    return jnp.concatenate([lo, hi], axis=1)


def _ada_kernel(c_ref, w_ref, b_ref, o_ref):
    ca = jax.nn.silu(c_ref[...])
    o_ref[...] = jnp.dot(ca.astype(BF16), w_ref[...].astype(BF16), preferred_element_type=F32) + b_ref[...]


def _ada(c, w_ada, b_ada):
    bsz, d = c.shape
    n = w_ada.shape[1]
    tn = n // 4
    return pl.pallas_call(
        _ada_kernel,
        grid=(n // tn,),
        in_specs=[pl.BlockSpec((bsz, d), lambda j: (0, 0)),
                  pl.BlockSpec((d, tn), lambda j: (0, j)),
                  pl.BlockSpec((1, tn), lambda j: (0, j))],
        out_specs=pl.BlockSpec((bsz, tn), lambda j: (0, j)),
        out_shape=jax.ShapeDtypeStruct((bsz, n), F32),
        compiler_params=_cparams(("arbitrary",)),
        name="ada",
    )(c, w_ada, b_ada.reshape(1, n))


def _inproj_kernel(tiles_per_seq, x_ref, pos_ref, ada_ref, g1_ref, win_ref, convw_ref, gconv_ref, invf_ref, sgn_ref,
                   yconv_ref, qt_ref, k_ref, vt_ref, ubuf):
    i = pl.program_id(0)
    tm = x_ref.shape[0]
    x = x_ref[...]
    ada = ada_ref[0]
    sh1, sc1 = ada[0:1], ada[1:2]
    hn = _rms(x, NORM_EPS) * g1_ref[...] * (1.0 + sc1) + sh1
    hb = hn.astype(BF16)

    pc = jnp.dot(hb, win_ref[:, 0:3 * D_CONV], preferred_element_type=F32)
    u = pc[:, 2 * D_CONV:3 * D_CONV] * pc[:, 0:D_CONV]
    first = (i % tiles_per_seq) == 0

    @pl.when(first)
    def _():
        ubuf[0:8, :] = jnp.zeros((8, D_CONV), F32)

    @pl.when(jnp.logical_not(first))
    def _():
        ubuf[0:8, :] = ubuf[tm:tm + 8, :]

    ubuf[8:8 + tm, :] = u
    u1 = ubuf[7:7 + tm, :]
    u2 = ubuf[6:6 + tm, :]
    cw = convw_ref[...]
    yc = pc[:, D_CONV:2 * D_CONV] * (cw[0:1] * u2 + cw[1:2] * u1 + cw[2:3] * u)
    yconv_ref[...] = (_rms(yc, NORM_EPS) * gconv_ref[...]).astype(BF16)

    ang = pos_ref[...].astype(F32) * invf_ref[...]
    cos = jnp.cos(ang)
    sin = jnp.sin(ang)
    s_up = sin * sgn_ref[0:1]
    s_dn = sin * sgn_ref[1:2]
    half = ROT_DIM // 2

    def rope(t):
        return t * cos + pltpu.roll(t, LANES - half, axis=1) * s_up + pltpu.roll(t, half, axis=1) * s_dn

    pq = jnp.dot(hb, win_ref[:, 3 * D_CONV:3 * D_CONV + D_QK], preferred_element_type=F32)
    pk = jnp.dot(hb, win_ref[:, 3 * D_CONV + D_QK:3 * D_CONV + 2 * D_QK], preferred_element_type=F32)
    scale = HEAD_DIM ** -0.5
    qs = []
    for h in range(N_HEADS):
        sl = slice(h * LANES, (h + 1) * LANES)
        qs.append(rope(pq[:, sl]) * scale)
        k_ref[:, sl] = rope(pk[:, sl]).astype(BF16)
    qt_ref[0] = jnp.concatenate(qs, axis=1).T.astype(BF16)
    pv = jnp.dot(hb, win_ref[:, 3 * D_CONV + 2 * D_QK:], preferred_element_type=F32)
    vt_ref[0] = pv.T.astype(BF16)


def _inproj(xf, pos, ada3, g1, win_b, conv_w, g_conv, invf, sgn, seq):
    t, d = xf.shape
    tm = min(512, seq)
    tiles_per_seq = seq // tm
    n_in = win_b.shape[1]
    full = lambda shape: pl.BlockSpec(shape, lambda i: (0,) * len(shape))
    row = lambda w: pl.BlockSpec((tm, w), lambda i: (i, 0))
    colmajor = lambda w: pl.BlockSpec((1, w, tm), lambda i: (i // tiles_per_seq, 0, i % tiles_per_seq))
    return pl.pallas_call(
        functools.partial(_inproj_kernel, tiles_per_seq),
        grid=(t // tm,),
        in_specs=[row(d), row(1),
                  pl.BlockSpec((1, 6, d), lambda i: (i // tiles_per_seq, 0, 0)),
                  full((1, d)), full((d, n_in)), full((CONV_WIDTH, D_CONV)), full((1, D_CONV)),
                  full((1, LANES)), full((2, LANES))],
        out_specs=[row(D_CONV), colmajor(D_QK), row(D_QK), colmajor(D_ATTN)],
        out_shape=[jax.ShapeDtypeStruct((t, D_CONV), BF16), jax.ShapeDtypeStruct((t // seq, D_QK, seq), BF16),
                   jax.ShapeDtypeStruct((t, D_QK), BF16), jax.ShapeDtypeStruct((t // seq, D_ATTN, seq), BF16)],
        scratch_shapes=[pltpu.VMEM((tm + 8, D_CONV), F32)],
        compiler_params=_cparams(("arbitrary",)),
        name="inproj",
    )(xf, pos, ada3, g1, win_b, conv_w, g_conv, invf, sgn)


def _attn_kernel(qt_ref, k_ref, vt_ref, lamp_ref, gs_ref, o_ref, qq_ref, m_ref, l_ref, acc_ref):
    i = pl.program_id(2)
    tq = qt_ref.shape[2]
    tk = tq
    qt = qt_ref[0]
    feat = lax.broadcasted_iota(I32, qt.shape, 0)
    zero = jnp.zeros_like(qt)
    qq_ref[:, 0:tq] = jnp.where(feat < HEAD_DIM, qt, zero)
    qq_ref[:, tq:2 * tq] = jnp.where(feat >= HEAD_DIM, qt, zero)
    m_ref[...] = jnp.full(m_ref.shape, NEG_BIG, F32)
    l_ref[...] = jnp.zeros(l_ref.shape, F32)
    acc_ref[...] = jnp.zeros(acc_ref.shape, F32)

    def step(j, masked):
        start = pl.multiple_of(j * tk, tk)
        kb = k_ref[pl.ds(start, tk), :]
        vtb = vt_ref[0, :, pl.ds(start, tk)]
        s = jnp.dot(kb, qq_ref[...], preferred_element_type=F32)
        if masked:
            key = lax.broadcasted_iota(I32, s.shape, 0)
            col = lax.broadcasted_iota(I32, s.shape, 1)
            qpos = jnp.where(col >= tq, col - tq, col)
            s = jnp.where(key <= qpos, s, NEG_BIG)
        m_prev = m_ref[...]
        m_new = jnp.maximum(m_prev, jnp.max(s, axis=0, keepdims=True))
        alpha = jnp.exp(m_prev - m_new)
        p = jnp.exp(s - m_new)
        l_ref[...] = alpha * l_ref[...] + jnp.sum(p, axis=0, keepdims=True)
        acc_ref[...] = alpha * acc_ref[...] + jnp.dot(vtb, p.astype(BF16), preferred_element_type=F32)
        m_ref[...] = m_new

    def body(j, carry):
        step(j, False)
        return carry

    lax.fori_loop(0, i, body, 0)
    step(i, True)

    o = acc_ref[...] / l_ref[...]
    lamp = lamp_ref[...]
    lam = (jnp.exp(jnp.sum(lamp[0:1] * lamp[1:2], axis=1, keepdims=True))
           - jnp.exp(jnp.sum(lamp[2:3] * lamp[3:4], axis=1, keepdims=True)) + LAMBDA_INIT)
    od = o[:, 0:tq] - lam * o[:, tq:2 * tq]
    y = od * lax.rsqrt(jnp.mean(od * od, axis=0, keepdims=True) + SUBLN_EPS) * gs_ref[...] * (1.0 - LAMBDA_INIT)
    o_ref[...] = y.T.astype(BF16)


def _attention(qt, k, vt, lamp, g_subln, bsz, seq):
    t = k.shape[0]
    tq = min(512, seq)
    nq = seq // tq
    return pl.pallas_call(
        _attn_kernel,
        grid=(bsz, N_HEADS, nq),
        in_specs=[pl.BlockSpec((1, LANES, tq), lambda b, h, i: (b, h, i)),
                  pl.BlockSpec((seq, LANES), lambda b, h, i: (b, h)),
                  pl.BlockSpec((1, V_DIM, seq), lambda b, h, i: (b, h, 0)),
                  pl.BlockSpec((4, HEAD_DIM), lambda b, h, i: (0, 0)),
                  pl.BlockSpec((V_DIM, 1), lambda b, h, i: (0, 0))],
        out_specs=pl.BlockSpec((tq, LANES), lambda b, h, i: (b * nq + i, h)),
        out_shape=jax.ShapeDtypeStruct((t, D_ATTN), BF16),
        scratch_shapes=[pltpu.VMEM((LANES, 2 * tq), BF16), pltpu.VMEM((1, 2 * tq), F32),
                        pltpu.VMEM((1, 2 * tq), F32), pltpu.VMEM((V_DIM, 2 * tq), F32)],
        compiler_params=_cparams(("arbitrary", "arbitrary", "arbitrary")),
        name="attention",
    )(qt, k, vt, lamp, g_subln)


def _first_index(hit, idx, size, axis):
    return jnp.min(jnp.where(hit, idx, size), axis=axis, keepdims=True)


def _outproj_router_kernel(yc_ref, ya_ref, x_ref, ada_ref, g2_ref, wout_ref, wrt_ref, rb_ref,
                           x1_ref, hp_ref, idx_ref, wts_ref, rank_ref, cnt_ref, base_ref):
    i = pl.program_id(0)
    tm = x_ref.shape[0]
    ada = ada_ref[0]
    gt1, sh2, sc2 = ada[2:3], ada[3:4], ada[4:5]
    mix = (jnp.dot(yc_ref[...], wout_ref[0:D_CONV, :], preferred_element_type=F32)
           + jnp.dot(ya_ref[...], wout_ref[D_CONV:, :], preferred_element_type=F32))
    x1 = x_ref[...] + gt1 * mix
    x1_ref[...] = x1
    hn2 = _rms(x1, NORM_EPS) * g2_ref[...] * (1.0 + sc2) + sh2
    hp_ref[...] = _pack_bf16_pairs(hn2)
    hb = hn2.astype(BF16)

    logits = lax.dot_general(wrt_ref[...], hb, (((1,), (1,)), ((), ())), preferred_element_type=F32)
    scores = jax.nn.sigmoid(logits)
    choice = scores + rb_ref[...]
    neg_inf = jnp.float32(-jnp.inf)

    ch3 = choice.reshape(N_GROUPS, GROUP_SIZE, tm)
    i3 = lax.broadcasted_iota(I32, ch3.shape, 1)
    m1 = jnp.max(ch3, axis=1, keepdims=True)
    f1 = _first_index(ch3 == m1, i3, GROUP_SIZE, 1)
    m2 = jnp.max(jnp.where(i3 == f1, neg_inf, ch3), axis=1, keepdims=True)
    gs = (m1 + m2).reshape(N_GROUPS, tm)

    gi = lax.broadcasted_iota(I32, gs.shape, 0)
    gkeep = jnp.zeros(gs.shape, F32)
    for _ in range(TOPK_GROUPS):
        m = jnp.max(gs, axis=0, keepdims=True)
        f = _first_index(gs == m, gi, N_GROUPS, 0)
        sel = gi == f
        gkeep = jnp.where(sel, 1.0, gkeep)
        gs = jnp.where(sel, neg_inf, gs)
    ekeep = jnp.broadcast_to(gkeep.reshape(N_GROUPS, 1, tm), (N_GROUPS, GROUP_SIZE, tm)).reshape(N_EXPERTS, tm)
    masked = jnp.where(ekeep > 0.0, choice, neg_inf)

    ei = lax.broadcasted_iota(I32, masked.shape, 0)
    picked = jnp.zeros(masked.shape, F32)
    idxs, ws = [], []
    for _ in range(TOP_K):
        m = jnp.max(masked, axis=0, keepdims=True)
        f = _first_index(masked == m, ei, N_EXPERTS, 0)
        sel = ei == f
        idxs.append(f)
        ws.append(jnp.sum(jnp.where(sel, scores, 0.0), axis=0, keepdims=True))
        picked = jnp.where(sel, 1.0, picked)
        masked = jnp.where(sel, neg_inf, masked)
    wsum = ws[0]
    for wk in ws[1:]:
        wsum = wsum + wk
    denom = wsum + 1e-20
    for kk in range(TOP_K):
        idx_ref[kk:kk + 1, :] = idxs[kk]
        wts_ref[kk:kk + 1, :] = ws[kk] / denom * ROUTED_SCALE

    @pl.when(i == 0)
    def _():
        base_ref[...] = jnp.zeros(base_ref.shape, F32)

    si = lax.broadcasted_iota(I32, (tm, tm), 0)
    ti = lax.broadcasted_iota(I32, (tm, tm), 1)
    earlier = jnp.where(si < ti, 1.0, 0.0).astype(BF16)
    before = jnp.dot(picked.astype(BF16), earlier, preferred_element_type=F32) + base_ref[...]
    for kk in range(TOP_K):
        rank_ref[kk:kk + 1, :] = jnp.sum(jnp.where(ei == idxs[kk], before, 0.0), axis=0, keepdims=True).astype(I32)
    base_ref[...] = base_ref[...] + jnp.sum(picked, axis=1, keepdims=True)
    cnt_ref[...] = base_ref[...]


def _outproj_router(yconv, yattn, xf, ada3, g2, wout_b, wrt_b, rbias, seq):
    t, d = xf.shape
    tm = min(512, seq)
    tiles_per_seq = seq // tm
    full = lambda shape: pl.BlockSpec(shape, lambda i: (0,) * len(shape))
    row = lambda w: pl.BlockSpec((tm, w), lambda i: (i, 0))
    col = pl.BlockSpec((TOP_K, tm), lambda i: (0, i))
    return pl.pallas_call(
        _outproj_router_kernel,
        grid=(t // tm,),
        in_specs=[row(D_CONV), row(D_ATTN), row(d),
                  pl.BlockSpec((1, 6, d), lambda i: (i // tiles_per_seq, 0, 0)),
                  full((1, d)), full((D_CONV + D_ATTN, d)), full((N_EXPERTS, d)), full((N_EXPERTS, 1))],
        out_specs=[row(d), row(d // 2), col, col, col, full((N_EXPERTS, 1))],
        out_shape=[jax.ShapeDtypeStruct((t, d), F32), jax.ShapeDtypeStruct((t, d // 2), I32),
                   jax.ShapeDtypeStruct((TOP_K, t), I32), jax.ShapeDtypeStruct((TOP_K, t), F32),
                   jax.ShapeDtypeStruct((TOP_K, t), I32), jax.ShapeDtypeStruct((N_EXPERTS, 1), F32)],
        scratch_shapes=[pltpu.VMEM((N_EXPERTS, 1), F32)],
        compiler_params=_cparams(("arbitrary",)),
        name="outproj_router",
    )(yconv, yattn, xf, ada3, g2, wout_b, wrt_b, rbias)


def _dest_kernel(idx_ref, rank_ref, ps_ref, dest_ref):
    tm = idx_ref.shape[1]
    ei = lax.broadcasted_iota(I32, (N_EXPERTS, tm), 0)
    ps = ps_ref[...]
    for kk in range(TOP_K):
        start = jnp.sum(jnp.where(ei == idx_ref[kk:kk + 1, :], ps, 0.0), axis=0, keepdims=True)
        dest_ref[kk:kk + 1, :] = start.astype(I32) + rank_ref[kk:kk + 1, :]


def _dest(idx_t, rank_t, pad_start_f):
    t = idx_t.shape[1]
    tm = min(2048, t)
    col = pl.BlockSpec((TOP_K, tm), lambda i: (0, i))
    return pl.pallas_call(
        _dest_kernel,
        grid=(t // tm,),
        in_specs=[col, col, pl.BlockSpec((N_EXPERTS, 1), lambda i: (0, 0))],
        out_specs=col,
        out_shape=jax.ShapeDtypeStruct((TOP_K, t), I32),
        compiler_params=_cparams(("arbitrary",)),
        name="dest",
    )(idx_t, rank_t, pad_start_f)


def _sc_dispatch(hp, dest_ck, n_rows):
    t, w = hp.shape
    nk = dest_ck.shape[1]
    per_worker = t // SC_CHUNK // SC_WORKERS
    mesh = plsc.VectorSubcoreMesh(core_axis_name="c", subcore_axis_name="s")

    @functools.partial(
        pl.kernel, mesh=mesh,
        out_type=jax.ShapeDtypeStruct((n_rows, w), hp.dtype),
        scratch_types=[pltpu.VMEM((nk, SC_CHUNK), I32), pltpu.VMEM((SC_CHUNK, w), hp.dtype), pltpu.SemaphoreType.DMA],
    )
    def k(hp_hbm, dest_hbm, xs_hbm, idx_v, rows_v, sem):
        wid = lax.axis_index("s") * SC_CORES + lax.axis_index("c")

        @pl.loop(0, per_worker)
        def _(j):
            c = wid * per_worker + j
            pltpu.sync_copy(dest_hbm.at[c], idx_v)
            pltpu.sync_copy(hp_hbm.at[pl.ds(c * SC_CHUNK, SC_CHUNK)], rows_v)
            for kk in range(nk):
                pltpu.async_copy(rows_v, xs_hbm.at[idx_v.at[kk]], sem)
            for kk in range(nk):
                pltpu.make_async_copy(rows_v, xs_hbm.at[idx_v.at[kk]], sem).wait()

    return k(hp, dest_ck)


def _sc_gather(src, idx_c):
    n_chunks = idx_c.shape[0]
    w = src.shape[1]
    per_worker = n_chunks // SC_WORKERS
    mesh = plsc.VectorSubcoreMesh(core_axis_name="c", subcore_axis_name="s")

    @functools.partial(
        pl.kernel, mesh=mesh,
        out_type=jax.ShapeDtypeStruct((n_chunks * SC_CHUNK, w), src.dtype),
        scratch_types=[pltpu.VMEM((1, SC_CHUNK), I32), pltpu.VMEM((SC_CHUNK, w), src.dtype), pltpu.SemaphoreType.DMA],
    )
    def k(src_hbm, idx_hbm, out_hbm, idx_v, rows_v, sem):
        wid = lax.axis_index("s") * SC_CORES + lax.axis_index("c")

        @pl.loop(0, per_worker)
        def _(j):
            c = wid * per_worker + j
            pltpu.sync_copy(idx_hbm.at[c], idx_v)
            pltpu.async_copy(src_hbm.at[idx_v.at[0]], rows_v, sem).wait()
            pltpu.sync_copy(rows_v, out_hbm.at[pl.ds(c * SC_CHUNK, SC_CHUNK)])

    return k(src, idx_c)


def _experts_kernel(blk_e_ref, nused_ref, xs_ref, wg_ref, wu_ref, wd_ref, o_ref):
    b = pl.program_id(0)

    @pl.when(b < nused_ref[0])
    def _():
        xb = _unpack_bf16_pairs(xs_ref[...]).astype(BF16)
        g = jnp.dot(xb, wg_ref[...].astype(BF16), preferred_element_type=F32)
        u = jnp.dot(xb, wu_ref[...].astype(BF16), preferred_element_type=F32)
        act = (jax.nn.silu(g) * u).astype(BF16)
        o_ref[...] = _pack_bf16_pairs(jnp.dot(act, wd_ref[...].astype(BF16), preferred_element_type=F32))


def _experts(xs, blk_e, n_used, w_gate, w_up, w_down):
    n_rows, half = xs.shape
    d = 2 * half
    nb = n_rows // ROW_TILE
    rows = pl.BlockSpec((ROW_TILE, half), lambda b, be, nu: (jnp.minimum(b, nu[0] - 1), 0))
    return pl.pallas_call(
        _experts_kernel,
        grid_spec=pltpu.PrefetchScalarGridSpec(
            num_scalar_prefetch=2,
            grid=(nb,),
            in_specs=[rows,
                      pl.BlockSpec((None, d, D_EXPERT), lambda b, be, nu: (be[b], 0, 0)),
                      pl.BlockSpec((None, d, D_EXPERT), lambda b, be, nu: (be[b], 0, 0)),
                      pl.BlockSpec((None, D_EXPERT, d), lambda b, be, nu: (be[b], 0, 0))],
            out_specs=rows,
        ),
        out_shape=jax.ShapeDtypeStruct((n_rows, half), I32),
        compiler_params=_cparams(("arbitrary",)),
        name="experts",
    )(blk_e, n_used, xs, w_gate, w_up, w_down)


def _final_kernel(x1_ref, hp_ref, ou_ref, wts_ref, ada_ref, wgs_ref, wus_ref, wds_ref, gf_ref, o_ref):
    nchunk = ou_ref.shape[0]
    gt2 = ada_ref[0][5:6]
    hb = _unpack_bf16_pairs(hp_ref[...]).astype(BF16)
    g = jnp.dot(hb, wgs_ref[...], preferred_element_type=F32)
    u = jnp.dot(hb, wus_ref[...], preferred_element_type=F32)
    y = jnp.dot((jax.nn.silu(g) * u).astype(BF16), wds_ref[...], preferred_element_type=F32)
    wts = wts_ref[...]
    parts = []
    for c in range(nchunk):
        rs = slice(c * SC_CHUNK, (c + 1) * SC_CHUNK)
        yr = wts[rs, 0:1] * _unpack_bf16_pairs(ou_ref[c, 0])
        for kk in range(1, TOP_K):
            yr = yr + wts[rs, kk:kk + 1] * _unpack_bf16_pairs(ou_ref[c, kk])
        parts.append(yr)
    y = y + (parts[0] if nchunk == 1 else jnp.concatenate(parts, axis=0))
    x2 = x1_ref[...] + gt2 * y
    o_ref[...] = _rms(x2, NORM_EPS) * gf_ref[...]


def _final(x1, hp, outu4, wts, ada3, wgs_b, wus_b, wds_b, g_final, seq):
    t, d = x1.shape
    tm = min(256, seq)
    nchunk = tm // SC_CHUNK
    tiles_per_seq = seq // tm
    full = lambda shape: pl.BlockSpec(shape, lambda i: (0,) * len(shape))
    row = lambda w: pl.BlockSpec((tm, w), lambda i: (i, 0))
    return pl.pallas_call(
        _final_kernel,
        grid=(t // tm,),
        in_specs=[row(d), row(d // 2),
                  pl.BlockSpec((nchunk, TOP_K, SC_CHUNK, d // 2), lambda i: (i, 0, 0, 0)),
                  row(TOP_K),
                  pl.BlockSpec((1, 6, d), lambda i: (i // tiles_per_seq, 0, 0)),
                  full((d, D_EXPERT)), full((d, D_EXPERT)), full((D_EXPERT, d)), full((1, d))],
        out_specs=row(d),
        out_shape=jax.ShapeDtypeStruct((t, d), F32),
        compiler_params=_cparams(("arbitrary",)),
        name="final",
    )(x1, hp, outu4, wts, ada3, wgs_b, wus_b, wds_b, g_final)


def kernel(x, c, positions, w_ada, b_ada, g_norm1, w_in, conv_w, g_conv_out, lam_q1, lam_k1, lam_q2, lam_k2, g_subln, w_out, g_norm2, w_router, router_bias, w_gate_e, w_up_e, w_down_e, w_gate_s, w_up_s, w_down_s, g_final):
    bsz, seq, d = x.shape
    t = bsz * seq
    xf = x.reshape(t, d)
    pos = positions.reshape(t, 1)

    inv_freq = ROPE_THETA ** (-jnp.arange(0, ROT_DIM, 2, dtype=F32) / ROT_DIM)
    half = ROT_DIM // 2
    comp = jnp.concatenate([inv_freq, inv_freq, jnp.zeros((HEAD_DIM - ROT_DIM,), F32)])
    invf = jnp.concatenate([comp, comp]).reshape(1, LANES)
    one = jnp.ones((half,), F32)
    zero = jnp.zeros((half,), F32)
    rest = jnp.zeros((HEAD_DIM - ROT_DIM,), F32)
    up = jnp.concatenate([-one, zero, rest])
    dn = jnp.concatenate([zero, one, rest])
    sgn = jnp.stack([jnp.concatenate([up, up]), jnp.concatenate([dn, dn])])

    ada3 = _ada(c, w_ada[0], b_ada[0]).reshape(bsz, 6, d)
    yconv, qt, k, vt = _inproj(xf, pos, ada3, g_norm1[0].reshape(1, d), w_in[0].astype(BF16), conv_w[0],
                             g_conv_out[0].reshape(1, D_CONV), invf, sgn, seq)
    lamp = jnp.stack([lam_q1[0], lam_k1[0], lam_q2[0], lam_k2[0]]).astype(F32)
    yattn = _attention(qt, k, vt, lamp, g_subln[0].reshape(V_DIM, 1), bsz, seq)

    x1, hp, idx_t, wts_t, rank_t, counts_f = _outproj_router(
        yconv, yattn, xf, ada3, g_norm2[0].reshape(1, d), w_out[0].astype(BF16),
        w_router[0].T.astype(BF16), router_bias[0].reshape(N_EXPERTS, 1), seq)

    counts = counts_f[:, 0].astype(I32)
    padded = ((counts + ROW_TILE - 1) // ROW_TILE) * ROW_TILE
    pad_end = jnp.cumsum(padded)
    pad_start = pad_end - padded
    nb = (t * TOP_K + N_EXPERTS * (ROW_TILE - 1)) // ROW_TILE
    n_used = pad_end[-1] // ROW_TILE
    blk = jnp.minimum(jnp.arange(nb, dtype=I32), n_used - 1) * ROW_TILE
    blk_e = jnp.minimum(jnp.sum((pad_end[None, :] <= blk[:, None]).astype(I32), axis=1), N_EXPERTS - 1)

    dest_t = _dest(idx_t, rank_t, pad_start.astype(F32).reshape(N_EXPERTS, 1))
    n_chunks = t // SC_CHUNK
    dest_ck = dest_t.reshape(TOP_K, n_chunks, SC_CHUNK).transpose(1, 0, 2)

    xs = _sc_dispatch(hp, dest_ck, nb * ROW_TILE)
    outs = _experts(xs, blk_e, n_used.reshape(1).astype(I32), w_gate_e[0], w_up_e[0], w_down_e[0])
    outu = _sc_gather(outs, dest_ck.reshape(n_chunks * TOP_K, 1, SC_CHUNK))

    out = _final(x1, hp, outu.reshape(n_chunks, TOP_K, SC_CHUNK, d // 2), wts_t.T, ada3,
                 w_gate_s[0].astype(BF16), w_up_s[0].astype(BF16), w_down_s[0].astype(BF16),
                 g_final.reshape(1, d), seq)
    return out.reshape(bsz, seq, d)
```

```python
import functools
import math

import jax
import jax.numpy as jnp
from jax import lax
from jax.experimental import pallas as pl
from jax.experimental.pallas import tpu as pltpu
from jax.experimental.pallas import tpu_sc as plsc

F32 = jnp.float32
BF16 = jnp.bfloat16
I32 = jnp.int32

D_CONV = 512
CONV_WIDTH = 3
N_HEADS = 4
HEAD_DIM = 64
V_DIM = 2 * HEAD_DIM
D_ATTN = N_HEADS * V_DIM
D_QK = N_HEADS * 2 * HEAD_DIM
ROT_DIM = HEAD_DIM // 4
ROPE_THETA = 500000.0
N_EXPERTS = 256
TOP_K = 8
N_GROUPS = 8
GROUP_SIZE = N_EXPERTS // N_GROUPS
TOPK_GROUPS = 4
D_EXPERT = 256
ROUTED_SCALE = 2.5
NORM_EPS = 1e-6
SUBLN_EPS = 1e-5
LAMBDA_INIT = 0.8 - 0.6 * math.exp(-0.3 * 0)

LANES = 128
SC_CORES = 2
SC_SUBCORES = 16
SC_WORKERS = SC_CORES * SC_SUBCORES
SC_CHUNK = 128

ROW_TILE = 256
NEG_BIG = -1e30
VMEM_LIMIT = 56 * 1024 * 1024


def _cparams(sem):
    return pltpu.CompilerParams(dimension_semantics=sem, vmem_limit_bytes=VMEM_LIMIT)


def _rms(x, eps):
    return x * lax.rsqrt(jnp.mean(x * x, axis=-1, keepdims=True) + eps)


def _pack_bf16_pairs(x):
    n = x.shape[1] // 2
    bits = lax.bitcast_convert_type(x.astype(BF16).astype(F32), I32)
    lo = lax.shift_right_logical(bits[:, :n], 16)
    return lo | bits[:, n:]


def _unpack_bf16_pairs(p):
    lo = lax.bitcast_convert_type(lax.shift_left(p, 16), F32)
    hi = lax.bitcast_convert_type(p & jnp.int32(-65536), F32)
    return jnp.concatenate([lo, hi], axis=1)


def _ada_kernel(c_ref, w_ref, b_ref, o_ref):
    ca = jax.nn.silu(c_ref[...])
    o_ref[...] = jnp.dot(ca.astype(BF16), w_ref[...].astype(BF16), preferred_element_type=F32) + b_ref[...]


def _ada(c, w_ada, b_ada):
    bsz, d = c.shape
    n = w_ada.shape[1]
    tn = n // 4
    return pl.pallas_call(
        _ada_kernel,
        grid=(n // tn,),
        in_specs=[pl.BlockSpec((bsz, d), lambda j: (0, 0)),
                  pl.BlockSpec((d, tn), lambda j: (0, j)),
                  pl.BlockSpec((1, tn), lambda j: (0, j))],
        out_specs=pl.BlockSpec((bsz, tn), lambda j: (0, j)),
        out_shape=jax.ShapeDtypeStruct((bsz, n), F32),
        compiler_params=_cparams(("arbitrary",)),
        name="ada",
    )(c, w_ada, b_ada.reshape(1, n))


def _inproj_kernel(tiles_per_seq, x_ref, pos_ref, ada_ref, g1_ref, win_ref, convw_ref, gconv_ref, invf_ref, sgn_ref,
                   yconv_ref, qt_ref, k_ref, vt_ref, ubuf):
    i = pl.program_id(0)
    tm = x_ref.shape[0]
    x = x_ref[...]
    ada = ada_ref[0]
    sh1, sc1 = ada[0:1], ada[1:2]
    hn = _rms(x, NORM_EPS) * g1_ref[...] * (1.0 + sc1) + sh1
    hb = hn.astype(BF16)

    pc = jnp.dot(hb, win_ref[:, 0:3 * D_CONV], preferred_element_type=F32)
    u = pc[:, 2 * D_CONV:3 * D_CONV] * pc[:, 0:D_CONV]
    first = (i % tiles_per_seq) == 0

    @pl.when(first)
    def _():
        ubuf[0:8, :] = jnp.zeros((8, D_CONV), F32)

    @pl.when(jnp.logical_not(first))
    def _():
        ubuf[0:8, :] = ubuf[tm:tm + 8, :]

    ubuf[8:8 + tm, :] = u
    u1 = ubuf[7:7 + tm, :]
    u2 = ubuf[6:6 + tm, :]
    cw = convw_ref[...]
    yc = pc[:, D_CONV:2 * D_CONV] * (cw[0:1] * u2 + cw[1:2] * u1 + cw[2:3] * u)
    yconv_ref[...] = (_rms(yc, NORM_EPS) * gconv_ref[...]).astype(BF16)

    ang = pos_ref[...].astype(F32) * invf_ref[...]
    cos = jnp.cos(ang)
    sin = jnp.sin(ang)
    s_up = sin * sgn_ref[0:1]
    s_dn = sin * sgn_ref[1:2]
    half = ROT_DIM // 2

    def rope(t):
        return t * cos + pltpu.roll(t, LANES - half, axis=1) * s_up + pltpu.roll(t, half, axis=1) * s_dn

    pq = jnp.dot(hb, win_ref[:, 3 * D_CONV:3 * D_CONV + D_QK], preferred_element_type=F32)
    pk = jnp.dot(hb, win_ref[:, 3 * D_CONV + D_QK:3 * D_CONV + 2 * D_QK], preferred_element_type=F32)
    scale = HEAD_DIM ** -0.5
    qs = []
    for h in range(N_HEADS):
        sl = slice(h * LANES, (h + 1) * LANES)
        qs.append(rope(pq[:, sl]) * scale)
        k_ref[:, sl] = rope(pk[:, sl]).astype(BF16)
    qt_ref[0] = jnp.concatenate(qs, axis=1).T.astype(BF16)
    pv = jnp.dot(hb, win_ref[:, 3 * D_CONV + 2 * D_QK:], preferred_element_type=F32)
    vt_ref[0] = pv.T.astype(BF16)


def _inproj(xf, pos, ada3, g1, win_b, conv_w, g_conv, invf, sgn, seq):
    t, d = xf.shape
    tm = min(512, seq)
    tiles_per_seq = seq // tm
    n_in = win_b.shape[1]
    full = lambda shape: pl.BlockSpec(shape, lambda i: (0,) * len(shape))
    row = lambda w: pl.BlockSpec((tm, w), lambda i: (i, 0))
    colmajor = lambda w: pl.BlockSpec((1, w, tm), lambda i: (i // tiles_per_seq, 0, i % tiles_per_seq))
    return pl.pallas_call(
        functools.partial(_inproj_kernel, tiles_per_seq),
        grid=(t // tm,),
        in_specs=[row(d), row(1),
                  pl.BlockSpec((1, 6, d), lambda i: (i // tiles_per_seq, 0, 0)),
                  full((1, d)), full((d, n_in)), full((CONV_WIDTH, D_CONV)), full((1, D_CONV)),
                  full((1, LANES)), full((2, LANES))],
        out_specs=[row(D_CONV), colmajor(D_QK), row(D_QK), colmajor(D_ATTN)],
        out_shape=[jax.ShapeDtypeStruct((t, D_CONV), BF16), jax.ShapeDtypeStruct((t // seq, D_QK, seq), BF16),
                   jax.ShapeDtypeStruct((t, D_QK), BF16), jax.ShapeDtypeStruct((t // seq, D_ATTN, seq), BF16)],
        scratch_shapes=[pltpu.VMEM((tm + 8, D_CONV), F32)],
        compiler_params=_cparams(("arbitrary",)),
        name="inproj",
    )(xf, pos, ada3, g1, win_b, conv_w, g_conv, invf, sgn)


def _attn_kernel(qt_ref, k_ref, vt_ref, lamp_ref, gs_ref, o_ref, qq_ref, m_ref, l_ref, acc_ref):
    i = pl.program_id(2)
    tq = qt_ref.shape[2]
    tk = tq
    qt = qt_ref[0]
    feat = lax.broadcasted_iota(I32, qt.shape, 0)
    zero = jnp.zeros_like(qt)
    qq_ref[:, 0:tq] = jnp.where(feat < HEAD_DIM, qt, zero)
    qq_ref[:, tq:2 * tq] = jnp.where(feat >= HEAD_DIM, qt, zero)
    m_ref[...] = jnp.full(m_ref.shape, NEG_BIG, F32)
    l_ref[...] = jnp.zeros(l_ref.shape, F32)
    acc_ref[...] = jnp.zeros(acc_ref.shape, F32)

    def step(j, masked):
        start = pl.multiple_of(j * tk, tk)
        kb = k_ref[pl.ds(start, tk), :]
        vtb = vt_ref[0, :, pl.ds(start, tk)]
        s = jnp.dot(kb, qq_ref[...], preferred_element_type=F32)
        if masked:
            key = lax.broadcasted_iota(I32, s.shape, 0)
            col = lax.broadcasted_iota(I32, s.shape, 1)
            qpos = jnp.where(col >= tq, col - tq, col)
            s = jnp.where(key <= qpos, s, NEG_BIG)
        m_prev = m_ref[...]
        m_new = jnp.maximum(m_prev, jnp.max(s, axis=0, keepdims=True))
        alpha = jnp.exp(m_prev - m_new)
        p = jnp.exp(s - m_new)
        l_ref[...] = alpha * l_ref[...] + jnp.sum(p, axis=0, keepdims=True)
        acc_ref[...] = alpha * acc_ref[...] + jnp.dot(vtb, p.astype(BF16), preferred_element_type=F32)
        m_ref[...] = m_new

    def body(j, carry):
        step(j, False)
        return carry

    lax.fori_loop(0, i, body, 0)
    step(i, True)

    o = acc_ref[...] / l_ref[...]
    lamp = lamp_ref[...]
    lam = (jnp.exp(jnp.sum(lamp[0:1] * lamp[1:2], axis=1, keepdims=True))
           - jnp.exp(jnp.sum(lamp[2:3] * lamp[3:4], axis=1, keepdims=True)) + LAMBDA_INIT)
    od = o[:, 0:tq] - lam * o[:, tq:2 * tq]
    y = od * lax.rsqrt(jnp.mean(od * od, axis=0, keepdims=True) + SUBLN_EPS) * gs_ref[...] * (1.0 - LAMBDA_INIT)
    o_ref[...] = y.T.astype(BF16)


def _attention(qt, k, vt, lamp, g_subln, bsz, seq):
    t = k.shape[0]
    tq = min(512, seq)
    nq = seq // tq
    return pl.pallas_call(
        _attn_kernel,
        grid=(bsz, N_HEADS, nq),
        in_specs=[pl.BlockSpec((1, LANES, tq), lambda b, h, i: (b, h, i)),
                  pl.BlockSpec((seq, LANES), lambda b, h, i: (b, h)),
                  pl.BlockSpec((1, V_DIM, seq), lambda b, h, i: (b, h, 0)),
                  pl.BlockSpec((4, HEAD_DIM), lambda b, h, i: (0, 0)),
                  pl.BlockSpec((V_DIM, 1), lambda b, h, i: (0, 0))],
        out_specs=pl.BlockSpec((tq, LANES), lambda b, h, i: (b * nq + i, h)),
        out_shape=jax.ShapeDtypeStruct((t, D_ATTN), BF16),
        scratch_shapes=[pltpu.VMEM((LANES, 2 * tq), BF16), pltpu.VMEM((1, 2 * tq), F32),
                        pltpu.VMEM((1, 2 * tq), F32), pltpu.VMEM((V_DIM, 2 * tq), F32)],
        compiler_params=_cparams(("arbitrary", "arbitrary", "arbitrary")),
        name="attention",
    )(qt, k, vt, lamp, g_subln)


def _first_index(hit, idx, size, axis):
    return jnp.min(jnp.where(hit, idx, size), axis=axis, keepdims=True)


def _outproj_router_kernel(yc_ref, ya_ref, x_ref, ada_ref, g2_ref, wout_ref, wrt_ref, rb_ref,
                           x1_ref, hp_ref, idx_ref, wts_ref, rank_ref, cnt_ref, base_ref):
    i = pl.program_id(0)
    tm = x_ref.shape[0]
    ada = ada_ref[0]
    gt1, sh2, sc2 = ada[2:3], ada[3:4], ada[4:5]
    mix = (jnp.dot(yc_ref[...], wout_ref[0:D_CONV, :], preferred_element_type=F32)
           + jnp.dot(ya_ref[...], wout_ref[D_CONV:, :], preferred_element_type=F32))
    x1 = x_ref[...] + gt1 * mix
    x1_ref[...] = x1
    hn2 = _rms(x1, NORM_EPS) * g2_ref[...] * (1.0 + sc2) + sh2
    hp_ref[...] = _pack_bf16_pairs(hn2)
    hb = hn2.astype(BF16)

    logits = lax.dot_general(wrt_ref[...], hb, (((1,), (1,)), ((), ())), preferred_element_type=F32)
    scores = jax.nn.sigmoid(logits)
    choice = scores + rb_ref[...]
    neg_inf = jnp.float32(-jnp.inf)

    ch3 = choice.reshape(N_GROUPS, GROUP_SIZE, tm)
    i3 = lax.broadcasted_iota(I32, ch3.shape, 1)
    m1 = jnp.max(ch3, axis=1, keepdims=True)
    f1 = _first_index(ch3 == m1, i3, GROUP_SIZE, 1)
    m2 = jnp.max(jnp.where(i3 == f1, neg_inf, ch3), axis=1, keepdims=True)
    gs = (m1 + m2).reshape(N_GROUPS, tm)

    gi = lax.broadcasted_iota(I32, gs.shape, 0)
    gkeep = jnp.zeros(gs.shape, F32)
    for _ in range(TOPK_GROUPS):
        m = jnp.max(gs, axis=0, keepdims=True)
        f = _first_index(gs == m, gi, N_GROUPS, 0)
        sel = gi == f
        gkeep = jnp.where(sel, 1.0, gkeep)
        gs = jnp.where(sel, neg_inf, gs)
    ekeep = jnp.broadcast_to(gkeep.reshape(N_GROUPS, 1, tm), (N_GROUPS, GROUP_SIZE, tm)).reshape(N_EXPERTS, tm)
    masked = jnp.where(ekeep > 0.0, choice, neg_inf)

    ei = lax.broadcasted_iota(I32, masked.shape, 0)
    picked = jnp.zeros(masked.shape, F32)
    idxs, ws = [], []
    for _ in range(TOP_K):
        m = jnp.max(masked, axis=0, keepdims=True)
        f = _first_index(masked == m, ei, N_EXPERTS, 0)
        sel = ei == f
        idxs.append(f)
        ws.append(jnp.sum(jnp.where(sel, scores, 0.0), axis=0, keepdims=True))
        picked = jnp.where(sel, 1.0, picked)
        masked = jnp.where(sel, neg_inf, masked)
    wsum = ws[0]
    for wk in ws[1:]:
        wsum = wsum + wk
    denom = wsum + 1e-20
    for kk in range(TOP_K):
        idx_ref[kk:kk + 1, :] = idxs[kk]
        wts_ref[kk:kk + 1, :] = ws[kk] / denom * ROUTED_SCALE

    @pl.when(i == 0)
    def _():
        base_ref[...] = jnp.zeros(base_ref.shape, F32)

    si = lax.broadcasted_iota(I32, (tm, tm), 0)
    ti = lax.broadcasted_iota(I32, (tm, tm), 1)
    earlier = jnp.where(si < ti, 1.0, 0.0).astype(BF16)
    before = jnp.dot(picked.astype(BF16), earlier, preferred_element_type=F32) + base_ref[...]
    for kk in range(TOP_K):
        rank_ref[kk:kk + 1, :] = jnp.sum(jnp.where(ei == idxs[kk], before, 0.0), axis=0, keepdims=True).astype(I32)
    base_ref[...] = base_ref[...] + jnp.sum(picked, axis=1, keepdims=True)
    cnt_ref[...] = base_ref[...]


def _outproj_router(yconv, yattn, xf, ada3, g2, wout_b, wrt_b, rbias, seq):
    t, d = xf.shape
    tm = min(512, seq)
    tiles_per_seq = seq // tm
    full = lambda shape: pl.BlockSpec(shape, lambda i: (0,) * len(shape))
    row = lambda w: pl.BlockSpec((tm, w), lambda i: (i, 0))
    col = pl.BlockSpec((TOP_K, tm), lambda i: (0, i))
    return pl.pallas_call(
        _outproj_router_kernel,
        grid=(t // tm,),
        in_specs=[row(D_CONV), row(D_ATTN), row(d),
                  pl.BlockSpec((1, 6, d), lambda i: (i // tiles_per_seq, 0, 0)),
                  full((1, d)), full((D_CONV + D_ATTN, d)), full((N_EXPERTS, d)), full((N_EXPERTS, 1))],
        out_specs=[row(d), row(d // 2), col, col, col, full((N_EXPERTS, 1))],
        out_shape=[jax.ShapeDtypeStruct((t, d), F32), jax.ShapeDtypeStruct((t, d // 2), I32),
                   jax.ShapeDtypeStruct((TOP_K, t), I32), jax.ShapeDtypeStruct((TOP_K, t), F32),
                   jax.ShapeDtypeStruct((TOP_K, t), I32), jax.ShapeDtypeStruct((N_EXPERTS, 1), F32)],
        scratch_shapes=[pltpu.VMEM((N_EXPERTS, 1), F32)],
        compiler_params=_cparams(("arbitrary",)),
        name="outproj_router",
    )(yconv, yattn, xf, ada3, g2, wout_b, wrt_b, rbias)


def _dest_kernel(idx_ref, rank_ref, ps_ref, dest_ref):
    tm = idx_ref.shape[1]
    ei = lax.broadcasted_iota(I32, (N_EXPERTS, tm), 0)
    ps = ps_ref[...]
    for kk in range(TOP_K):
        start = jnp.sum(jnp.where(ei == idx_ref[kk:kk + 1, :], ps, 0.0), axis=0, keepdims=True)
        dest_ref[kk:kk + 1, :] = start.astype(I32) + rank_ref[kk:kk + 1, :]


def _dest(idx_t, rank_t, pad_start_f):
    t = idx_t.shape[1]
    tm = min(2048, t)
    col = pl.BlockSpec((TOP_K, tm), lambda i: (0, i))
    return pl.pallas_call(
        _dest_kernel,
        grid=(t // tm,),
        in_specs=[col, col, pl.BlockSpec((N_EXPERTS, 1), lambda i: (0, 0))],
        out_specs=col,
        out_shape=jax.ShapeDtypeStruct((TOP_K, t), I32),
        compiler_params=_cparams(("arbitrary",)),
        name="dest",
    )(idx_t, rank_t, pad_start_f)


def _sc_dispatch(hp, dest_ck, n_rows):
    t, w = hp.shape
    nk = dest_ck.shape[1]
    per_worker = t // SC_CHUNK // SC_WORKERS
    mesh = plsc.VectorSubcoreMesh(core_axis_name="c", subcore_axis_name="s")

    @functools.partial(
        pl.kernel, mesh=mesh,
        out_type=jax.ShapeDtypeStruct((n_rows, w), hp.dtype),
        scratch_types=[pltpu.VMEM((nk, SC_CHUNK), I32), pltpu.VMEM((SC_CHUNK, w), hp.dtype), pltpu.SemaphoreType.DMA],
    )
    def k(hp_hbm, dest_hbm, xs_hbm, idx_v, rows_v, sem):
        wid = lax.axis_index("s") * SC_CORES + lax.axis_index("c")

        @pl.loop(0, per_worker)
        def _(j):
            c = wid * per_worker + j
            pltpu.sync_copy(dest_hbm.at[c], idx_v)
            pltpu.sync_copy(hp_hbm.at[pl.ds(c * SC_CHUNK, SC_CHUNK)], rows_v)
            for kk in range(nk):
                pltpu.async_copy(rows_v, xs_hbm.at[idx_v.at[kk]], sem)
            for kk in range(nk):
                pltpu.make_async_copy(rows_v, xs_hbm.at[idx_v.at[kk]], sem).wait()

    return k(hp, dest_ck)


def _sc_gather(src, idx_c):
    n_chunks = idx_c.shape[0]
    w = src.shape[1]
    per_worker = n_chunks // SC_WORKERS
    mesh = plsc.VectorSubcoreMesh(core_axis_name="c", subcore_axis_name="s")

    @functools.partial(
        pl.kernel, mesh=mesh,
        out_type=jax.ShapeDtypeStruct((n_chunks * SC_CHUNK, w), src.dtype),
        scratch_types=[pltpu.VMEM((1, SC_CHUNK), I32), pltpu.VMEM((SC_CHUNK, w), src.dtype), pltpu.SemaphoreType.DMA],
    )
    def k(src_hbm, idx_hbm, out_hbm, idx_v, rows_v, sem):
        wid = lax.axis_index("s") * SC_CORES + lax.axis_index("c")

        @pl.loop(0, per_worker)
        def _(j):
            c = wid * per_worker + j
            pltpu.sync_copy(idx_hbm.at[c], idx_v)
            pltpu.async_copy(src_hbm.at[idx_v.at[0]], rows_v, sem).wait()
            pltpu.sync_copy(rows_v, out_hbm.at[pl.ds(c * SC_CHUNK, SC_CHUNK)])

    return k(src, idx_c)


def _experts_kernel(bstart_ref, xs_hbm, wg_ref, wu_ref, wd_ref, out_hbm,
                    wgu_b, wd_b, xbuf, obuf, sem_in, sem_out):
    e = pl.program_id(0)
    b0 = bstart_ref[e]
    b1 = bstart_ref[e + 1]
    n_used = bstart_ref[N_EXPERTS]

    def in_copy(g, slot):
        return pltpu.make_async_copy(xs_hbm.at[pl.ds(g * ROW_TILE, ROW_TILE)], xbuf.at[slot], sem_in.at[slot])

    def out_copy(g, slot):
        return pltpu.make_async_copy(obuf.at[slot], out_hbm.at[pl.ds(g * ROW_TILE, ROW_TILE)], sem_out.at[slot])

    @pl.when(e == 0)
    def _():
        in_copy(0, 0).start()

    @pl.when(b1 > b0)
    def _():
        wgu_b[:, 0:D_EXPERT] = wg_ref[...].astype(BF16)
        wgu_b[:, D_EXPERT:2 * D_EXPERT] = wu_ref[...].astype(BF16)
        wd_b[...] = wd_ref[...].astype(BF16)

        def body(g, carry):
            slot = g & 1
            in_copy(g, slot).wait()

            @pl.when(g + 1 < n_used)
            def _():
                in_copy(g + 1, 1 - slot).start()

            @pl.when(g >= 2)
            def _():
                out_copy(g - 2, slot).wait()

            xb = _unpack_bf16_pairs(xbuf[slot]).astype(BF16)
            gu = jnp.dot(xb, wgu_b[...], preferred_element_type=F32)
            act = (jax.nn.silu(gu[:, 0:D_EXPERT]) * gu[:, D_EXPERT:2 * D_EXPERT]).astype(BF16)
            obuf[slot] = _pack_bf16_pairs(jnp.dot(act, wd_b[...], preferred_element_type=F32))
            out_copy(g, slot).start()
            return carry

        lax.fori_loop(b0, b1, body, 0)

    @pl.when(e == N_EXPERTS - 1)
    def _():
        last = n_used - 1
        out_copy(last, last & 1).wait()

        @pl.when(n_used >= 2)
        def _():
            out_copy(last - 1, (last - 1) & 1).wait()


def _experts(xs, bstart, w_gate, w_up, w_down):
    n_rows, half = xs.shape
    d = 2 * half
    weights = lambda shape: pl.BlockSpec((None,) + shape, lambda e, bs: (e, 0, 0))
    return pl.pallas_call(
        _experts_kernel,
        grid_spec=pltpu.PrefetchScalarGridSpec(
            num_scalar_prefetch=1,
            grid=(N_EXPERTS,),
            in_specs=[pl.BlockSpec(memory_space=pl.ANY),
                      weights((d, D_EXPERT)), weights((d, D_EXPERT)), weights((D_EXPERT, d))],
            out_specs=pl.BlockSpec(memory_space=pl.ANY),
            scratch_shapes=[pltpu.VMEM((d, 2 * D_EXPERT), BF16), pltpu.VMEM((D_EXPERT, d), BF16),
                            pltpu.VMEM((2, ROW_TILE, half), I32), pltpu.VMEM((2, ROW_TILE, half), I32),
                            pltpu.SemaphoreType.DMA((2,)), pltpu.SemaphoreType.DMA((2,))],
        ),
        out_shape=jax.ShapeDtypeStruct((n_rows, half), I32),
        compiler_params=_cparams(("arbitrary",)),
        name="experts",
    )(bstart, xs, w_gate, w_up, w_down)


def _final_kernel(x1_ref, hp_ref, ou_ref, wts_ref, ada_ref, wgs_ref, wus_ref, wds_ref, gf_ref, o_ref):
    nchunk = ou_ref.shape[0]
    gt2 = ada_ref[0][5:6]
    hb = _unpack_bf16_pairs(hp_ref[...]).astype(BF16)
    g = jnp.dot(hb, wgs_ref[...], preferred_element_type=F32)
    u = jnp.dot(hb, wus_ref[...], preferred_element_type=F32)
    y = jnp.dot((jax.nn.silu(g) * u).astype(BF16), wds_ref[...], preferred_element_type=F32)
    wts = wts_ref[...]
    parts = []
    for c in range(nchunk):
        rs = slice(c * SC_CHUNK, (c + 1) * SC_CHUNK)
        yr = wts[rs, 0:1] * _unpack_bf16_pairs(ou_ref[c, 0])
        for kk in range(1, TOP_K):
            yr = yr + wts[rs, kk:kk + 1] * _unpack_bf16_pairs(ou_ref[c, kk])
        parts.append(yr)
    y = y + (parts[0] if nchunk == 1 else jnp.concatenate(parts, axis=0))
    x2 = x1_ref[...] + gt2 * y
    o_ref[...] = _rms(x2, NORM_EPS) * gf_ref[...]


def _final(x1, hp, outu4, wts, ada3, wgs_b, wus_b, wds_b, g_final, seq):
    t, d = x1.shape
    tm = min(256, seq)
    nchunk = tm // SC_CHUNK
    tiles_per_seq = seq // tm
    full = lambda shape: pl.BlockSpec(shape, lambda i: (0,) * len(shape))
    row = lambda w: pl.BlockSpec((tm, w), lambda i: (i, 0))
    return pl.pallas_call(
        _final_kernel,
        grid=(t // tm,),
        in_specs=[row(d), row(d // 2),
                  pl.BlockSpec((nchunk, TOP_K, SC_CHUNK, d // 2), lambda i: (i, 0, 0, 0)),
                  row(TOP_K),
                  pl.BlockSpec((1, 6, d), lambda i: (i // tiles_per_seq, 0, 0)),
                  full((d, D_EXPERT)), full((d, D_EXPERT)), full((D_EXPERT, d)), full((1, d))],
        out_specs=row(d),
        out_shape=jax.ShapeDtypeStruct((t, d), F32),
        compiler_params=_cparams(("arbitrary",)),
        name="final",
    )(x1, hp, outu4, wts, ada3, wgs_b, wus_b, wds_b, g_final)


def kernel(x, c, positions, w_ada, b_ada, g_norm1, w_in, conv_w, g_conv_out, lam_q1, lam_k1, lam_q2, lam_k2, g_subln, w_out, g_norm2, w_router, router_bias, w_gate_e, w_up_e, w_down_e, w_gate_s, w_up_s, w_down_s, g_final):
    bsz, seq, d = x.shape
    t = bsz * seq
    xf = x.reshape(t, d)
    pos = positions.reshape(t, 1)

    inv_freq = ROPE_THETA ** (-jnp.arange(0, ROT_DIM, 2, dtype=F32) / ROT_DIM)
    half = ROT_DIM // 2
    comp = jnp.concatenate([inv_freq, inv_freq, jnp.zeros((HEAD_DIM - ROT_DIM,), F32)])
    invf = jnp.concatenate([comp, comp]).reshape(1, LANES)
    one = jnp.ones((half,), F32)
    zero = jnp.zeros((half,), F32)
    rest = jnp.zeros((HEAD_DIM - ROT_DIM,), F32)
    up = jnp.concatenate([-one, zero, rest])
    dn = jnp.concatenate([zero, one, rest])
    sgn = jnp.stack([jnp.concatenate([up, up]), jnp.concatenate([dn, dn])])

    ada3 = _ada(c, w_ada[0], b_ada[0]).reshape(bsz, 6, d)
    yconv, qt, k, vt = _inproj(xf, pos, ada3, g_norm1[0].reshape(1, d), w_in[0].astype(BF16), conv_w[0],
                             g_conv_out[0].reshape(1, D_CONV), invf, sgn, seq)
    lamp = jnp.stack([lam_q1[0], lam_k1[0], lam_q2[0], lam_k2[0]]).astype(F32)
    yattn = _attention(qt, k, vt, lamp, g_subln[0].reshape(V_DIM, 1), bsz, seq)

    x1, hp, idx_t, wts_t, rank_t, counts_f = _outproj_router(
        yconv, yattn, xf, ada3, g_norm2[0].reshape(1, d), w_out[0].astype(BF16),
        w_router[0].T.astype(BF16), router_bias[0].reshape(N_EXPERTS, 1), seq)

    counts = counts_f[:, 0].astype(I32)
    padded = ((counts + ROW_TILE - 1) // ROW_TILE) * ROW_TILE
    pad_end = jnp.cumsum(padded)
    pad_start = pad_end - padded
    nb = (t * TOP_K + N_EXPERTS * (ROW_TILE - 1)) // ROW_TILE
    bstart = (jnp.concatenate([pad_start, pad_end[-1:]]) // ROW_TILE).astype(I32)

    dest_t = _dest(idx_t, rank_t, pad_start.astype(F32).reshape(N_EXPERTS, 1))
    n_chunks = t // SC_CHUNK
    dest_ck = dest_t.reshape(TOP_K, n_chunks, SC_CHUNK).transpose(1, 0, 2)

    xs = _sc_dispatch(hp, dest_ck, nb * ROW_TILE)
    outs = _experts(xs, bstart, w_gate_e[0], w_up_e[0], w_down_e[0])
    outu = _sc_gather(outs, dest_ck.reshape(n_chunks * TOP_K, 1, SC_CHUNK))

    out = _final(x1, hp, outu.reshape(n_chunks, TOP_K, SC_CHUNK, d // 2), wts_t.T, ada3,
                 w_gate_s[0].astype(BF16), w_up_s[0].astype(BF16), w_down_s[0].astype(BF16),
                 g_final.reshape(1, d), seq)
    return out.reshape(bsz, seq, d)
```

```python
import functools
import math

import jax
import jax.numpy as jnp
from jax import lax
from jax.experimental import pallas as pl
from jax.experimental.pallas import tpu as pltpu
from jax.experimental.pallas import tpu_sc as plsc

F32 = jnp.float32
BF16 = jnp.bfloat16
I32 = jnp.int32

D_CONV = 512
CONV_WIDTH = 3
N_HEADS = 4
HEAD_DIM = 64
V_DIM = 2 * HEAD_DIM
D_ATTN = N_HEADS * V_DIM
D_QK = N_HEADS * 2 * HEAD_DIM
ROT_DIM = HEAD_DIM // 4
ROPE_THETA = 500000.0
N_EXPERTS = 256
TOP_K = 8
N_GROUPS = 8
GROUP_SIZE = N_EXPERTS // N_GROUPS
TOPK_GROUPS = 4
D_EXPERT = 256
ROUTED_SCALE = 2.5
NORM_EPS = 1e-6
SUBLN_EPS = 1e-5
LAMBDA_INIT = 0.8 - 0.6 * math.exp(-0.3 * 0)

LANES = 128
SC_CORES = 2
SC_SUBCORES = 16
SC_WORKERS = SC_CORES * SC_SUBCORES
SC_CHUNK = 128

ROW_TILE = 256
EXPERT_BUFS = 4
NEG_BIG = -1e30
VMEM_LIMIT = 56 * 1024 * 1024


def _cparams(sem):
    return pltpu.CompilerParams(dimension_semantics=sem, vmem_limit_bytes=VMEM_LIMIT)


def _rms(x, eps):
    return x * lax.rsqrt(jnp.mean(x * x, axis=-1, keepdims=True) + eps)


def _pack_bf16_pairs(x):
    n = x.shape[1] // 2
    bits = lax.bitcast_convert_type(x.astype(BF16).astype(F32), I32)
    lo = lax.shift_right_logical(bits[:, :n], 16)
    return lo | bits[:, n:]


def _unpack_bf16_pairs(p):
    lo = lax.bitcast_convert_type(lax.shift_left(p, 16), F32)
    hi = lax.bitcast_convert_type(p & jnp.int32(-65536), F32)
    return jnp.concatenate([lo, hi], axis=1)


def _ada_kernel(c_ref, w_ref, b_ref, o_ref):
    ca = jax.nn.silu(c_ref[...])
    o_ref[...] = jnp.dot(ca.astype(BF16), w_ref[...].astype(BF16), preferred_element_type=F32) + b_ref[...]


def _ada(c, w_ada, b_ada):
    bsz, d = c.shape
    n = w_ada.shape[1]
    tn = n // 4
    return pl.pallas_call(
        _ada_kernel,
        grid=(n // tn,),
        in_specs=[pl.BlockSpec((bsz, d), lambda j: (0, 0)),
                  pl.BlockSpec((d, tn), lambda j: (0, j)),
                  pl.BlockSpec((1, tn), lambda j: (0, j))],
        out_specs=pl.BlockSpec((bsz, tn), lambda j: (0, j)),
        out_shape=jax.ShapeDtypeStruct((bsz, n), F32),
        compiler_params=_cparams(("arbitrary",)),
        name="ada",
    )(c, w_ada, b_ada.reshape(1, n))


def _inproj_kernel(tiles_per_seq, x_ref, pos_ref, ada_ref, g1_ref, win_ref, convw_ref, gconv_ref, invf_ref, sgn_ref,
                   yconv_ref, qt_ref, k_ref, vt_ref, ubuf):
    i = pl.program_id(0)
    tm = x_ref.shape[0]
    x = x_ref[...]
    ada = ada_ref[0]
    sh1, sc1 = ada[0:1], ada[1:2]
    hn = _rms(x, NORM_EPS) * g1_ref[...] * (1.0 + sc1) + sh1
    hb = hn.astype(BF16)

    pc = jnp.dot(hb, win_ref[:, 0:3 * D_CONV], preferred_element_type=F32)
    u = pc[:, 2 * D_CONV:3 * D_CONV] * pc[:, 0:D_CONV]
    first = (i % tiles_per_seq) == 0

    @pl.when(first)
    def _():
        ubuf[0:8, :] = jnp.zeros((8, D_CONV), F32)

    @pl.when(jnp.logical_not(first))
    def _():
        ubuf[0:8, :] = ubuf[tm:tm + 8, :]

    ubuf[8:8 + tm, :] = u
    u1 = ubuf[7:7 + tm, :]
    u2 = ubuf[6:6 + tm, :]
    cw = convw_ref[...]
    yc = pc[:, D_CONV:2 * D_CONV] * (cw[0:1] * u2 + cw[1:2] * u1 + cw[2:3] * u)
    yconv_ref[...] = (_rms(yc, NORM_EPS) * gconv_ref[...]).astype(BF16)

    ang = pos_ref[...].astype(F32) * invf_ref[...]
    cos = jnp.cos(ang)
    sin = jnp.sin(ang)
    s_up = sin * sgn_ref[0:1]
    s_dn = sin * sgn_ref[1:2]
    half = ROT_DIM // 2

    def rope(t):
        return t * cos + pltpu.roll(t, LANES - half, axis=1) * s_up + pltpu.roll(t, half, axis=1) * s_dn

    pq = jnp.dot(hb, win_ref[:, 3 * D_CONV:3 * D_CONV + D_QK], preferred_element_type=F32)
    pk = jnp.dot(hb, win_ref[:, 3 * D_CONV + D_QK:3 * D_CONV + 2 * D_QK], preferred_element_type=F32)
    scale = HEAD_DIM ** -0.5
    qs = []
    for h in range(N_HEADS):
        sl = slice(h * LANES, (h + 1) * LANES)
        qs.append(rope(pq[:, sl]) * scale)
        k_ref[:, sl] = rope(pk[:, sl]).astype(BF16)
    qt_ref[0] = jnp.concatenate(qs, axis=1).T.astype(BF16)
    pv = jnp.dot(hb, win_ref[:, 3 * D_CONV + 2 * D_QK:], preferred_element_type=F32)
    vt_ref[0] = pv.T.astype(BF16)


def _inproj(xf, pos, ada3, g1, win_b, conv_w, g_conv, invf, sgn, seq):
    t, d = xf.shape
    tm = min(512, seq)
    tiles_per_seq = seq // tm
    n_in = win_b.shape[1]
    full = lambda shape: pl.BlockSpec(shape, lambda i: (0,) * len(shape))
    row = lambda w: pl.BlockSpec((tm, w), lambda i: (i, 0))
    colmajor = lambda w: pl.BlockSpec((1, w, tm), lambda i: (i // tiles_per_seq, 0, i % tiles_per_seq))
    return pl.pallas_call(
        functools.partial(_inproj_kernel, tiles_per_seq),
        grid=(t // tm,),
        in_specs=[row(d), row(1),
                  pl.BlockSpec((1, 6, d), lambda i: (i // tiles_per_seq, 0, 0)),
                  full((1, d)), full((d, n_in)), full((CONV_WIDTH, D_CONV)), full((1, D_CONV)),
                  full((1, LANES)), full((2, LANES))],
        out_specs=[row(D_CONV), colmajor(D_QK), row(D_QK), colmajor(D_ATTN)],
        out_shape=[jax.ShapeDtypeStruct((t, D_CONV), BF16), jax.ShapeDtypeStruct((t // seq, D_QK, seq), BF16),
                   jax.ShapeDtypeStruct((t, D_QK), BF16), jax.ShapeDtypeStruct((t // seq, D_ATTN, seq), BF16)],
        scratch_shapes=[pltpu.VMEM((tm + 8, D_CONV), F32)],
        compiler_params=_cparams(("arbitrary",)),
        name="inproj",
    )(xf, pos, ada3, g1, win_b, conv_w, g_conv, invf, sgn)


def _attn_kernel(qt_ref, k_ref, vt_ref, lamp_ref, gs_ref, o_ref, qq_ref, m_ref, l_ref, acc_ref):
    i = pl.program_id(2)
    tq = qt_ref.shape[2]
    tk = tq
    qt = qt_ref[0]
    feat = lax.broadcasted_iota(I32, qt.shape, 0)
    zero = jnp.zeros_like(qt)
    qq_ref[:, 0:tq] = jnp.where(feat < HEAD_DIM, qt, zero)
    qq_ref[:, tq:2 * tq] = jnp.where(feat >= HEAD_DIM, qt, zero)
    m_ref[...] = jnp.full(m_ref.shape, NEG_BIG, F32)
    l_ref[...] = jnp.zeros(l_ref.shape, F32)
    acc_ref[...] = jnp.zeros(acc_ref.shape, F32)

    def step(j, masked):
        start = pl.multiple_of(j * tk, tk)
        kb = k_ref[pl.ds(start, tk), :]
        vtb = vt_ref[0, :, pl.ds(start, tk)]
        s = jnp.dot(kb, qq_ref[...], preferred_element_type=F32)
        if masked:
            key = lax.broadcasted_iota(I32, s.shape, 0)
            col = lax.broadcasted_iota(I32, s.shape, 1)
            qpos = jnp.where(col >= tq, col - tq, col)
            s = jnp.where(key <= qpos, s, NEG_BIG)
        m_prev = m_ref[...]
        m_new = jnp.maximum(m_prev, jnp.max(s, axis=0, keepdims=True))
        alpha = jnp.exp(m_prev - m_new)
        p = jnp.exp(s - m_new)
        l_ref[...] = alpha * l_ref[...] + jnp.sum(p, axis=0, keepdims=True)
        acc_ref[...] = alpha * acc_ref[...] + jnp.dot(vtb, p.astype(BF16), preferred_element_type=F32)
        m_ref[...] = m_new

    def body(j, carry):
        step(j, False)
        return carry

    lax.fori_loop(0, i, body, 0)
    step(i, True)

    o = acc_ref[...] / l_ref[...]
    lamp = lamp_ref[...]
    lam = (jnp.exp(jnp.sum(lamp[0:1] * lamp[1:2], axis=1, keepdims=True))
           - jnp.exp(jnp.sum(lamp[2:3] * lamp[3:4], axis=1, keepdims=True)) + LAMBDA_INIT)
    od = o[:, 0:tq] - lam * o[:, tq:2 * tq]
    y = od * lax.rsqrt(jnp.mean(od * od, axis=0, keepdims=True) + SUBLN_EPS) * gs_ref[...] * (1.0 - LAMBDA_INIT)
    o_ref[...] = y.T.astype(BF16)


def _attention(qt, k, vt, lamp, g_subln, bsz, seq):
    t = k.shape[0]
    tq = min(512, seq)
    nq = seq // tq
    return pl.pallas_call(
        _attn_kernel,
        grid=(bsz, N_HEADS, nq),
        in_specs=[pl.BlockSpec((1, LANES, tq), lambda b, h, i: (b, h, i)),
                  pl.BlockSpec((seq, LANES), lambda b, h, i: (b, h)),
                  pl.BlockSpec((1, V_DIM, seq), lambda b, h, i: (b, h, 0)),
                  pl.BlockSpec((4, HEAD_DIM), lambda b, h, i: (0, 0)),
                  pl.BlockSpec((V_DIM, 1), lambda b, h, i: (0, 0))],
        out_specs=pl.BlockSpec((tq, LANES), lambda b, h, i: (b * nq + i, h)),
        out_shape=jax.ShapeDtypeStruct((t, D_ATTN), BF16),
        scratch_shapes=[pltpu.VMEM((LANES, 2 * tq), BF16), pltpu.VMEM((1, 2 * tq), F32),
                        pltpu.VMEM((1, 2 * tq), F32), pltpu.VMEM((V_DIM, 2 * tq), F32)],
        compiler_params=_cparams(("arbitrary", "arbitrary", "arbitrary")),
        name="attention",
    )(qt, k, vt, lamp, g_subln)


def _first_index(hit, idx, size, axis):
    return jnp.min(jnp.where(hit, idx, size), axis=axis, keepdims=True)


def _outproj_router_kernel(yc_ref, ya_ref, x_ref, ada_ref, g2_ref, wout_ref, wrt_ref, rb_ref,
                           x1_ref, hp_ref, idx_ref, wts_ref, rank_ref, cnt_ref, base_ref):
    i = pl.program_id(0)
    tm = x_ref.shape[0]
    ada = ada_ref[0]
    gt1, sh2, sc2 = ada[2:3], ada[3:4], ada[4:5]
    mix = (jnp.dot(yc_ref[...], wout_ref[0:D_CONV, :], preferred_element_type=F32)
           + jnp.dot(ya_ref[...], wout_ref[D_CONV:, :], preferred_element_type=F32))
    x1 = x_ref[...] + gt1 * mix
    x1_ref[...] = x1
    hn2 = _rms(x1, NORM_EPS) * g2_ref[...] * (1.0 + sc2) + sh2
    hp_ref[...] = _pack_bf16_pairs(hn2)
    hb = hn2.astype(BF16)

    logits = lax.dot_general(wrt_ref[...], hb, (((1,), (1,)), ((), ())), preferred_element_type=F32)
    scores = jax.nn.sigmoid(logits)
    choice = scores + rb_ref[...]
    neg_inf = jnp.float32(-jnp.inf)

    ch3 = choice.reshape(N_GROUPS, GROUP_SIZE, tm)
    i3 = lax.broadcasted_iota(I32, ch3.shape, 1)
    m1 = jnp.max(ch3, axis=1, keepdims=True)
    f1 = _first_index(ch3 == m1, i3, GROUP_SIZE, 1)
    m2 = jnp.max(jnp.where(i3 == f1, neg_inf, ch3), axis=1, keepdims=True)
    gs = (m1 + m2).reshape(N_GROUPS, tm)

    gi = lax.broadcasted_iota(I32, gs.shape, 0)
    gkeep = jnp.zeros(gs.shape, F32)
    for _ in range(TOPK_GROUPS):
        m = jnp.max(gs, axis=0, keepdims=True)
        f = _first_index(gs == m, gi, N_GROUPS, 0)
        sel = gi == f
        gkeep = jnp.where(sel, 1.0, gkeep)
        gs = jnp.where(sel, neg_inf, gs)
    ekeep = jnp.broadcast_to(gkeep.reshape(N_GROUPS, 1, tm), (N_GROUPS, GROUP_SIZE, tm)).reshape(N_EXPERTS, tm)
    masked = jnp.where(ekeep > 0.0, choice, neg_inf)

    ei = lax.broadcasted_iota(I32, masked.shape, 0)
    picked = jnp.zeros(masked.shape, F32)
    idxs, ws = [], []
    for _ in range(TOP_K):
        m = jnp.max(masked, axis=0, keepdims=True)
        f = _first_index(masked == m, ei, N_EXPERTS, 0)
        sel = ei == f
        idxs.append(f)
        ws.append(jnp.sum(jnp.where(sel, scores, 0.0), axis=0, keepdims=True))
        picked = jnp.where(sel, 1.0, picked)
        masked = jnp.where(sel, neg_inf, masked)
    wsum = ws[0]
    for wk in ws[1:]:
        wsum = wsum + wk
    denom = wsum + 1e-20
    for kk in range(TOP_K):
        idx_ref[kk:kk + 1, :] = idxs[kk]
        wts_ref[kk:kk + 1, :] = ws[kk] / denom * ROUTED_SCALE

    @pl.when(i == 0)
    def _():
        base_ref[...] = jnp.zeros(base_ref.shape, F32)

    si = lax.broadcasted_iota(I32, (tm, tm), 0)
    ti = lax.broadcasted_iota(I32, (tm, tm), 1)
    earlier = jnp.where(si < ti, 1.0, 0.0).astype(BF16)
    before = jnp.dot(picked.astype(BF16), earlier, preferred_element_type=F32) + base_ref[...]
    for kk in range(TOP_K):
        rank_ref[kk:kk + 1, :] = jnp.sum(jnp.where(ei == idxs[kk], before, 0.0), axis=0, keepdims=True).astype(I32)
    base_ref[...] = base_ref[...] + jnp.sum(picked, axis=1, keepdims=True)
    cnt_ref[...] = base_ref[...]


def _outproj_router(yconv, yattn, xf, ada3, g2, wout_b, wrt_b, rbias, seq):
    t, d = xf.shape
    tm = min(512, seq)
    tiles_per_seq = seq // tm
    full = lambda shape: pl.BlockSpec(shape, lambda i: (0,) * len(shape))
    row = lambda w: pl.BlockSpec((tm, w), lambda i: (i, 0))
    col = pl.BlockSpec((TOP_K, tm), lambda i: (0, i))
    return pl.pallas_call(
        _outproj_router_kernel,
        grid=(t // tm,),
        in_specs=[row(D_CONV), row(D_ATTN), row(d),
                  pl.BlockSpec((1, 6, d), lambda i: (i // tiles_per_seq, 0, 0)),
                  full((1, d)), full((D_CONV + D_ATTN, d)), full((N_EXPERTS, d)), full((N_EXPERTS, 1))],
        out_specs=[row(d), row(d // 2), col, col, col, full((N_EXPERTS, 1))],
        out_shape=[jax.ShapeDtypeStruct((t, d), F32), jax.ShapeDtypeStruct((t, d // 2), I32),
                   jax.ShapeDtypeStruct((TOP_K, t), I32), jax.ShapeDtypeStruct((TOP_K, t), F32),
                   jax.ShapeDtypeStruct((TOP_K, t), I32), jax.ShapeDtypeStruct((N_EXPERTS, 1), F32)],
        scratch_shapes=[pltpu.VMEM((N_EXPERTS, 1), F32)],
        compiler_params=_cparams(("arbitrary",)),
        name="outproj_router",
    )(yconv, yattn, xf, ada3, g2, wout_b, wrt_b, rbias)


def _dest_kernel(idx_ref, rank_ref, ps_ref, dest_ref):
    tm = idx_ref.shape[1]
    ei = lax.broadcasted_iota(I32, (N_EXPERTS, tm), 0)
    ps = ps_ref[...]
    for kk in range(TOP_K):
        start = jnp.sum(jnp.where(ei == idx_ref[kk:kk + 1, :], ps, 0.0), axis=0, keepdims=True)
        dest_ref[kk:kk + 1, :] = start.astype(I32) + rank_ref[kk:kk + 1, :]


def _dest(idx_t, rank_t, pad_start_f):
    t = idx_t.shape[1]
    tm = min(2048, t)
    col = pl.BlockSpec((TOP_K, tm), lambda i: (0, i))
    return pl.pallas_call(
        _dest_kernel,
        grid=(t // tm,),
        in_specs=[col, col, pl.BlockSpec((N_EXPERTS, 1), lambda i: (0, 0))],
        out_specs=col,
        out_shape=jax.ShapeDtypeStruct((TOP_K, t), I32),
        compiler_params=_cparams(("arbitrary",)),
        name="dest",
    )(idx_t, rank_t, pad_start_f)


def _sc_dispatch(hp, dest_ck, n_rows):
    t, w = hp.shape
    nk = dest_ck.shape[1]
    per_worker = t // SC_CHUNK // SC_WORKERS
    mesh = plsc.VectorSubcoreMesh(core_axis_name="c", subcore_axis_name="s")

    @functools.partial(
        pl.kernel, mesh=mesh,
        out_type=jax.ShapeDtypeStruct((n_rows, w), hp.dtype),
        scratch_types=[pltpu.VMEM((nk, SC_CHUNK), I32), pltpu.VMEM((SC_CHUNK, w), hp.dtype), pltpu.SemaphoreType.DMA],
    )
    def k(hp_hbm, dest_hbm, xs_hbm, idx_v, rows_v, sem):
        wid = lax.axis_index("s") * SC_CORES + lax.axis_index("c")

        @pl.loop(0, per_worker)
        def _(j):
            c = wid * per_worker + j
            pltpu.sync_copy(dest_hbm.at[c], idx_v)
            pltpu.sync_copy(hp_hbm.at[pl.ds(c * SC_CHUNK, SC_CHUNK)], rows_v)
            for kk in range(nk):
                pltpu.async_copy(rows_v, xs_hbm.at[idx_v.at[kk]], sem)
            for kk in range(nk):
                pltpu.make_async_copy(rows_v, xs_hbm.at[idx_v.at[kk]], sem).wait()

    return k(hp, dest_ck)


def _sc_gather(src, idx_c):
    n_chunks = idx_c.shape[0]
    w = src.shape[1]
    per_worker = n_chunks // SC_WORKERS
    mesh = plsc.VectorSubcoreMesh(core_axis_name="c", subcore_axis_name="s")

    @functools.partial(
        pl.kernel, mesh=mesh,
        out_type=jax.ShapeDtypeStruct((n_chunks * SC_CHUNK, w), src.dtype),
        scratch_types=[pltpu.VMEM((1, SC_CHUNK), I32), pltpu.VMEM((SC_CHUNK, w), src.dtype), pltpu.SemaphoreType.DMA],
    )
    def k(src_hbm, idx_hbm, out_hbm, idx_v, rows_v, sem):
        wid = lax.axis_index("s") * SC_CORES + lax.axis_index("c")

        @pl.loop(0, per_worker)
        def _(j):
            c = wid * per_worker + j
            pltpu.sync_copy(idx_hbm.at[c], idx_v)
            pltpu.async_copy(src_hbm.at[idx_v.at[0]], rows_v, sem).wait()
            pltpu.sync_copy(rows_v, out_hbm.at[pl.ds(c * SC_CHUNK, SC_CHUNK)])

    return k(src, idx_c)


def _experts_kernel(bstart_ref, xs_hbm, wg_ref, wu_ref, wd_ref, out_hbm,
                    wgu_b, wd_b, xbuf, obuf, sem_in, sem_out):
    e = pl.program_id(0)
    b0 = bstart_ref[e]
    b1 = bstart_ref[e + 1]
    n_used = bstart_ref[N_EXPERTS]
    nbuf = EXPERT_BUFS

    def in_copy(g, slot):
        return pltpu.make_async_copy(xs_hbm.at[pl.ds(g * ROW_TILE, ROW_TILE)], xbuf.at[slot], sem_in.at[slot])

    def out_copy(g, slot):
        return pltpu.make_async_copy(obuf.at[slot], out_hbm.at[pl.ds(g * ROW_TILE, ROW_TILE)], sem_out.at[slot])

    @pl.when(e == 0)
    def _():
        for g in range(nbuf - 1):
            @pl.when(g < n_used)
            def _():
                in_copy(g, g).start()

    @pl.when(b1 > b0)
    def _():
        wgu_b[:, 0:D_EXPERT] = wg_ref[...].astype(BF16)
        wgu_b[:, D_EXPERT:2 * D_EXPERT] = wu_ref[...].astype(BF16)
        wd_b[...] = wd_ref[...].astype(BF16)

        def body(g, carry):
            slot = lax.rem(g, nbuf)
            in_copy(g, slot).wait()
            ahead = g + (nbuf - 1)

            @pl.when(ahead < n_used)
            def _():
                in_copy(ahead, lax.rem(ahead, nbuf)).start()

            @pl.when(g >= nbuf)
            def _():
                out_copy(g - nbuf, slot).wait()

            xb = _unpack_bf16_pairs(xbuf[slot]).astype(BF16)
            gu = jnp.dot(xb, wgu_b[...], preferred_element_type=F32)
            act = (jax.nn.silu(gu[:, 0:D_EXPERT]) * gu[:, D_EXPERT:2 * D_EXPERT]).astype(BF16)
            obuf[slot] = _pack_bf16_pairs(jnp.dot(act, wd_b[...], preferred_element_type=F32))
            out_copy(g, slot).start()
            return carry

        lax.fori_loop(b0, b1, body, 0)

    @pl.when(e == N_EXPERTS - 1)
    def _():
        for back in range(1, nbuf + 1):
            @pl.when(n_used >= back)
            def _():
                g = n_used - back
                out_copy(g, lax.rem(g, nbuf)).wait()


def _experts(xs, bstart, w_gate, w_up, w_down):
    n_rows, half = xs.shape
    d = 2 * half
    weights = lambda shape: pl.BlockSpec((None,) + shape, lambda e, bs: (e, 0, 0))
    return pl.pallas_call(
        _experts_kernel,
        grid_spec=pltpu.PrefetchScalarGridSpec(
            num_scalar_prefetch=1,
            grid=(N_EXPERTS,),
            in_specs=[pl.BlockSpec(memory_space=pl.ANY),
                      weights((d, D_EXPERT)), weights((d, D_EXPERT)), weights((D_EXPERT, d))],
            out_specs=pl.BlockSpec(memory_space=pl.ANY),
            scratch_shapes=[pltpu.VMEM((d, 2 * D_EXPERT), BF16), pltpu.VMEM((D_EXPERT, d), BF16),
                            pltpu.VMEM((EXPERT_BUFS, ROW_TILE, half), I32),
                            pltpu.VMEM((EXPERT_BUFS, ROW_TILE, half), I32),
                            pltpu.SemaphoreType.DMA((EXPERT_BUFS,)), pltpu.SemaphoreType.DMA((EXPERT_BUFS,))],
        ),
        out_shape=jax.ShapeDtypeStruct((n_rows, half), I32),
        compiler_params=_cparams(("arbitrary",)),
        name="experts",
    )(bstart, xs, w_gate, w_up, w_down)


def _final_kernel(x1_ref, hp_ref, ou_ref, wts_ref, ada_ref, wgs_ref, wus_ref, wds_ref, gf_ref, o_ref):
    nchunk = ou_ref.shape[0]
    gt2 = ada_ref[0][5:6]
    hb = _unpack_bf16_pairs(hp_ref[...]).astype(BF16)
    g = jnp.dot(hb, wgs_ref[...], preferred_element_type=F32)
    u = jnp.dot(hb, wus_ref[...], preferred_element_type=F32)
    y = jnp.dot((jax.nn.silu(g) * u).astype(BF16), wds_ref[...], preferred_element_type=F32)
    wts = wts_ref[...]
    parts = []
    for c in range(nchunk):
        rs = slice(c * SC_CHUNK, (c + 1) * SC_CHUNK)
        yr = wts[rs, 0:1] * _unpack_bf16_pairs(ou_ref[c, 0])
        for kk in range(1, TOP_K):
            yr = yr + wts[rs, kk:kk + 1] * _unpack_bf16_pairs(ou_ref[c, kk])
        parts.append(yr)
    y = y + (parts[0] if nchunk == 1 else jnp.concatenate(parts, axis=0))
    x2 = x1_ref[...] + gt2 * y
    o_ref[...] = _rms(x2, NORM_EPS) * gf_ref[...]


def _final(x1, hp, outu4, wts, ada3, wgs_b, wus_b, wds_b, g_final, seq):
    t, d = x1.shape
    tm = min(256, seq)
    nchunk = tm // SC_CHUNK
    tiles_per_seq = seq // tm
    full = lambda shape: pl.BlockSpec(shape, lambda i: (0,) * len(shape))
    row = lambda w: pl.BlockSpec((tm, w), lambda i: (i, 0))
    return pl.pallas_call(
        _final_kernel,
        grid=(t // tm,),
        in_specs=[row(d), row(d // 2),
                  pl.BlockSpec((nchunk, TOP_K, SC_CHUNK, d // 2), lambda i: (i, 0, 0, 0)),
                  row(TOP_K),
                  pl.BlockSpec((1, 6, d), lambda i: (i // tiles_per_seq, 0, 0)),
                  full((d, D_EXPERT)), full((d, D_EXPERT)), full((D_EXPERT, d)), full((1, d))],
        out_specs=row(d),
        out_shape=jax.ShapeDtypeStruct((t, d), F32),
        compiler_params=_cparams(("arbitrary",)),
        name="final",
    )(x1, hp, outu4, wts, ada3, wgs_b, wus_b, wds_b, g_final)


def kernel(x, c, positions, w_ada, b_ada, g_norm1, w_in, conv_w, g_conv_out, lam_q1, lam_k1, lam_q2, lam_k2, g_subln, w_out, g_norm2, w_router, router_bias, w_gate_e, w_up_e, w_down_e, w_gate_s, w_up_s, w_down_s, g_final):
    bsz, seq, d = x.shape
    t = bsz * seq
    xf = x.reshape(t, d)
    pos = positions.reshape(t, 1)

    inv_freq = ROPE_THETA ** (-jnp.arange(0, ROT_DIM, 2, dtype=F32) / ROT_DIM)
    half = ROT_DIM // 2
    comp = jnp.concatenate([inv_freq, inv_freq, jnp.zeros((HEAD_DIM - ROT_DIM,), F32)])
    invf = jnp.concatenate([comp, comp]).reshape(1, LANES)
    one = jnp.ones((half,), F32)
    zero = jnp.zeros((half,), F32)
    rest = jnp.zeros((HEAD_DIM - ROT_DIM,), F32)
    up = jnp.concatenate([-one, zero, rest])
    dn = jnp.concatenate([zero, one, rest])
    sgn = jnp.stack([jnp.concatenate([up, up]), jnp.concatenate([dn, dn])])

    ada3 = _ada(c, w_ada[0], b_ada[0]).reshape(bsz, 6, d)
    yconv, qt, k, vt = _inproj(xf, pos, ada3, g_norm1[0].reshape(1, d), w_in[0].astype(BF16), conv_w[0],
                             g_conv_out[0].reshape(1, D_CONV), invf, sgn, seq)
    lamp = jnp.stack([lam_q1[0], lam_k1[0], lam_q2[0], lam_k2[0]]).astype(F32)
    yattn = _attention(qt, k, vt, lamp, g_subln[0].reshape(V_DIM, 1), bsz, seq)

    x1, hp, idx_t, wts_t, rank_t, counts_f = _outproj_router(
        yconv, yattn, xf, ada3, g_norm2[0].reshape(1, d), w_out[0].astype(BF16),
        w_router[0].T.astype(BF16), router_bias[0].reshape(N_EXPERTS, 1), seq)

    counts = counts_f[:, 0].astype(I32)
    padded = ((counts + ROW_TILE - 1) // ROW_TILE) * ROW_TILE
    pad_end = jnp.cumsum(padded)
    pad_start = pad_end - padded
    nb = (t * TOP_K + N_EXPERTS * (ROW_TILE - 1)) // ROW_TILE
    bstart = (jnp.concatenate([pad_start, pad_end[-1:]]) // ROW_TILE).astype(I32)

    dest_t = _dest(idx_t, rank_t, pad_start.astype(F32).reshape(N_EXPERTS, 1))
    n_chunks = t // SC_CHUNK
    dest_ck = dest_t.reshape(TOP_K, n_chunks, SC_CHUNK).transpose(1, 0, 2)

    xs = _sc_dispatch(hp, dest_ck, nb * ROW_TILE)
    outs = _experts(xs, bstart, w_gate_e[0], w_up_e[0], w_down_e[0])
    outu = _sc_gather(outs, dest_ck.reshape(n_chunks * TOP_K, 1, SC_CHUNK))

    out = _final(x1, hp, outu.reshape(n_chunks, TOP_K, SC_CHUNK, d // 2), wts_t.T, ada3,
                 w_gate_s[0].astype(BF16), w_up_s[0].astype(BF16), w_down_s[0].astype(BF16),
                 g_final.reshape(1, d), seq)
    return out.reshape(bsz, seq, d)
```

```python
import functools
import math

import jax
import jax.numpy as jnp
from jax import lax
from jax.experimental import pallas as pl
from jax.experimental.pallas import tpu as pltpu
from jax.experimental.pallas import tpu_sc as plsc

F32 = jnp.float32
BF16 = jnp.bfloat16
I32 = jnp.int32

D_CONV = 512
CONV_WIDTH = 3
N_HEADS = 4
HEAD_DIM = 64
V_DIM = 2 * HEAD_DIM
D_ATTN = N_HEADS * V_DIM
D_QK = N_HEADS * 2 * HEAD_DIM
ROT_DIM = HEAD_DIM // 4
ROPE_THETA = 500000.0
N_EXPERTS = 256
TOP_K = 8
N_GROUPS = 8
GROUP_SIZE = N_EXPERTS // N_GROUPS
TOPK_GROUPS = 4
D_EXPERT = 256
ROUTED_SCALE = 2.5
NORM_EPS = 1e-6
SUBLN_EPS = 1e-5
LAMBDA_INIT = 0.8 - 0.6 * math.exp(-0.3 * 0)

LANES = 128
SC_CORES = 2
SC_SUBCORES = 16
SC_WORKERS = SC_CORES * SC_SUBCORES
SC_CHUNK = 128

ROW_TILE = 256
EXPERT_BUFS = 4
NEG_BIG = -1e30
VMEM_LIMIT = 56 * 1024 * 1024


def _cparams(sem):
    return pltpu.CompilerParams(dimension_semantics=sem, vmem_limit_bytes=VMEM_LIMIT)


def _rms(x, eps):
    return x * lax.rsqrt(jnp.mean(x * x, axis=-1, keepdims=True) + eps)


def _pack_bf16_pairs(x):
    n = x.shape[1] // 2
    bits = lax.bitcast_convert_type(x.astype(BF16).astype(F32), I32)
    lo = lax.shift_right_logical(bits[:, :n], 16)
    return lo | bits[:, n:]


def _unpack_bf16_pairs(p):
    lo = lax.bitcast_convert_type(lax.shift_left(p, 16), F32)
    hi = lax.bitcast_convert_type(p & jnp.int32(-65536), F32)
    return jnp.concatenate([lo, hi], axis=1)


def _ada_kernel(c_ref, w_ref, b_ref, o_ref):
    ca = jax.nn.silu(c_ref[...])
    o_ref[...] = jnp.dot(ca.astype(BF16), w_ref[...].astype(BF16), preferred_element_type=F32) + b_ref[...]


def _ada(c, w_ada, b_ada):
    bsz, d = c.shape
    n = w_ada.shape[1]
    tn = n // 4
    return pl.pallas_call(
        _ada_kernel,
        grid=(n // tn,),
        in_specs=[pl.BlockSpec((bsz, d), lambda j: (0, 0)),
                  pl.BlockSpec((d, tn), lambda j: (0, j)),
                  pl.BlockSpec((1, tn), lambda j: (0, j))],
        out_specs=pl.BlockSpec((bsz, tn), lambda j: (0, j)),
        out_shape=jax.ShapeDtypeStruct((bsz, n), F32),
        compiler_params=_cparams(("arbitrary",)),
        name="ada",
    )(c, w_ada, b_ada.reshape(1, n))


def _inproj_kernel(tiles_per_seq, x_ref, pos_ref, ada_ref, g1_ref, win_ref, convw_ref, gconv_ref, invf_ref, sgn_ref,
                   yconv_ref, qt_ref, k_ref, vt_ref, ubuf):
    i = pl.program_id(0)
    tm = x_ref.shape[0]
    x = x_ref[...]
    ada = ada_ref[0]
    sh1, sc1 = ada[0:1], ada[1:2]
    hn = _rms(x, NORM_EPS) * g1_ref[...] * (1.0 + sc1) + sh1
    hb = hn.astype(BF16)

    pc = jnp.dot(hb, win_ref[:, 0:3 * D_CONV], preferred_element_type=F32)
    u = pc[:, 2 * D_CONV:3 * D_CONV] * pc[:, 0:D_CONV]
    first = (i % tiles_per_seq) == 0

    @pl.when(first)
    def _():
        ubuf[0:8, :] = jnp.zeros((8, D_CONV), F32)

    @pl.when(jnp.logical_not(first))
    def _():
        ubuf[0:8, :] = ubuf[tm:tm + 8, :]

    ubuf[8:8 + tm, :] = u
    u1 = ubuf[7:7 + tm, :]
    u2 = ubuf[6:6 + tm, :]
    cw = convw_ref[...]
    yc = pc[:, D_CONV:2 * D_CONV] * (cw[0:1] * u2 + cw[1:2] * u1 + cw[2:3] * u)
    yconv_ref[...] = (_rms(yc, NORM_EPS) * gconv_ref[...]).astype(BF16)

    ang = pos_ref[...].astype(F32) * invf_ref[...]
    cos = jnp.cos(ang)
    sin = jnp.sin(ang)
    s_up = sin * sgn_ref[0:1]
    s_dn = sin * sgn_ref[1:2]
    half = ROT_DIM // 2

    def rope(t):
        return t * cos + pltpu.roll(t, LANES - half, axis=1) * s_up + pltpu.roll(t, half, axis=1) * s_dn

    pq = jnp.dot(hb, win_ref[:, 3 * D_CONV:3 * D_CONV + D_QK], preferred_element_type=F32)
    pk = jnp.dot(hb, win_ref[:, 3 * D_CONV + D_QK:3 * D_CONV + 2 * D_QK], preferred_element_type=F32)
    scale = HEAD_DIM ** -0.5 * math.log2(math.e)
    qs = []
    for h in range(N_HEADS):
        sl = slice(h * LANES, (h + 1) * LANES)
        qs.append(rope(pq[:, sl]) * scale)
        k_ref[:, sl] = rope(pk[:, sl]).astype(BF16)
    qt_ref[0] = jnp.concatenate(qs, axis=1).T.astype(BF16)
    pv = jnp.dot(hb, win_ref[:, 3 * D_CONV + 2 * D_QK:], preferred_element_type=F32)
    vt_ref[0] = pv.T.astype(BF16)


def _inproj(xf, pos, ada3, g1, win_b, conv_w, g_conv, invf, sgn, seq):
    t, d = xf.shape
    tm = min(512, seq)
    tiles_per_seq = seq // tm
    n_in = win_b.shape[1]
    full = lambda shape: pl.BlockSpec(shape, lambda i: (0,) * len(shape))
    row = lambda w: pl.BlockSpec((tm, w), lambda i: (i, 0))
    colmajor = lambda w: pl.BlockSpec((1, w, tm), lambda i: (i // tiles_per_seq, 0, i % tiles_per_seq))
    return pl.pallas_call(
        functools.partial(_inproj_kernel, tiles_per_seq),
        grid=(t // tm,),
        in_specs=[row(d), row(1),
                  pl.BlockSpec((1, 6, d), lambda i: (i // tiles_per_seq, 0, 0)),
                  full((1, d)), full((d, n_in)), full((CONV_WIDTH, D_CONV)), full((1, D_CONV)),
                  full((1, LANES)), full((2, LANES))],
        out_specs=[row(D_CONV), colmajor(D_QK), row(D_QK), colmajor(D_ATTN)],
        out_shape=[jax.ShapeDtypeStruct((t, D_CONV), BF16), jax.ShapeDtypeStruct((t // seq, D_QK, seq), BF16),
                   jax.ShapeDtypeStruct((t, D_QK), BF16), jax.ShapeDtypeStruct((t // seq, D_ATTN, seq), BF16)],
        scratch_shapes=[pltpu.VMEM((tm + 8, D_CONV), F32)],
        compiler_params=_cparams(("arbitrary",)),
        name="inproj",
    )(xf, pos, ada3, g1, win_b, conv_w, g_conv, invf, sgn)


def _attn_kernel(qt_ref, k_ref, vt_ref, lamp_ref, gs_ref, o_ref, qq_ref, m_ref, l_ref, acc_ref, s0_ref, s1_ref):
    i = pl.program_id(2)
    tq = qt_ref.shape[2]
    tk = tq
    qt = qt_ref[0]
    feat = lax.broadcasted_iota(I32, qt.shape, 0)
    zero = jnp.zeros_like(qt)
    qq_ref[:, 0:tq] = jnp.where(feat < HEAD_DIM, qt, zero)
    qq_ref[:, tq:2 * tq] = jnp.where(feat >= HEAD_DIM, qt, zero)
    m_ref[...] = jnp.full(m_ref.shape, NEG_BIG, F32)
    l_ref[...] = jnp.zeros(l_ref.shape, F32)
    acc_ref[...] = jnp.zeros(acc_ref.shape, F32)

    def scores(j, dst):
        start = pl.multiple_of(j * tk, tk)
        dst[...] = jnp.dot(k_ref[pl.ds(start, tk), :], qq_ref[...], preferred_element_type=F32)

    def consume(src, j, masked):
        start = pl.multiple_of(j * tk, tk)
        vtb = vt_ref[0, :, pl.ds(start, tk)]
        s = src[...]
        if masked:
            key = lax.broadcasted_iota(I32, s.shape, 0)
            col = lax.broadcasted_iota(I32, s.shape, 1)
            qpos = jnp.where(col >= tq, col - tq, col)
            s = jnp.where(key <= qpos, s, NEG_BIG)
        m_prev = m_ref[...]
        m_new = jnp.maximum(m_prev, jnp.max(s, axis=0, keepdims=True))
        alpha = jnp.exp2(m_prev - m_new)
        p = jnp.exp2(s - m_new)
        l_ref[...] = alpha * l_ref[...] + jnp.sum(p, axis=0, keepdims=True)
        acc_ref[...] = alpha * acc_ref[...] + jnp.dot(vtb, p.astype(BF16), preferred_element_type=F32)
        m_ref[...] = m_new

    scores(0, s0_ref)

    def pair(jj, carry):
        j = 2 * jj
        scores(j + 1, s1_ref)
        consume(s0_ref, j, False)
        scores(j + 2, s0_ref)
        consume(s1_ref, j + 1, False)
        return carry

    lax.fori_loop(0, i // 2, pair, 0)

    @pl.when(i % 2 == 1)
    def _():
        scores(i, s1_ref)
        consume(s0_ref, i - 1, False)
        consume(s1_ref, i, True)

    @pl.when(i % 2 == 0)
    def _():
        consume(s0_ref, i, True)

    o = acc_ref[...] / l_ref[...]
    lamp = lamp_ref[...]
    lam = (jnp.exp(jnp.sum(lamp[0:1] * lamp[1:2], axis=1, keepdims=True))
           - jnp.exp(jnp.sum(lamp[2:3] * lamp[3:4], axis=1, keepdims=True)) + LAMBDA_INIT)
    od = o[:, 0:tq] - lam * o[:, tq:2 * tq]
    y = od * lax.rsqrt(jnp.mean(od * od, axis=0, keepdims=True) + SUBLN_EPS) * gs_ref[...] * (1.0 - LAMBDA_INIT)
    o_ref[...] = y.T.astype(BF16)


def _attention(qt, k, vt, lamp, g_subln, bsz, seq):
    t = k.shape[0]
    tq = min(512, seq)
    nq = seq // tq
    return pl.pallas_call(
        _attn_kernel,
        grid=(bsz, N_HEADS, nq),
        in_specs=[pl.BlockSpec((1, LANES, tq), lambda b, h, i: (b, h, i)),
                  pl.BlockSpec((seq, LANES), lambda b, h, i: (b, h)),
                  pl.BlockSpec((1, V_DIM, seq), lambda b, h, i: (b, h, 0)),
                  pl.BlockSpec((4, HEAD_DIM), lambda b, h, i: (0, 0)),
                  pl.BlockSpec((V_DIM, 1), lambda b, h, i: (0, 0))],
        out_specs=pl.BlockSpec((tq, LANES), lambda b, h, i: (b * nq + i, h)),
        out_shape=jax.ShapeDtypeStruct((t, D_ATTN), BF16),
        scratch_shapes=[pltpu.VMEM((LANES, 2 * tq), BF16), pltpu.VMEM((1, 2 * tq), F32),
                        pltpu.VMEM((1, 2 * tq), F32), pltpu.VMEM((V_DIM, 2 * tq), F32),
                        pltpu.VMEM((tq, 2 * tq), F32), pltpu.VMEM((tq, 2 * tq), F32)],
        compiler_params=_cparams(("arbitrary", "arbitrary", "arbitrary")),
        name="attention",
    )(qt, k, vt, lamp, g_subln)


def _first_index(hit, idx, size, axis):
    return jnp.min(jnp.where(hit, idx, size), axis=axis, keepdims=True)


def _outproj_router_kernel(yc_ref, ya_ref, x_ref, ada_ref, g2_ref, wout_ref, wrt_ref, rb_ref,
                           x1_ref, hp_ref, idx_ref, wts_ref, rank_ref, cnt_ref, base_ref):
    i = pl.program_id(0)
    tm = x_ref.shape[0]
    ada = ada_ref[0]
    gt1, sh2, sc2 = ada[2:3], ada[3:4], ada[4:5]
    mix = (jnp.dot(yc_ref[...], wout_ref[0:D_CONV, :], preferred_element_type=F32)
           + jnp.dot(ya_ref[...], wout_ref[D_CONV:, :], preferred_element_type=F32))
    x1 = x_ref[...] + gt1 * mix
    x1_ref[...] = x1
    hn2 = _rms(x1, NORM_EPS) * g2_ref[...] * (1.0 + sc2) + sh2
    hp_ref[...] = _pack_bf16_pairs(hn2)
    hb = hn2.astype(BF16)

    logits = lax.dot_general(wrt_ref[...], hb, (((1,), (1,)), ((), ())), preferred_element_type=F32)
    scores = jax.nn.sigmoid(logits)
    choice = scores + rb_ref[...]
    neg_inf = jnp.float32(-jnp.inf)

    ch3 = choice.reshape(N_GROUPS, GROUP_SIZE, tm)
    i3 = lax.broadcasted_iota(I32, ch3.shape, 1)
    m1 = jnp.max(ch3, axis=1, keepdims=True)
    f1 = _first_index(ch3 == m1, i3, GROUP_SIZE, 1)
    m2 = jnp.max(jnp.where(i3 == f1, neg_inf, ch3), axis=1, keepdims=True)
    gs = (m1 + m2).reshape(N_GROUPS, tm)

    gi = lax.broadcasted_iota(I32, gs.shape, 0)
    gkeep = jnp.zeros(gs.shape, F32)
    for _ in range(TOPK_GROUPS):
        m = jnp.max(gs, axis=0, keepdims=True)
        f = _first_index(gs == m, gi, N_GROUPS, 0)
        sel = gi == f
        gkeep = jnp.where(sel, 1.0, gkeep)
        gs = jnp.where(sel, neg_inf, gs)
    ekeep = jnp.broadcast_to(gkeep.reshape(N_GROUPS, 1, tm), (N_GROUPS, GROUP_SIZE, tm)).reshape(N_EXPERTS, tm)
    masked = jnp.where(ekeep > 0.0, choice, neg_inf)

    ei = lax.broadcasted_iota(I32, masked.shape, 0)
    picked = jnp.zeros(masked.shape, F32)
    idxs, ws = [], []
    for _ in range(TOP_K):
        m = jnp.max(masked, axis=0, keepdims=True)
        f = _first_index(masked == m, ei, N_EXPERTS, 0)
        sel = ei == f
        idxs.append(f)
        ws.append(jnp.sum(jnp.where(sel, scores, 0.0), axis=0, keepdims=True))
        picked = jnp.where(sel, 1.0, picked)
        masked = jnp.where(sel, neg_inf, masked)
    wsum = ws[0]
    for wk in ws[1:]:
        wsum = wsum + wk
    denom = wsum + 1e-20
    for kk in range(TOP_K):
        idx_ref[kk:kk + 1, :] = idxs[kk]
        wts_ref[kk:kk + 1, :] = ws[kk] / denom * ROUTED_SCALE

    @pl.when(i == 0)
    def _():
        base_ref[...] = jnp.zeros(base_ref.shape, F32)

    si = lax.broadcasted_iota(I32, (tm, tm), 0)
    ti = lax.broadcasted_iota(I32, (tm, tm), 1)
    earlier = jnp.where(si < ti, 1.0, 0.0).astype(BF16)
    before = jnp.dot(picked.astype(BF16), earlier, preferred_element_type=F32) + base_ref[...]
    for kk in range(TOP_K):
        rank_ref[kk:kk + 1, :] = jnp.sum(jnp.where(ei == idxs[kk], before, 0.0), axis=0, keepdims=True).astype(I32)
    base_ref[...] = base_ref[...] + jnp.sum(picked, axis=1, keepdims=True)
    cnt_ref[...] = base_ref[...]


def _outproj_router(yconv, yattn, xf, ada3, g2, wout_b, wrt_b, rbias, seq):
    t, d = xf.shape
    tm = min(512, seq)
    tiles_per_seq = seq // tm
    full = lambda shape: pl.BlockSpec(shape, lambda i: (0,) * len(shape))
    row = lambda w: pl.BlockSpec((tm, w), lambda i: (i, 0))
    col = pl.BlockSpec((TOP_K, tm), lambda i: (0, i))
    return pl.pallas_call(
        _outproj_router_kernel,
        grid=(t // tm,),
        in_specs=[row(D_CONV), row(D_ATTN), row(d),
                  pl.BlockSpec((1, 6, d), lambda i: (i // tiles_per_seq, 0, 0)),
                  full((1, d)), full((D_CONV + D_ATTN, d)), full((N_EXPERTS, d)), full((N_EXPERTS, 1))],
        out_specs=[row(d), row(d // 2), col, col, col, full((N_EXPERTS, 1))],
        out_shape=[jax.ShapeDtypeStruct((t, d), F32), jax.ShapeDtypeStruct((t, d // 2), I32),
                   jax.ShapeDtypeStruct((TOP_K, t), I32), jax.ShapeDtypeStruct((TOP_K, t), F32),
                   jax.ShapeDtypeStruct((TOP_K, t), I32), jax.ShapeDtypeStruct((N_EXPERTS, 1), F32)],
        scratch_shapes=[pltpu.VMEM((N_EXPERTS, 1), F32)],
        compiler_params=_cparams(("arbitrary",)),
        name="outproj_router",
    )(yconv, yattn, xf, ada3, g2, wout_b, wrt_b, rbias)


def _dest_kernel(idx_ref, rank_ref, ps_ref, dest_ref):
    tm = idx_ref.shape[1]
    ei = lax.broadcasted_iota(I32, (N_EXPERTS, tm), 0)
    ps = ps_ref[...]
    for kk in range(TOP_K):
        start = jnp.sum(jnp.where(ei == idx_ref[kk:kk + 1, :], ps, 0.0), axis=0, keepdims=True)
        dest_ref[kk:kk + 1, :] = start.astype(I32) + rank_ref[kk:kk + 1, :]


def _dest(idx_t, rank_t, pad_start_f):
    t = idx_t.shape[1]
    tm = min(2048, t)
    col = pl.BlockSpec((TOP_K, tm), lambda i: (0, i))
    return pl.pallas_call(
        _dest_kernel,
        grid=(t // tm,),
        in_specs=[col, col, pl.BlockSpec((N_EXPERTS, 1), lambda i: (0, 0))],
        out_specs=col,
        out_shape=jax.ShapeDtypeStruct((TOP_K, t), I32),
        compiler_params=_cparams(("arbitrary",)),
        name="dest",
    )(idx_t, rank_t, pad_start_f)


def _sc_dispatch(hp, dest_ck, n_rows):
    t, w = hp.shape
    nk = dest_ck.shape[1]
    per_worker = t // SC_CHUNK // SC_WORKERS
    mesh = plsc.VectorSubcoreMesh(core_axis_name="c", subcore_axis_name="s")

    @functools.partial(
        pl.kernel, mesh=mesh,
        out_type=jax.ShapeDtypeStruct((n_rows, w), hp.dtype),
        scratch_types=[pltpu.VMEM((nk, SC_CHUNK), I32), pltpu.VMEM((SC_CHUNK, w), hp.dtype), pltpu.SemaphoreType.DMA],
    )
    def k(hp_hbm, dest_hbm, xs_hbm, idx_v, rows_v, sem):
        wid = lax.axis_index("s") * SC_CORES + lax.axis_index("c")

        @pl.loop(0, per_worker)
        def _(j):
            c = wid * per_worker + j
            pltpu.sync_copy(dest_hbm.at[c], idx_v)
            pltpu.sync_copy(hp_hbm.at[pl.ds(c * SC_CHUNK, SC_CHUNK)], rows_v)
            for kk in range(nk):
                pltpu.async_copy(rows_v, xs_hbm.at[idx_v.at[kk]], sem)
            for kk in range(nk):
                pltpu.make_async_copy(rows_v, xs_hbm.at[idx_v.at[kk]], sem).wait()

    return k(hp, dest_ck)


def _sc_gather(src, idx_c):
    n_chunks = idx_c.shape[0]
    w = src.shape[1]
    per_worker = n_chunks // SC_WORKERS
    mesh = plsc.VectorSubcoreMesh(core_axis_name="c", subcore_axis_name="s")

    @functools.partial(
        pl.kernel, mesh=mesh,
        out_type=jax.ShapeDtypeStruct((n_chunks * SC_CHUNK, w), src.dtype),
        scratch_types=[pltpu.VMEM((1, SC_CHUNK), I32), pltpu.VMEM((SC_CHUNK, w), src.dtype), pltpu.SemaphoreType.DMA],
    )
    def k(src_hbm, idx_hbm, out_hbm, idx_v, rows_v, sem):
        wid = lax.axis_index("s") * SC_CORES + lax.axis_index("c")

        @pl.loop(0, per_worker)
        def _(j):
            c = wid * per_worker + j
            pltpu.sync_copy(idx_hbm.at[c], idx_v)
            pltpu.async_copy(src_hbm.at[idx_v.at[0]], rows_v, sem).wait()
            pltpu.sync_copy(rows_v, out_hbm.at[pl.ds(c * SC_CHUNK, SC_CHUNK)])

    return k(src, idx_c)


def _experts_kernel(bstart_ref, xs_hbm, wg_ref, wu_ref, wd_ref, out_hbm,
                    wgu_b, wd_b, xbuf, obuf, sem_in, sem_out):
    e = pl.program_id(0)
    b0 = bstart_ref[e]
    b1 = bstart_ref[e + 1]
    n_used = bstart_ref[N_EXPERTS]
    nbuf = EXPERT_BUFS

    def in_copy(g, slot):
        return pltpu.make_async_copy(xs_hbm.at[pl.ds(g * ROW_TILE, ROW_TILE)], xbuf.at[slot], sem_in.at[slot])

    def out_copy(g, slot):
        return pltpu.make_async_copy(obuf.at[slot], out_hbm.at[pl.ds(g * ROW_TILE, ROW_TILE)], sem_out.at[slot])

    @pl.when(e == 0)
    def _():
        for g in range(nbuf - 1):
            @pl.when(g < n_used)
            def _():
                in_copy(g, g).start()

    @pl.when(b1 > b0)
    def _():
        wgu_b[:, 0:D_EXPERT] = wg_ref[...].astype(BF16)
        wgu_b[:, D_EXPERT:2 * D_EXPERT] = wu_ref[...].astype(BF16)
        wd_b[...] = wd_ref[...].astype(BF16)

        def body(g, carry):
            slot = lax.rem(g, nbuf)
            in_copy(g, slot).wait()
            ahead = g + (nbuf - 1)

            @pl.when(ahead < n_used)
            def _():
                in_copy(ahead, lax.rem(ahead, nbuf)).start()

            @pl.when(g >= nbuf)
            def _():
                out_copy(g - nbuf, slot).wait()

            xb = _unpack_bf16_pairs(xbuf[slot]).astype(BF16)
            gu = jnp.dot(xb, wgu_b[...], preferred_element_type=F32)
            act = (jax.nn.silu(gu[:, 0:D_EXPERT]) * gu[:, D_EXPERT:2 * D_EXPERT]).astype(BF16)
            obuf[slot] = _pack_bf16_pairs(jnp.dot(act, wd_b[...], preferred_element_type=F32))
            out_copy(g, slot).start()
            return carry

        lax.fori_loop(b0, b1, body, 0)

    @pl.when(e == N_EXPERTS - 1)
    def _():
        for back in range(1, nbuf + 1):
            @pl.when(n_used >= back)
            def _():
                g = n_used - back
                out_copy(g, lax.rem(g, nbuf)).wait()


def _experts(xs, bstart, w_gate, w_up, w_down):
    n_rows, half = xs.shape
    d = 2 * half
    weights = lambda shape: pl.BlockSpec((None,) + shape, lambda e, bs: (e, 0, 0))
    return pl.pallas_call(
        _experts_kernel,
        grid_spec=pltpu.PrefetchScalarGridSpec(
            num_scalar_prefetch=1,
            grid=(N_EXPERTS,),
            in_specs=[pl.BlockSpec(memory_space=pl.ANY),
                      weights((d, D_EXPERT)), weights((d, D_EXPERT)), weights((D_EXPERT, d))],
            out_specs=pl.BlockSpec(memory_space=pl.ANY),
            scratch_shapes=[pltpu.VMEM((d, 2 * D_EXPERT), BF16), pltpu.VMEM((D_EXPERT, d), BF16),
                            pltpu.VMEM((EXPERT_BUFS, ROW_TILE, half), I32),
                            pltpu.VMEM((EXPERT_BUFS, ROW_TILE, half), I32),
                            pltpu.SemaphoreType.DMA((EXPERT_BUFS,)), pltpu.SemaphoreType.DMA((EXPERT_BUFS,))],
        ),
        out_shape=jax.ShapeDtypeStruct((n_rows, half), I32),
        compiler_params=_cparams(("arbitrary",)),
        name="experts",
    )(bstart, xs, w_gate, w_up, w_down)


def _final_kernel(x1_ref, hp_ref, ou_ref, wts_ref, ada_ref, wgs_ref, wus_ref, wds_ref, gf_ref, o_ref):
    nchunk = ou_ref.shape[0]
    gt2 = ada_ref[0][5:6]
    hb = _unpack_bf16_pairs(hp_ref[...]).astype(BF16)
    g = jnp.dot(hb, wgs_ref[...], preferred_element_type=F32)
    u = jnp.dot(hb, wus_ref[...], preferred_element_type=F32)
    y = jnp.dot((jax.nn.silu(g) * u).astype(BF16), wds_ref[...], preferred_element_type=F32)
    wts = wts_ref[...]
    parts = []
    for c in range(nchunk):
        rs = slice(c * SC_CHUNK, (c + 1) * SC_CHUNK)
        yr = wts[rs, 0:1] * _unpack_bf16_pairs(ou_ref[c, 0])
        for kk in range(1, TOP_K):
            yr = yr + wts[rs, kk:kk + 1] * _unpack_bf16_pairs(ou_ref[c, kk])
        parts.append(yr)
    y = y + (parts[0] if nchunk == 1 else jnp.concatenate(parts, axis=0))
    x2 = x1_ref[...] + gt2 * y
    o_ref[...] = _rms(x2, NORM_EPS) * gf_ref[...]


def _final(x1, hp, outu4, wts, ada3, wgs_b, wus_b, wds_b, g_final, seq):
    t, d = x1.shape
    tm = min(256, seq)
    nchunk = tm // SC_CHUNK
    tiles_per_seq = seq // tm
    full = lambda shape: pl.BlockSpec(shape, lambda i: (0,) * len(shape))
    row = lambda w: pl.BlockSpec((tm, w), lambda i: (i, 0))
    return pl.pallas_call(
        _final_kernel,
        grid=(t // tm,),
        in_specs=[row(d), row(d // 2),
                  pl.BlockSpec((nchunk, TOP_K, SC_CHUNK, d // 2), lambda i: (i, 0, 0, 0)),
                  row(TOP_K),
                  pl.BlockSpec((1, 6, d), lambda i: (i // tiles_per_seq, 0, 0)),
                  full((d, D_EXPERT)), full((d, D_EXPERT)), full((D_EXPERT, d)), full((1, d))],
        out_specs=row(d),
        out_shape=jax.ShapeDtypeStruct((t, d), F32),
        compiler_params=_cparams(("arbitrary",)),
        name="final",
    )(x1, hp, outu4, wts, ada3, wgs_b, wus_b, wds_b, g_final)


def kernel(x, c, positions, w_ada, b_ada, g_norm1, w_in, conv_w, g_conv_out, lam_q1, lam_k1, lam_q2, lam_k2, g_subln, w_out, g_norm2, w_router, router_bias, w_gate_e, w_up_e, w_down_e, w_gate_s, w_up_s, w_down_s, g_final):
    bsz, seq, d = x.shape
    t = bsz * seq
    xf = x.reshape(t, d)
    pos = positions.reshape(t, 1)

    inv_freq = ROPE_THETA ** (-jnp.arange(0, ROT_DIM, 2, dtype=F32) / ROT_DIM)
    half = ROT_DIM // 2
    comp = jnp.concatenate([inv_freq, inv_freq, jnp.zeros((HEAD_DIM - ROT_DIM,), F32)])
    invf = jnp.concatenate([comp, comp]).reshape(1, LANES)
    one = jnp.ones((half,), F32)
    zero = jnp.zeros((half,), F32)
    rest = jnp.zeros((HEAD_DIM - ROT_DIM,), F32)
    up = jnp.concatenate([-one, zero, rest])
    dn = jnp.concatenate([zero, one, rest])
    sgn = jnp.stack([jnp.concatenate([up, up]), jnp.concatenate([dn, dn])])

    ada3 = _ada(c, w_ada[0], b_ada[0]).reshape(bsz, 6, d)
    yconv, qt, k, vt = _inproj(xf, pos, ada3, g_norm1[0].reshape(1, d), w_in[0].astype(BF16), conv_w[0],
                             g_conv_out[0].reshape(1, D_CONV), invf, sgn, seq)
    lamp = jnp.stack([lam_q1[0], lam_k1[0], lam_q2[0], lam_k2[0]]).astype(F32)
    yattn = _attention(qt, k, vt, lamp, g_subln[0].reshape(V_DIM, 1), bsz, seq)

    x1, hp, idx_t, wts_t, rank_t, counts_f = _outproj_router(
        yconv, yattn, xf, ada3, g_norm2[0].reshape(1, d), w_out[0].astype(BF16),
        w_router[0].T.astype(BF16), router_bias[0].reshape(N_EXPERTS, 1), seq)

    counts = counts_f[:, 0].astype(I32)
    padded = ((counts + ROW_TILE - 1) // ROW_TILE) * ROW_TILE
    pad_end = jnp.cumsum(padded)
    pad_start = pad_end - padded
    nb = (t * TOP_K + N_EXPERTS * (ROW_TILE - 1)) // ROW_TILE
    bstart = (jnp.concatenate([pad_start, pad_end[-1:]]) // ROW_TILE).astype(I32)

    dest_t = _dest(idx_t, rank_t, pad_start.astype(F32).reshape(N_EXPERTS, 1))
    n_chunks = t // SC_CHUNK
    dest_ck = dest_t.reshape(TOP_K, n_chunks, SC_CHUNK).transpose(1, 0, 2)

    xs = _sc_dispatch(hp, dest_ck, nb * ROW_TILE)
    outs = _experts(xs, bstart, w_gate_e[0], w_up_e[0], w_down_e[0])
    outu = _sc_gather(outs, dest_ck.reshape(n_chunks * TOP_K, 1, SC_CHUNK))

    out = _final(x1, hp, outu.reshape(n_chunks, TOP_K, SC_CHUNK, d // 2), wts_t.T, ada3,
                 w_gate_s[0].astype(BF16), w_up_s[0].astype(BF16), w_down_s[0].astype(BF16),
                 g_final.reshape(1, d), seq)
    return out.reshape(bsz, seq, d)
```

```python
import functools
import math

import jax
import jax.numpy as jnp
from jax import lax
from jax.experimental import pallas as pl
from jax.experimental.pallas import tpu as pltpu
from jax.experimental.pallas import tpu_sc as plsc

F32 = jnp.float32
BF16 = jnp.bfloat16
I32 = jnp.int32

D_CONV = 512
CONV_WIDTH = 3
N_HEADS = 4
HEAD_DIM = 64
V_DIM = 2 * HEAD_DIM
D_ATTN = N_HEADS * V_DIM
D_QK = N_HEADS * 2 * HEAD_DIM
ROT_DIM = HEAD_DIM // 4
ROPE_THETA = 500000.0
N_EXPERTS = 256
TOP_K = 8
N_GROUPS = 8
GROUP_SIZE = N_EXPERTS // N_GROUPS
TOPK_GROUPS = 4
D_EXPERT = 256
ROUTED_SCALE = 2.5
NORM_EPS = 1e-6
SUBLN_EPS = 1e-5
LAMBDA_INIT = 0.8 - 0.6 * math.exp(-0.3 * 0)

LANES = 128
SC_CORES = 2
SC_SUBCORES = 16
SC_WORKERS = SC_CORES * SC_SUBCORES
SC_CHUNK = 128

ROW_TILE = 256
EXPERT_BUFS = 6
NEG_BIG = -1e30
VMEM_LIMIT = 56 * 1024 * 1024


def _cparams(sem):
    return pltpu.CompilerParams(dimension_semantics=sem, vmem_limit_bytes=VMEM_LIMIT)


def _rms(x, eps):
    return x * lax.rsqrt(jnp.mean(x * x, axis=-1, keepdims=True) + eps)


def _pack_bf16_pairs(x):
    n = x.shape[1] // 2
    bits = lax.bitcast_convert_type(x.astype(BF16).astype(F32), I32)
    lo = lax.shift_right_logical(bits[:, :n], 16)
    return lo | bits[:, n:]


def _unpack_bf16_pairs(p):
    lo = lax.bitcast_convert_type(lax.shift_left(p, 16), F32)
    hi = lax.bitcast_convert_type(p & jnp.int32(-65536), F32)
    return jnp.concatenate([lo, hi], axis=1)


def _ada_kernel(c_ref, w_ref, b_ref, o_ref):
    ca = jax.nn.silu(c_ref[...])
    o_ref[...] = jnp.dot(ca.astype(BF16), w_ref[...].astype(BF16), preferred_element_type=F32) + b_ref[...]


def _ada(c, w_ada, b_ada):
    bsz, d = c.shape
    n = w_ada.shape[1]
    tn = n // 4
    return pl.pallas_call(
        _ada_kernel,
        grid=(n // tn,),
        in_specs=[pl.BlockSpec((bsz, d), lambda j: (0, 0)),
                  pl.BlockSpec((d, tn), lambda j: (0, j)),
                  pl.BlockSpec((1, tn), lambda j: (0, j))],
        out_specs=pl.BlockSpec((bsz, tn), lambda j: (0, j)),
        out_shape=jax.ShapeDtypeStruct((bsz, n), F32),
        compiler_params=_cparams(("arbitrary",)),
        name="ada",
    )(c, w_ada, b_ada.reshape(1, n))


def _inproj_kernel(tiles_per_seq, x_ref, pos_ref, ada_ref, g1_ref, win_ref, convw_ref, gconv_ref, invf_ref,
                   yconv_ref, qt_ref, k_ref, vt_ref, ubuf):
    i = pl.program_id(0)
    tm = x_ref.shape[0]
    x = x_ref[...]
    ada = ada_ref[0]
    sh1, sc1 = ada[0:1], ada[1:2]
    hn = _rms(x, NORM_EPS) * g1_ref[...] * (1.0 + sc1) + sh1
    hb = hn.astype(BF16)

    pc = jnp.dot(hb, win_ref[:, 0:3 * D_CONV], preferred_element_type=F32)
    u = pc[:, 2 * D_CONV:3 * D_CONV] * pc[:, 0:D_CONV]
    first = (i % tiles_per_seq) == 0

    @pl.when(first)
    def _():
        ubuf[0:8, :] = jnp.zeros((8, D_CONV), F32)

    @pl.when(jnp.logical_not(first))
    def _():
        ubuf[0:8, :] = ubuf[tm:tm + 8, :]

    ubuf[8:8 + tm, :] = u
    u1 = ubuf[7:7 + tm, :]
    u2 = ubuf[6:6 + tm, :]
    cw = convw_ref[...]
    yc = pc[:, D_CONV:2 * D_CONV] * (cw[0:1] * u2 + cw[1:2] * u1 + cw[2:3] * u)
    yconv_ref[...] = (_rms(yc, NORM_EPS) * gconv_ref[...]).astype(BF16)

    half = ROT_DIM // 2
    ang = invf_ref[...] * pos_ref[...].astype(F32)
    cos_ft = jnp.cos(ang)
    sin_ft = jnp.sin(ang)
    zero_h = jnp.zeros((half, tm), F32)
    zero_r = jnp.zeros((HEAD_DIM - ROT_DIM, tm), F32)
    lay = lambda a, b, r: jnp.concatenate([a, b, r, a, b, r], axis=0).T
    cos = lay(cos_ft, cos_ft, zero_r + 1.0)
    s_up = lay(-sin_ft, zero_h, zero_r)
    s_dn = lay(zero_h, sin_ft, zero_r)

    def rope(t):
        return t * cos + pltpu.roll(t, LANES - half, axis=1) * s_up + pltpu.roll(t, half, axis=1) * s_dn

    pq = jnp.dot(hb, win_ref[:, 3 * D_CONV:3 * D_CONV + D_QK], preferred_element_type=F32)
    pk = jnp.dot(hb, win_ref[:, 3 * D_CONV + D_QK:3 * D_CONV + 2 * D_QK], preferred_element_type=F32)
    scale = HEAD_DIM ** -0.5 * math.log2(math.e)
    qs = []
    for h in range(N_HEADS):
        sl = slice(h * LANES, (h + 1) * LANES)
        qs.append(rope(pq[:, sl]) * scale)
        k_ref[:, sl] = rope(pk[:, sl]).astype(BF16)
    qt_ref[0] = jnp.concatenate(qs, axis=1).T.astype(BF16)
    pv = jnp.dot(hb, win_ref[:, 3 * D_CONV + 2 * D_QK:], preferred_element_type=F32)
    vt_ref[0] = pv.T.astype(BF16)


def _inproj(xf, pos, ada3, g1, win_b, conv_w, g_conv, invf, seq):
    t, d = xf.shape
    tm = min(512, seq)
    tiles_per_seq = seq // tm
    n_in = win_b.shape[1]
    full = lambda shape: pl.BlockSpec(shape, lambda i: (0,) * len(shape))
    row = lambda w: pl.BlockSpec((tm, w), lambda i: (i, 0))
    colmajor = lambda w: pl.BlockSpec((1, w, tm), lambda i: (i // tiles_per_seq, 0, i % tiles_per_seq))
    return pl.pallas_call(
        functools.partial(_inproj_kernel, tiles_per_seq),
        grid=(t // tm,),
        in_specs=[row(d), pl.BlockSpec((1, tm), lambda i: (0, i)),
                  pl.BlockSpec((1, 6, d), lambda i: (i // tiles_per_seq, 0, 0)),
                  full((1, d)), full((d, n_in)), full((CONV_WIDTH, D_CONV)), full((1, D_CONV)),
                  full((ROT_DIM // 2, 1))],
        out_specs=[row(D_CONV), colmajor(D_QK), row(D_QK), colmajor(D_ATTN)],
        out_shape=[jax.ShapeDtypeStruct((t, D_CONV), BF16), jax.ShapeDtypeStruct((t // seq, D_QK, seq), BF16),
                   jax.ShapeDtypeStruct((t, D_QK), BF16), jax.ShapeDtypeStruct((t // seq, D_ATTN, seq), BF16)],
        scratch_shapes=[pltpu.VMEM((tm + 8, D_CONV), F32)],
        compiler_params=_cparams(("arbitrary",)),
        name="inproj",
    )(xf, pos, ada3, g1, win_b, conv_w, g_conv, invf)


def _attn_kernel(qt_ref, k_ref, vt_ref, lamp_ref, gs_ref, o_ref, qq_ref, m_ref, l_ref, acc_ref, s0_ref, s1_ref):
    i = pl.program_id(2)
    tq = qt_ref.shape[2]
    tk = tq
    qt = qt_ref[0]
    feat = lax.broadcasted_iota(I32, qt.shape, 0)
    zero = jnp.zeros_like(qt)
    qq_ref[:, 0:tq] = jnp.where(feat < HEAD_DIM, qt, zero)
    qq_ref[:, tq:2 * tq] = jnp.where(feat >= HEAD_DIM, qt, zero)
    m_ref[...] = jnp.full(m_ref.shape, NEG_BIG, F32)
    l_ref[...] = jnp.zeros(l_ref.shape, F32)
    acc_ref[...] = jnp.zeros(acc_ref.shape, F32)

    def scores(j, dst):
        start = pl.multiple_of(j * tk, tk)
        dst[...] = jnp.dot(k_ref[pl.ds(start, tk), :], qq_ref[...], preferred_element_type=F32)

    def consume(src, j, masked):
        start = pl.multiple_of(j * tk, tk)
        vtb = vt_ref[0, :, pl.ds(start, tk)]
        s = src[...]
        if masked:
            key = lax.broadcasted_iota(I32, s.shape, 0)
            col = lax.broadcasted_iota(I32, s.shape, 1)
            qpos = jnp.where(col >= tq, col - tq, col)
            s = jnp.where(key <= qpos, s, NEG_BIG)
        m_prev = m_ref[...]
        m_new = jnp.maximum(m_prev, jnp.max(s, axis=0, keepdims=True))
        alpha = jnp.exp2(m_prev - m_new)
        p = jnp.exp2(s - m_new)
        l_ref[...] = alpha * l_ref[...] + jnp.sum(p, axis=0, keepdims=True)
        acc_ref[...] = alpha * acc_ref[...] + jnp.dot(vtb, p.astype(BF16), preferred_element_type=F32)
        m_ref[...] = m_new

    scores(0, s0_ref)

    def pair(jj, carry):
        j = 2 * jj
        scores(j + 1, s1_ref)
        consume(s0_ref, j, False)
        scores(j + 2, s0_ref)
        consume(s1_ref, j + 1, False)
        return carry

    lax.fori_loop(0, i // 2, pair, 0)

    @pl.when(i % 2 == 1)
    def _():
        scores(i, s1_ref)
        consume(s0_ref, i - 1, False)
        consume(s1_ref, i, True)

    @pl.when(i % 2 == 0)
    def _():
        consume(s0_ref, i, True)

    o = acc_ref[...] / l_ref[...]
    lamp = lamp_ref[...]
    lam = (jnp.exp(jnp.sum(lamp[0:1] * lamp[1:2], axis=1, keepdims=True))
           - jnp.exp(jnp.sum(lamp[2:3] * lamp[3:4], axis=1, keepdims=True)) + LAMBDA_INIT)
    od = o[:, 0:tq] - lam * o[:, tq:2 * tq]
    y = od * lax.rsqrt(jnp.mean(od * od, axis=0, keepdims=True) + SUBLN_EPS) * gs_ref[...] * (1.0 - LAMBDA_INIT)
    o_ref[...] = y.T.astype(BF16)


def _attention(qt, k, vt, lamp, g_subln, bsz, seq):
    t = k.shape[0]
    tq = min(512, seq)
    nq = seq // tq
    return pl.pallas_call(
        _attn_kernel,
        grid=(bsz, N_HEADS, nq),
        in_specs=[pl.BlockSpec((1, LANES, tq), lambda b, h, i: (b, h, i)),
                  pl.BlockSpec((seq, LANES), lambda b, h, i: (b, h)),
                  pl.BlockSpec((1, V_DIM, seq), lambda b, h, i: (b, h, 0)),
                  pl.BlockSpec((4, HEAD_DIM), lambda b, h, i: (0, 0)),
                  pl.BlockSpec((V_DIM, 1), lambda b, h, i: (0, 0))],
        out_specs=pl.BlockSpec((tq, LANES), lambda b, h, i: (b * nq + i, h)),
        out_shape=jax.ShapeDtypeStruct((t, D_ATTN), BF16),
        scratch_shapes=[pltpu.VMEM((LANES, 2 * tq), BF16), pltpu.VMEM((1, 2 * tq), F32),
                        pltpu.VMEM((1, 2 * tq), F32), pltpu.VMEM((V_DIM, 2 * tq), F32),
                        pltpu.VMEM((tq, 2 * tq), F32), pltpu.VMEM((tq, 2 * tq), F32)],
        compiler_params=_cparams(("arbitrary", "arbitrary", "arbitrary")),
        name="attention",
    )(qt, k, vt, lamp, g_subln)


def _first_index(hit, idx, size, axis):
    return jnp.min(jnp.where(hit, idx, size), axis=axis, keepdims=True)


def _outproj_router_kernel(yc_ref, ya_ref, x_ref, ada_ref, g2_ref, wout_ref, wrt_ref, rb_ref,
                           x1_ref, hp_ref, idx_ref, wts_ref, rank_ref, cnt_ref, base_ref):
    i = pl.program_id(0)
    tm = x_ref.shape[0]
    ada = ada_ref[0]
    gt1, sh2, sc2 = ada[2:3], ada[3:4], ada[4:5]
    mix = (jnp.dot(yc_ref[...], wout_ref[0:D_CONV, :], preferred_element_type=F32)
           + jnp.dot(ya_ref[...], wout_ref[D_CONV:, :], preferred_element_type=F32))
    x1 = x_ref[...] + gt1 * mix
    x1_ref[...] = x1
    hn2 = _rms(x1, NORM_EPS) * g2_ref[...] * (1.0 + sc2) + sh2
    hp_ref[...] = _pack_bf16_pairs(hn2)
    hb = hn2.astype(BF16)

    logits = lax.dot_general(wrt_ref[...], hb, (((1,), (1,)), ((), ())), preferred_element_type=F32)
    scores = jax.nn.sigmoid(logits)
    choice = scores + rb_ref[...]
    neg_inf = jnp.float32(-jnp.inf)

    ch3 = choice.reshape(N_GROUPS, GROUP_SIZE, tm)
    i3 = lax.broadcasted_iota(I32, ch3.shape, 1)
    m1 = jnp.max(ch3, axis=1, keepdims=True)
    f1 = _first_index(ch3 == m1, i3, GROUP_SIZE, 1)
    m2 = jnp.max(jnp.where(i3 == f1, neg_inf, ch3), axis=1, keepdims=True)
    gs = (m1 + m2).reshape(N_GROUPS, tm)

    gi = lax.broadcasted_iota(I32, gs.shape, 0)
    gkeep = jnp.zeros(gs.shape, F32)
    for _ in range(TOPK_GROUPS):
        m = jnp.max(gs, axis=0, keepdims=True)
        f = _first_index(gs == m, gi, N_GROUPS, 0)
        sel = gi == f
        gkeep = jnp.where(sel, 1.0, gkeep)
        gs = jnp.where(sel, neg_inf, gs)
    ekeep = jnp.broadcast_to(gkeep.reshape(N_GROUPS, 1, tm), (N_GROUPS, GROUP_SIZE, tm)).reshape(N_EXPERTS, tm)
    masked = jnp.where(ekeep > 0.0, choice, neg_inf)

    ei = lax.broadcasted_iota(I32, masked.shape, 0)
    picked = jnp.zeros(masked.shape, F32)
    idxs, ws = [], []
    for _ in range(TOP_K):
        m = jnp.max(masked, axis=0, keepdims=True)
        f = _first_index(masked == m, ei, N_EXPERTS, 0)
        sel = ei == f
        idxs.append(f)
        ws.append(jnp.sum(jnp.where(sel, scores, 0.0), axis=0, keepdims=True))
        picked = jnp.where(sel, 1.0, picked)
        masked = jnp.where(sel, neg_inf, masked)
    wsum = ws[0]
    for wk in ws[1:]:
        wsum = wsum + wk
    denom = wsum + 1e-20
    for kk in range(TOP_K):
        idx_ref[kk:kk + 1, :] = idxs[kk]
        wts_ref[kk:kk + 1, :] = ws[kk] / denom * ROUTED_SCALE

    @pl.when(i == 0)
    def _():
        base_ref[...] = jnp.zeros(base_ref.shape, F32)

    si = lax.broadcasted_iota(I32, (tm, tm), 0)
    ti = lax.broadcasted_iota(I32, (tm, tm), 1)
    earlier = jnp.where(si < ti, 1.0, 0.0).astype(BF16)
    before = jnp.dot(picked.astype(BF16), earlier, preferred_element_type=F32) + base_ref[...]
    for kk in range(TOP_K):
        rank_ref[kk:kk + 1, :] = jnp.sum(jnp.where(ei == idxs[kk], before, 0.0), axis=0, keepdims=True).astype(I32)
    base_ref[...] = base_ref[...] + jnp.sum(picked, axis=1, keepdims=True)
    cnt_ref[...] = base_ref[...]


def _outproj_router(yconv, yattn, xf, ada3, g2, wout_b, wrt_b, rbias, seq):
    t, d = xf.shape
    tm = min(512, seq)
    tiles_per_seq = seq // tm
    full = lambda shape: pl.BlockSpec(shape, lambda i: (0,) * len(shape))
    row = lambda w: pl.BlockSpec((tm, w), lambda i: (i, 0))
    col = pl.BlockSpec((TOP_K, tm), lambda i: (0, i))
    return pl.pallas_call(
        _outproj_router_kernel,
        grid=(t // tm,),
        in_specs=[row(D_CONV), row(D_ATTN), row(d),
                  pl.BlockSpec((1, 6, d), lambda i: (i // tiles_per_seq, 0, 0)),
                  full((1, d)), full((D_CONV + D_ATTN, d)), full((N_EXPERTS, d)), full((N_EXPERTS, 1))],
        out_specs=[row(d), row(d // 2), col, col, col, full((N_EXPERTS, 1))],
        out_shape=[jax.ShapeDtypeStruct((t, d), F32), jax.ShapeDtypeStruct((t, d // 2), I32),
                   jax.ShapeDtypeStruct((TOP_K, t), I32), jax.ShapeDtypeStruct((TOP_K, t), F32),
                   jax.ShapeDtypeStruct((TOP_K, t), I32), jax.ShapeDtypeStruct((N_EXPERTS, 1), F32)],
        scratch_shapes=[pltpu.VMEM((N_EXPERTS, 1), F32)],
        compiler_params=_cparams(("arbitrary",)),
        name="outproj_router",
    )(yconv, yattn, xf, ada3, g2, wout_b, wrt_b, rbias)


def _dest_kernel(idx_ref, rank_ref, ps_ref, dest_ref):
    tm = idx_ref.shape[1]
    ei = lax.broadcasted_iota(I32, (N_EXPERTS, tm), 0)
    ps = ps_ref[...]
    for kk in range(TOP_K):
        start = jnp.sum(jnp.where(ei == idx_ref[kk:kk + 1, :], ps, 0.0), axis=0, keepdims=True)
        dest_ref[kk:kk + 1, :] = start.astype(I32) + rank_ref[kk:kk + 1, :]


def _dest(idx_t, rank_t, pad_start_f):
    t = idx_t.shape[1]
    tm = min(2048, t)
    col = pl.BlockSpec((TOP_K, tm), lambda i: (0, i))
    return pl.pallas_call(
        _dest_kernel,
        grid=(t // tm,),
        in_specs=[col, col, pl.BlockSpec((N_EXPERTS, 1), lambda i: (0, 0))],
        out_specs=col,
        out_shape=jax.ShapeDtypeStruct((TOP_K, t), I32),
        compiler_params=_cparams(("arbitrary",)),
        name="dest",
    )(idx_t, rank_t, pad_start_f)


def _sc_dispatch(hp, dest_ck, n_rows):
    t, w = hp.shape
    nk = dest_ck.shape[1]
    per_worker = t // SC_CHUNK // SC_WORKERS
    mesh = plsc.VectorSubcoreMesh(core_axis_name="c", subcore_axis_name="s")

    @functools.partial(
        pl.kernel, mesh=mesh,
        out_type=jax.ShapeDtypeStruct((n_rows, w), hp.dtype),
        scratch_types=[pltpu.VMEM((nk, SC_CHUNK), I32), pltpu.VMEM((SC_CHUNK, w), hp.dtype), pltpu.SemaphoreType.DMA],
    )
    def k(hp_hbm, dest_hbm, xs_hbm, idx_v, rows_v, sem):
        wid = lax.axis_index("s") * SC_CORES + lax.axis_index("c")

        @pl.loop(0, per_worker)
        def _(j):
            c = wid * per_worker + j
            pltpu.sync_copy(dest_hbm.at[c], idx_v)
            pltpu.sync_copy(hp_hbm.at[pl.ds(c * SC_CHUNK, SC_CHUNK)], rows_v)
            for kk in range(nk):
                pltpu.async_copy(rows_v, xs_hbm.at[idx_v.at[kk]], sem)
            for kk in range(nk):
                pltpu.make_async_copy(rows_v, xs_hbm.at[idx_v.at[kk]], sem).wait()

    return k(hp, dest_ck)


def _sc_gather(src, idx_c):
    n_chunks = idx_c.shape[0]
    w = src.shape[1]
    per_worker = n_chunks // SC_WORKERS
    mesh = plsc.VectorSubcoreMesh(core_axis_name="c", subcore_axis_name="s")

    @functools.partial(
        pl.kernel, mesh=mesh,
        out_type=jax.ShapeDtypeStruct((n_chunks * SC_CHUNK, w), src.dtype),
        scratch_types=[pltpu.VMEM((1, SC_CHUNK), I32), pltpu.VMEM((SC_CHUNK, w), src.dtype), pltpu.SemaphoreType.DMA],
    )
    def k(src_hbm, idx_hbm, out_hbm, idx_v, rows_v, sem):
        wid = lax.axis_index("s") * SC_CORES + lax.axis_index("c")

        @pl.loop(0, per_worker)
        def _(j):
            c = wid * per_worker + j
            pltpu.sync_copy(idx_hbm.at[c], idx_v)
            pltpu.async_copy(src_hbm.at[idx_v.at[0]], rows_v, sem).wait()
            pltpu.sync_copy(rows_v, out_hbm.at[pl.ds(c * SC_CHUNK, SC_CHUNK)])

    return k(src, idx_c)


def _experts_kernel(bstart_ref, xs_hbm, wg_ref, wu_ref, wd_ref, out_hbm,
                    wgu_b, wd_b, xbuf, obuf, sem_in, sem_out):
    e = pl.program_id(0)
    b0 = bstart_ref[e]
    b1 = bstart_ref[e + 1]
    n_used = bstart_ref[N_EXPERTS]
    nbuf = EXPERT_BUFS

    def in_copy(g, slot):
        return pltpu.make_async_copy(xs_hbm.at[pl.ds(g * ROW_TILE, ROW_TILE)], xbuf.at[slot], sem_in.at[slot])

    def out_copy(g, slot):
        return pltpu.make_async_copy(obuf.at[slot], out_hbm.at[pl.ds(g * ROW_TILE, ROW_TILE)], sem_out.at[slot])

    @pl.when(e == 0)
    def _():
        for g in range(nbuf - 1):
            @pl.when(g < n_used)
            def _():
                in_copy(g, g).start()

    @pl.when(b1 > b0)
    def _():
        wgu_b[:, 0:D_EXPERT] = wg_ref[...].astype(BF16)
        wgu_b[:, D_EXPERT:2 * D_EXPERT] = wu_ref[...].astype(BF16)
        wd_b[...] = wd_ref[...].astype(BF16)

        def fetch(g):
            @pl.when(g < n_used)
            def _():
                in_copy(g, lax.rem(g, nbuf)).start()

        def run(blocks):
            slots = [lax.rem(g, nbuf) for g in blocks]
            for g, slot in zip(blocks, slots):
                in_copy(g, slot).wait()
            fetch(blocks[0] + nbuf - 1)
            for g, slot in zip(blocks, slots):
                @pl.when(g >= nbuf)
                def _():
                    out_copy(g - nbuf, slot).wait()
            xs_rows = [_unpack_bf16_pairs(xbuf[slot]).astype(BF16) for slot in slots]
            xb = xs_rows[0] if len(slots) == 1 else jnp.concatenate(xs_rows, axis=0)
            gu = jnp.dot(xb, wgu_b[...], preferred_element_type=F32)
            act = (jax.nn.silu(gu[:, 0:D_EXPERT]) * gu[:, D_EXPERT:2 * D_EXPERT]).astype(BF16)
            out = _pack_bf16_pairs(jnp.dot(act, wd_b[...], preferred_element_type=F32))
            for n, slot in enumerate(slots):
                obuf[slot] = out[n * ROW_TILE:(n + 1) * ROW_TILE]
            for g, slot in zip(blocks, slots):
                out_copy(g, slot).start()
            for g in blocks[:-1]:
                fetch(g + nbuf)

        def pair(jj, carry):
            g = b0 + 2 * jj
            run([g, g + 1])
            return carry

        lax.fori_loop(0, (b1 - b0) // 2, pair, 0)

        @pl.when((b1 - b0) % 2 == 1)
        def _():
            run([b1 - 1])

    @pl.when(e == N_EXPERTS - 1)
    def _():
        for back in range(1, nbuf + 1):
            @pl.when(n_used >= back)
            def _():
                g = n_used - back
                out_copy(g, lax.rem(g, nbuf)).wait()


def _experts(xs, bstart, w_gate, w_up, w_down):
    n_rows, half = xs.shape
    d = 2 * half
    weights = lambda shape: pl.BlockSpec((None,) + shape, lambda e, bs: (e, 0, 0))
    return pl.pallas_call(
        _experts_kernel,
        grid_spec=pltpu.PrefetchScalarGridSpec(
            num_scalar_prefetch=1,
            grid=(N_EXPERTS,),
            in_specs=[pl.BlockSpec(memory_space=pl.ANY),
                      weights((d, D_EXPERT)), weights((d, D_EXPERT)), weights((D_EXPERT, d))],
            out_specs=pl.BlockSpec(memory_space=pl.ANY),
            scratch_shapes=[pltpu.VMEM((d, 2 * D_EXPERT), BF16), pltpu.VMEM((D_EXPERT, d), BF16),
                            pltpu.VMEM((EXPERT_BUFS, ROW_TILE, half), I32),
                            pltpu.VMEM((EXPERT_BUFS, ROW_TILE, half), I32),
                            pltpu.SemaphoreType.DMA((EXPERT_BUFS,)), pltpu.SemaphoreType.DMA((EXPERT_BUFS,))],
        ),
        out_shape=jax.ShapeDtypeStruct((n_rows, half), I32),
        compiler_params=_cparams(("arbitrary",)),
        name="experts",
    )(bstart, xs, w_gate, w_up, w_down)


def _final_kernel(x1_ref, hp_ref, ou_ref, wts_ref, ada_ref, wgs_ref, wus_ref, wds_ref, gf_ref, o_ref):
    nchunk = ou_ref.shape[0]
    gt2 = ada_ref[0][5:6]
    hb = _unpack_bf16_pairs(hp_ref[...]).astype(BF16)
    g = jnp.dot(hb, wgs_ref[...], preferred_element_type=F32)
    u = jnp.dot(hb, wus_ref[...], preferred_element_type=F32)
    y = jnp.dot((jax.nn.silu(g) * u).astype(BF16), wds_ref[...], preferred_element_type=F32)
    wts = wts_ref[...]
    parts = []
    for c in range(nchunk):
        rs = slice(c * SC_CHUNK, (c + 1) * SC_CHUNK)
        yr = wts[rs, 0:1] * _unpack_bf16_pairs(ou_ref[c, 0])
        for kk in range(1, TOP_K):
            yr = yr + wts[rs, kk:kk + 1] * _unpack_bf16_pairs(ou_ref[c, kk])
        parts.append(yr)
    y = y + (parts[0] if nchunk == 1 else jnp.concatenate(parts, axis=0))
    x2 = x1_ref[...] + gt2 * y
    o_ref[...] = _rms(x2, NORM_EPS) * gf_ref[...]


def _final(x1, hp, outu4, wts, ada3, wgs_b, wus_b, wds_b, g_final, seq):
    t, d = x1.shape
    tm = min(256, seq)
    nchunk = tm // SC_CHUNK
    tiles_per_seq = seq // tm
    full = lambda shape: pl.BlockSpec(shape, lambda i: (0,) * len(shape))
    row = lambda w: pl.BlockSpec((tm, w), lambda i: (i, 0))
    return pl.pallas_call(
        _final_kernel,
        grid=(t // tm,),
        in_specs=[row(d), row(d // 2),
                  pl.BlockSpec((nchunk, TOP_K, SC_CHUNK, d // 2), lambda i: (i, 0, 0, 0)),
                  row(TOP_K),
                  pl.BlockSpec((1, 6, d), lambda i: (i // tiles_per_seq, 0, 0)),
                  full((d, D_EXPERT)), full((d, D_EXPERT)), full((D_EXPERT, d)), full((1, d))],
        out_specs=row(d),
        out_shape=jax.ShapeDtypeStruct((t, d), F32),
        compiler_params=_cparams(("arbitrary",)),
        name="final",
    )(x1, hp, outu4, wts, ada3, wgs_b, wus_b, wds_b, g_final)


def kernel(x, c, positions, w_ada, b_ada, g_norm1, w_in, conv_w, g_conv_out, lam_q1, lam_k1, lam_q2, lam_k2, g_subln, w_out, g_norm2, w_router, router_bias, w_gate_e, w_up_e, w_down_e, w_gate_s, w_up_s, w_down_s, g_final):
    bsz, seq, d = x.shape
    t = bsz * seq
    xf = x.reshape(t, d)
    pos = positions.reshape(1, t)
    invf = (ROPE_THETA ** (-jnp.arange(0, ROT_DIM, 2, dtype=F32) / ROT_DIM)).reshape(ROT_DIM // 2, 1)

    ada3 = _ada(c, w_ada[0], b_ada[0]).reshape(bsz, 6, d)
    yconv, qt, k, vt = _inproj(xf, pos, ada3, g_norm1[0].reshape(1, d), w_in[0].astype(BF16), conv_w[0],
                             g_conv_out[0].reshape(1, D_CONV), invf, seq)
    lamp = jnp.stack([lam_q1[0], lam_k1[0], lam_q2[0], lam_k2[0]]).astype(F32)
    yattn = _attention(qt, k, vt, lamp, g_subln[0].reshape(V_DIM, 1), bsz, seq)

    x1, hp, idx_t, wts_t, rank_t, counts_f = _outproj_router(
        yconv, yattn, xf, ada3, g_norm2[0].reshape(1, d), w_out[0].astype(BF16),
        w_router[0].T.astype(BF16), router_bias[0].reshape(N_EXPERTS, 1), seq)

    counts = counts_f[:, 0].astype(I32)
    padded = ((counts + ROW_TILE - 1) // ROW_TILE) * ROW_TILE
    pad_end = jnp.cumsum(padded)
    pad_start = pad_end - padded
    nb = (t * TOP_K + N_EXPERTS * (ROW_TILE - 1)) // ROW_TILE
    bstart = (jnp.concatenate([pad_start, pad_end[-1:]]) // ROW_TILE).astype(I32)

    dest_t = _dest(idx_t, rank_t, pad_start.astype(F32).reshape(N_EXPERTS, 1))
    n_chunks = t // SC_CHUNK
    dest_ck = dest_t.reshape(TOP_K, n_chunks, SC_CHUNK).transpose(1, 0, 2)

    xs = _sc_dispatch(hp, dest_ck, nb * ROW_TILE)
    outs = _experts(xs, bstart, w_gate_e[0], w_up_e[0], w_down_e[0])
    outu = _sc_gather(outs, dest_ck.reshape(n_chunks * TOP_K, 1, SC_CHUNK))

    out = _final(x1, hp, outu.reshape(n_chunks, TOP_K, SC_CHUNK, d // 2), wts_t.T, ada3,
                 w_gate_s[0].astype(BF16), w_up_s[0].astype(BF16), w_down_s[0].astype(BF16),
                 g_final.reshape(1, d), seq)
    return out.reshape(bsz, seq, d)
```

```python
import functools
import math

import jax
import jax.numpy as jnp
from jax import lax
from jax.experimental import pallas as pl
from jax.experimental.pallas import tpu as pltpu
from jax.experimental.pallas import tpu_sc as plsc

F32 = jnp.float32
BF16 = jnp.bfloat16
I32 = jnp.int32

D_CONV = 512
CONV_WIDTH = 3
N_HEADS = 4
HEAD_DIM = 64
V_DIM = 2 * HEAD_DIM
D_ATTN = N_HEADS * V_DIM
D_QK = N_HEADS * 2 * HEAD_DIM
ROT_DIM = HEAD_DIM // 4
ROPE_THETA = 500000.0
N_EXPERTS = 256
TOP_K = 8
N_GROUPS = 8
GROUP_SIZE = N_EXPERTS // N_GROUPS
TOPK_GROUPS = 4
D_EXPERT = 256
ROUTED_SCALE = 2.5
NORM_EPS = 1e-6
SUBLN_EPS = 1e-5
LAMBDA_INIT = 0.8 - 0.6 * math.exp(-0.3 * 0)

LANES = 128
SC_CORES = 2
SC_SUBCORES = 16
SC_WORKERS = SC_CORES * SC_SUBCORES
SC_CHUNK = 128
SC_PACK_CHUNK_WORDS = 32768
SC_PACK_UNROLL = 2

ROW_TILE = 256
EXPERT_BUFS = 6
NEG_BIG = -1e30
VMEM_LIMIT = 56 * 1024 * 1024


def _cparams(sem):
    return pltpu.CompilerParams(dimension_semantics=sem, vmem_limit_bytes=VMEM_LIMIT)


def _rms(x, eps):
    return x * lax.rsqrt(jnp.mean(x * x, axis=-1, keepdims=True) + eps)


def _pack_bf16_pairs(x):
    n = x.shape[1] // 2
    bits = lax.bitcast_convert_type(x.astype(BF16).astype(F32), I32)
    lo = lax.shift_right_logical(bits[:, :n], 16)
    return lo | bits[:, n:]


def _unpack_bf16_pairs(p):
    lo = lax.bitcast_convert_type(lax.shift_left(p, 16), F32)
    hi = lax.bitcast_convert_type(p & jnp.int32(-65536), F32)
    return jnp.concatenate([lo, hi], axis=1)


def _ada_kernel(c_ref, w_ref, b_ref, o_ref):
    ca = jax.nn.silu(c_ref[...])
    o_ref[...] = jnp.dot(ca.astype(BF16), w_ref[...].astype(BF16), preferred_element_type=F32) + b_ref[...]


def _ada(c, w_ada, b_ada):
    bsz, d = c.shape
    n = w_ada.shape[1]
    tn = n // 4
    return pl.pallas_call(
        _ada_kernel,
        grid=(n // tn,),
        in_specs=[pl.BlockSpec((bsz, d), lambda j: (0, 0)),
                  pl.BlockSpec((d, tn), lambda j: (0, j)),
                  pl.BlockSpec((1, tn), lambda j: (0, j))],
        out_specs=pl.BlockSpec((bsz, tn), lambda j: (0, j)),
        out_shape=jax.ShapeDtypeStruct((bsz, n), F32),
        compiler_params=_cparams(("arbitrary",)),
        name="ada",
    )(c, w_ada, b_ada.reshape(1, n))


def _inproj_kernel(tiles_per_seq, x_ref, pos_ref, ada_ref, g1_ref, win_ref, convw_ref, gconv_ref, invf_ref,
                   yconv_ref, qt_ref, k_ref, vt_ref, ubuf):
    i = pl.program_id(0)
    tm = x_ref.shape[0]
    x = x_ref[...]
    ada = ada_ref[0]
    sh1, sc1 = ada[0:1], ada[1:2]
    hn = _rms(x, NORM_EPS) * g1_ref[...] * (1.0 + sc1) + sh1
    hb = hn.astype(BF16)

    pc = jnp.dot(hb, win_ref[:, 0:3 * D_CONV], preferred_element_type=F32)
    u = pc[:, 2 * D_CONV:3 * D_CONV] * pc[:, 0:D_CONV]
    first = (i % tiles_per_seq) == 0

    @pl.when(first)
    def _():
        ubuf[0:8, :] = jnp.zeros((8, D_CONV), F32)

    @pl.when(jnp.logical_not(first))
    def _():
        ubuf[0:8, :] = ubuf[tm:tm + 8, :]

    ubuf[8:8 + tm, :] = u
    u1 = ubuf[7:7 + tm, :]
    u2 = ubuf[6:6 + tm, :]
    cw = convw_ref[...]
    yc = pc[:, D_CONV:2 * D_CONV] * (cw[0:1] * u2 + cw[1:2] * u1 + cw[2:3] * u)
    yconv_ref[...] = (_rms(yc, NORM_EPS) * gconv_ref[...]).astype(BF16)

    half = ROT_DIM // 2
    ang = invf_ref[...] * pos_ref[...].astype(F32)
    cos_ft = jnp.cos(ang)
    sin_ft = jnp.sin(ang)
    zero_h = jnp.zeros((half, tm), F32)
    zero_r = jnp.zeros((HEAD_DIM - ROT_DIM, tm), F32)
    lay = lambda a, b, r: jnp.concatenate([a, b, r, a, b, r], axis=0).T
    cos = lay(cos_ft, cos_ft, zero_r + 1.0)
    s_up = lay(-sin_ft, zero_h, zero_r)
    s_dn = lay(zero_h, sin_ft, zero_r)

    def rope(t):
        return t * cos + pltpu.roll(t, LANES - half, axis=1) * s_up + pltpu.roll(t, half, axis=1) * s_dn

    pq = jnp.dot(hb, win_ref[:, 3 * D_CONV:3 * D_CONV + D_QK], preferred_element_type=F32)
    pk = jnp.dot(hb, win_ref[:, 3 * D_CONV + D_QK:3 * D_CONV + 2 * D_QK], preferred_element_type=F32)
    scale = HEAD_DIM ** -0.5 * math.log2(math.e)
    qs = []
    for h in range(N_HEADS):
        sl = slice(h * LANES, (h + 1) * LANES)
        qs.append(rope(pq[:, sl]) * scale)
        k_ref[:, sl] = rope(pk[:, sl]).astype(BF16)
    qt_ref[0] = jnp.concatenate(qs, axis=1).T.astype(BF16)
    pv = jnp.dot(hb, win_ref[:, 3 * D_CONV + 2 * D_QK:], preferred_element_type=F32)
    vt_ref[0] = pv.T.astype(BF16)


def _inproj(xf, pos, ada3, g1, win_b, conv_w, g_conv, invf, seq):
    t, d = xf.shape
    tm = min(512, seq)
    tiles_per_seq = seq // tm
    n_in = win_b.shape[1]
    full = lambda shape: pl.BlockSpec(shape, lambda i: (0,) * len(shape))
    row = lambda w: pl.BlockSpec((tm, w), lambda i: (i, 0))
    colmajor = lambda w: pl.BlockSpec((1, w, tm), lambda i: (i // tiles_per_seq, 0, i % tiles_per_seq))
    return pl.pallas_call(
        functools.partial(_inproj_kernel, tiles_per_seq),
        grid=(t // tm,),
        in_specs=[row(d), pl.BlockSpec((1, tm), lambda i: (0, i)),
                  pl.BlockSpec((1, 6, d), lambda i: (i // tiles_per_seq, 0, 0)),
                  full((1, d)), full((d, n_in)), full((CONV_WIDTH, D_CONV)), full((1, D_CONV)),
                  full((ROT_DIM // 2, 1))],
        out_specs=[row(D_CONV), colmajor(D_QK), row(D_QK), colmajor(D_ATTN)],
        out_shape=[jax.ShapeDtypeStruct((t, D_CONV), BF16), jax.ShapeDtypeStruct((t // seq, D_QK, seq), BF16),
                   jax.ShapeDtypeStruct((t, D_QK), BF16), jax.ShapeDtypeStruct((t // seq, D_ATTN, seq), BF16)],
        scratch_shapes=[pltpu.VMEM((tm + 8, D_CONV), F32)],
        compiler_params=_cparams(("arbitrary",)),
        name="inproj",
    )(xf, pos, ada3, g1, win_b, conv_w, g_conv, invf)


def _attn_kernel(qt_ref, k_ref, vt_ref, lamp_ref, gs_ref, o_ref, qq_ref, m_ref, l_ref, acc_ref, s0_ref, s1_ref):
    i = pl.program_id(2)
    tq = qt_ref.shape[2]
    tk = tq
    qt = qt_ref[0]
    feat = lax.broadcasted_iota(I32, qt.shape, 0)
    zero = jnp.zeros_like(qt)
    qq_ref[:, 0:tq] = jnp.where(feat < HEAD_DIM, qt, zero)
    qq_ref[:, tq:2 * tq] = jnp.where(feat >= HEAD_DIM, qt, zero)
    m_ref[...] = jnp.full(m_ref.shape, NEG_BIG, F32)
    l_ref[...] = jnp.zeros(l_ref.shape, F32)
    acc_ref[...] = jnp.zeros(acc_ref.shape, F32)

    def scores(j, dst):
        start = pl.multiple_of(j * tk, tk)
        dst[...] = jnp.dot(k_ref[pl.ds(start, tk), :], qq_ref[...], preferred_element_type=F32)

    def consume(src, j, masked):
        start = pl.multiple_of(j * tk, tk)
        vtb = vt_ref[0, :, pl.ds(start, tk)]
        s = src[...]
        if masked:
            key = lax.broadcasted_iota(I32, s.shape, 0)
            col = lax.broadcasted_iota(I32, s.shape, 1)
            qpos = jnp.where(col >= tq, col - tq, col)
            s = jnp.where(key <= qpos, s, NEG_BIG)
        m_prev = m_ref[...]
        m_new = jnp.maximum(m_prev, jnp.max(s, axis=0, keepdims=True))
        alpha = jnp.exp2(m_prev - m_new)
        p = jnp.exp2(s - m_new)
        l_ref[...] = alpha * l_ref[...] + jnp.sum(p, axis=0, keepdims=True)
        acc_ref[...] = alpha * acc_ref[...] + jnp.dot(vtb, p.astype(BF16), preferred_element_type=F32)
        m_ref[...] = m_new

    scores(0, s0_ref)

    def pair(jj, carry):
        j = 2 * jj
        scores(j + 1, s1_ref)
        consume(s0_ref, j, False)
        scores(j + 2, s0_ref)
        consume(s1_ref, j + 1, False)
        return carry

    lax.fori_loop(0, i // 2, pair, 0)

    @pl.when(i % 2 == 1)
    def _():
        scores(i, s1_ref)
        consume(s0_ref, i - 1, False)
        consume(s1_ref, i, True)

    @pl.when(i % 2 == 0)
    def _():
        consume(s0_ref, i, True)

    o = acc_ref[...] / l_ref[...]
    lamp = lamp_ref[...]
    lam = (jnp.exp(jnp.sum(lamp[0:1] * lamp[1:2], axis=1, keepdims=True))
           - jnp.exp(jnp.sum(lamp[2:3] * lamp[3:4], axis=1, keepdims=True)) + LAMBDA_INIT)
    od = o[:, 0:tq] - lam * o[:, tq:2 * tq]
    y = od * lax.rsqrt(jnp.mean(od * od, axis=0, keepdims=True) + SUBLN_EPS) * gs_ref[...] * (1.0 - LAMBDA_INIT)
    o_ref[...] = y.T.astype(BF16)


def _attention(qt, k, vt, lamp, g_subln, bsz, seq):
    t = k.shape[0]
    tq = min(512, seq)
    nq = seq // tq
    return pl.pallas_call(
        _attn_kernel,
        grid=(bsz, N_HEADS, nq),
        in_specs=[pl.BlockSpec((1, LANES, tq), lambda b, h, i: (b, h, i)),
                  pl.BlockSpec((seq, LANES), lambda b, h, i: (b, h)),
                  pl.BlockSpec((1, V_DIM, seq), lambda b, h, i: (b, h, 0)),
                  pl.BlockSpec((4, HEAD_DIM), lambda b, h, i: (0, 0)),
                  pl.BlockSpec((V_DIM, 1), lambda b, h, i: (0, 0))],
        out_specs=pl.BlockSpec((tq, LANES), lambda b, h, i: (b * nq + i, h)),
        out_shape=jax.ShapeDtypeStruct((t, D_ATTN), BF16),
        scratch_shapes=[pltpu.VMEM((LANES, 2 * tq), BF16), pltpu.VMEM((1, 2 * tq), F32),
                        pltpu.VMEM((1, 2 * tq), F32), pltpu.VMEM((V_DIM, 2 * tq), F32),
                        pltpu.VMEM((tq, 2 * tq), F32), pltpu.VMEM((tq, 2 * tq), F32)],
        compiler_params=_cparams(("arbitrary", "arbitrary", "arbitrary")),
        name="attention",
    )(qt, k, vt, lamp, g_subln)


def _first_index(hit, idx, size, axis):
    return jnp.min(jnp.where(hit, idx, size), axis=axis, keepdims=True)


def _outproj_router_kernel(yc_ref, ya_ref, x_ref, ada_ref, g2_ref, wout_ref, wrt_ref, rb_ref,
                           x1_ref, hp_ref, idx_ref, wts_ref, rank_ref, cnt_ref, base_ref):
    i = pl.program_id(0)
    tm = x_ref.shape[0]
    ada = ada_ref[0]
    gt1, sh2, sc2 = ada[2:3], ada[3:4], ada[4:5]
    mix = (jnp.dot(yc_ref[...], wout_ref[0:D_CONV, :], preferred_element_type=F32)
           + jnp.dot(ya_ref[...], wout_ref[D_CONV:, :], preferred_element_type=F32))
    x1 = x_ref[...] + gt1 * mix
    x1_ref[...] = x1
    hn2 = _rms(x1, NORM_EPS) * g2_ref[...] * (1.0 + sc2) + sh2
    hp_ref[...] = _pack_bf16_pairs(hn2)
    hb = hn2.astype(BF16)

    logits = lax.dot_general(wrt_ref[...], hb, (((1,), (1,)), ((), ())), preferred_element_type=F32)
    scores = jax.nn.sigmoid(logits)
    choice = scores + rb_ref[...]
    neg_inf = jnp.float32(-jnp.inf)

    ch3 = choice.reshape(N_GROUPS, GROUP_SIZE, tm)
    i3 = lax.broadcasted_iota(I32, ch3.shape, 1)
    m1 = jnp.max(ch3, axis=1, keepdims=True)
    f1 = _first_index(ch3 == m1, i3, GROUP_SIZE, 1)
    m2 = jnp.max(jnp.where(i3 == f1, neg_inf, ch3), axis=1, keepdims=True)
    gs = (m1 + m2).reshape(N_GROUPS, tm)

    gi = lax.broadcasted_iota(I32, gs.shape, 0)
    gkeep = jnp.zeros(gs.shape, F32)
    for _ in range(TOPK_GROUPS):
        m = jnp.max(gs, axis=0, keepdims=True)
        f = _first_index(gs == m, gi, N_GROUPS, 0)
        sel = gi == f
        gkeep = jnp.where(sel, 1.0, gkeep)
        gs = jnp.where(sel, neg_inf, gs)
    ekeep = jnp.broadcast_to(gkeep.reshape(N_GROUPS, 1, tm), (N_GROUPS, GROUP_SIZE, tm)).reshape(N_EXPERTS, tm)
    masked = jnp.where(ekeep > 0.0, choice, neg_inf)

    ei = lax.broadcasted_iota(I32, masked.shape, 0)
    picked = jnp.zeros(masked.shape, F32)
    idxs, ws = [], []
    for _ in range(TOP_K):
        m = jnp.max(masked, axis=0, keepdims=True)
        f = _first_index(masked == m, ei, N_EXPERTS, 0)
        sel = ei == f
        idxs.append(f)
        ws.append(jnp.sum(jnp.where(sel, scores, 0.0), axis=0, keepdims=True))
        picked = jnp.where(sel, 1.0, picked)
        masked = jnp.where(sel, neg_inf, masked)
    wsum = ws[0]
    for wk in ws[1:]:
        wsum = wsum + wk
    denom = wsum + 1e-20
    for kk in range(TOP_K):
        idx_ref[kk:kk + 1, :] = idxs[kk]
        wts_ref[kk:kk + 1, :] = ws[kk] / denom * ROUTED_SCALE

    @pl.when(i == 0)
    def _():
        base_ref[...] = jnp.zeros(base_ref.shape, F32)

    si = lax.broadcasted_iota(I32, (tm, tm), 0)
    ti = lax.broadcasted_iota(I32, (tm, tm), 1)
    earlier = jnp.where(si < ti, 1.0, 0.0).astype(BF16)
    before = jnp.dot(picked.astype(BF16), earlier, preferred_element_type=F32) + base_ref[...]
    for kk in range(TOP_K):
        rank_ref[kk:kk + 1, :] = jnp.sum(jnp.where(ei == idxs[kk], before, 0.0), axis=0, keepdims=True).astype(I32)
    base_ref[...] = base_ref[...] + jnp.sum(picked, axis=1, keepdims=True)
    cnt_ref[...] = base_ref[...]


def _outproj_router(yconv, yattn, xf, ada3, g2, wout_b, wrt_b, rbias, seq):
    t, d = xf.shape
    tm = min(512, seq)
    tiles_per_seq = seq // tm
    full = lambda shape: pl.BlockSpec(shape, lambda i: (0,) * len(shape))
    row = lambda w: pl.BlockSpec((tm, w), lambda i: (i, 0))
    col = pl.BlockSpec((TOP_K, tm), lambda i: (0, i))
    return pl.pallas_call(
        _outproj_router_kernel,
        grid=(t // tm,),
        in_specs=[row(D_CONV), row(D_ATTN), row(d),
                  pl.BlockSpec((1, 6, d), lambda i: (i // tiles_per_seq, 0, 0)),
                  full((1, d)), full((D_CONV + D_ATTN, d)), full((N_EXPERTS, d)), full((N_EXPERTS, 1))],
        out_specs=[row(d), row(d // 2), col, col, col, full((N_EXPERTS, 1))],
        out_shape=[jax.ShapeDtypeStruct((t, d), F32), jax.ShapeDtypeStruct((t, d // 2), I32),
                   jax.ShapeDtypeStruct((TOP_K, t), I32), jax.ShapeDtypeStruct((TOP_K, t), F32),
                   jax.ShapeDtypeStruct((TOP_K, t), I32), jax.ShapeDtypeStruct((N_EXPERTS, 1), F32)],
        scratch_shapes=[pltpu.VMEM((N_EXPERTS, 1), F32)],
        compiler_params=_cparams(("arbitrary",)),
        name="outproj_router",
    )(yconv, yattn, xf, ada3, g2, wout_b, wrt_b, rbias)


def _dest_kernel(idx_ref, rank_ref, ps_ref, dest_ref):
    tm = idx_ref.shape[1]
    ei = lax.broadcasted_iota(I32, (N_EXPERTS, tm), 0)
    ps = ps_ref[...]
    for kk in range(TOP_K):
        start = jnp.sum(jnp.where(ei == idx_ref[kk:kk + 1, :], ps, 0.0), axis=0, keepdims=True)
        dest_ref[kk:kk + 1, :] = start.astype(I32) + rank_ref[kk:kk + 1, :]


def _dest(idx_t, rank_t, pad_start_f):
    t = idx_t.shape[1]
    tm = min(2048, t)
    col = pl.BlockSpec((TOP_K, tm), lambda i: (0, i))
    return pl.pallas_call(
        _dest_kernel,
        grid=(t // tm,),
        in_specs=[col, col, pl.BlockSpec((N_EXPERTS, 1), lambda i: (0, 0))],
        out_specs=col,
        out_shape=jax.ShapeDtypeStruct((TOP_K, t), I32),
        compiler_params=_cparams(("arbitrary",)),
        name="dest",
    )(idx_t, rank_t, pad_start_f)


def _sc_dispatch(hp, dest_ck, n_rows):
    t, w = hp.shape
    nk = dest_ck.shape[1]
    per_worker = t // SC_CHUNK // SC_WORKERS
    mesh = plsc.VectorSubcoreMesh(core_axis_name="c", subcore_axis_name="s")

    @functools.partial(
        pl.kernel, mesh=mesh,
        out_type=jax.ShapeDtypeStruct((n_rows, w), hp.dtype),
        scratch_types=[pltpu.VMEM((nk, SC_CHUNK), I32), pltpu.VMEM((SC_CHUNK, w), hp.dtype), pltpu.SemaphoreType.DMA],
    )
    def k(hp_hbm, dest_hbm, xs_hbm, idx_v, rows_v, sem):
        wid = lax.axis_index("s") * SC_CORES + lax.axis_index("c")

        @pl.loop(0, per_worker)
        def _(j):
            c = wid * per_worker + j
            pltpu.sync_copy(dest_hbm.at[c], idx_v)
            pltpu.sync_copy(hp_hbm.at[pl.ds(c * SC_CHUNK, SC_CHUNK)], rows_v)
            for kk in range(nk):
                pltpu.async_copy(rows_v, xs_hbm.at[idx_v.at[kk]], sem)
            for kk in range(nk):
                pltpu.make_async_copy(rows_v, xs_hbm.at[idx_v.at[kk]], sem).wait()

    return k(hp, dest_ck)


def _sc_gather(src, idx_c):
    n_chunks = idx_c.shape[0]
    w = src.shape[1]
    per_worker = n_chunks // SC_WORKERS
    mesh = plsc.VectorSubcoreMesh(core_axis_name="c", subcore_axis_name="s")

    @functools.partial(
        pl.kernel, mesh=mesh,
        out_type=jax.ShapeDtypeStruct((n_chunks * SC_CHUNK, w), src.dtype),
        scratch_types=[pltpu.VMEM((1, SC_CHUNK), I32), pltpu.VMEM((SC_CHUNK, w), src.dtype), pltpu.SemaphoreType.DMA],
    )
    def k(src_hbm, idx_hbm, out_hbm, idx_v, rows_v, sem):
        wid = lax.axis_index("s") * SC_CORES + lax.axis_index("c")

        @pl.loop(0, per_worker)
        def _(j):
            c = wid * per_worker + j
            pltpu.sync_copy(idx_hbm.at[c], idx_v)
            pltpu.async_copy(src_hbm.at[idx_v.at[0]], rows_v, sem).wait()
            pltpu.sync_copy(rows_v, out_hbm.at[pl.ds(c * SC_CHUNK, SC_CHUNK)])

    return k(src, idx_c)


def _sc_pack_weights(w2d):
    rows, width = w2d.shape
    half = width // 2
    lanes = 16
    chunk = SC_PACK_CHUNK_WORDS // width
    per_worker = rows // chunk // SC_WORKERS
    mesh = plsc.VectorSubcoreMesh(core_axis_name="c", subcore_axis_name="s")

    @functools.partial(
        pl.kernel, mesh=mesh,
        out_type=jax.ShapeDtypeStruct((rows, half), I32),
        scratch_types=[pltpu.VMEM((2, chunk, width), F32), pltpu.VMEM((2, chunk, half), I32),
                       pltpu.SemaphoreType.DMA((2,)), pltpu.SemaphoreType.DMA((2,))],
        compiler_params=pltpu.CompilerParams(needs_layout_passes=False),
    )
    def k(w_hbm, o_hbm, inb, outb, sem_in, sem_out):
        wid = lax.axis_index("s") * SC_CORES + lax.axis_index("c")
        base = wid * per_worker

        def in_copy(c, slot):
            return pltpu.make_async_copy(w_hbm.at[pl.ds((base + c) * chunk, chunk)], inb.at[slot], sem_in.at[slot])

        def out_copy(c, slot):
            return pltpu.make_async_copy(outb.at[slot], o_hbm.at[pl.ds((base + c) * chunk, chunk)], sem_out.at[slot])

        in_copy(0, 0).start()

        @pl.loop(0, per_worker, step=2)
        def _(c0):
            for slot in range(2):
                c = c0 + slot
                in_copy(c, slot).wait()

                @pl.when(c + 1 < per_worker)
                def _():
                    in_copy(c + 1, 1 - slot).start()

                @pl.when(c >= 2)
                def _():
                    out_copy(c - 2, slot).wait()

                @plsc.parallel_loop(0, chunk, unroll=SC_PACK_UNROLL)
                def _(r):
                    for j in range(half // lanes):
                        lo = inb[slot, r, pl.ds(j * lanes, lanes)]
                        hi = inb[slot, r, pl.ds(half + j * lanes, lanes)]
                        packed = plsc.pack(lo, hi, format=plsc.PackFormat.INTERLEAVED)
                        outb[slot, r, pl.ds(j * lanes, lanes)] = plsc.bitcast(packed, I32)

                out_copy(c, slot).start()

        out_copy(per_worker - 2, 0).wait()
        out_copy(per_worker - 1, 1).wait()

    return k(w2d)


def _experts_kernel(bstart_ref, xs_hbm, wg_ref, wu_ref, wd_ref, out_hbm,
                    wgu_b, wd_b, xbuf, obuf, sem_in, sem_out):
    e = pl.program_id(0)
    b0 = bstart_ref[e]
    b1 = bstart_ref[e + 1]
    n_used = bstart_ref[N_EXPERTS]
    nbuf = EXPERT_BUFS

    def in_copy(g, slot):
        return pltpu.make_async_copy(xs_hbm.at[pl.ds(g * ROW_TILE, ROW_TILE)], xbuf.at[slot], sem_in.at[slot])

    def out_copy(g, slot):
        return pltpu.make_async_copy(obuf.at[slot], out_hbm.at[pl.ds(g * ROW_TILE, ROW_TILE)], sem_out.at[slot])

    @pl.when(e == 0)
    def _():
        for g in range(nbuf - 1):
            @pl.when(g < n_used)
            def _():
                in_copy(g, g).start()

    @pl.when(b1 > b0)
    def _():
        wgu_b[:, 0:D_EXPERT] = _unpack_bf16_pairs(wg_ref[...]).astype(BF16)
        wgu_b[:, D_EXPERT:2 * D_EXPERT] = _unpack_bf16_pairs(wu_ref[...]).astype(BF16)
        wd_b[...] = _unpack_bf16_pairs(wd_ref[...]).astype(BF16)

        def fetch(g):
            @pl.when(g < n_used)
            def _():
                in_copy(g, lax.rem(g, nbuf)).start()

        def run(blocks):
            slots = [lax.rem(g, nbuf) for g in blocks]
            for g, slot in zip(blocks, slots):
                in_copy(g, slot).wait()
            fetch(blocks[0] + nbuf - 1)
            for g, slot in zip(blocks, slots):
                @pl.when(g >= nbuf)
                def _():
                    out_copy(g - nbuf, slot).wait()
            xs_rows = [_unpack_bf16_pairs(xbuf[slot]).astype(BF16) for slot in slots]
            xb = xs_rows[0] if len(slots) == 1 else jnp.concatenate(xs_rows, axis=0)
            gu = jnp.dot(xb, wgu_b[...], preferred_element_type=F32)
            act = (jax.nn.silu(gu[:, 0:D_EXPERT]) * gu[:, D_EXPERT:2 * D_EXPERT]).astype(BF16)
            out = _pack_bf16_pairs(jnp.dot(act, wd_b[...], preferred_element_type=F32))
            for n, slot in enumerate(slots):
                obuf[slot] = out[n * ROW_TILE:(n + 1) * ROW_TILE]
            for g, slot in zip(blocks, slots):
                out_copy(g, slot).start()
            for g in blocks[:-1]:
                fetch(g + nbuf)

        def pair(jj, carry):
            g = b0 + 2 * jj
            run([g, g + 1])
            return carry

        lax.fori_loop(0, (b1 - b0) // 2, pair, 0)

        @pl.when((b1 - b0) % 2 == 1)
        def _():
            run([b1 - 1])

    @pl.when(e == N_EXPERTS - 1)
    def _():
        for back in range(1, nbuf + 1):
            @pl.when(n_used >= back)
            def _():
                g = n_used - back
                out_copy(g, lax.rem(g, nbuf)).wait()


def _experts(xs, bstart, w_gate, w_up, w_down):
    n_rows, half = xs.shape
    d = 2 * half
    weights = lambda shape: pl.BlockSpec((None,) + shape, lambda e, bs: (e, 0, 0))
    return pl.pallas_call(
        _experts_kernel,
        grid_spec=pltpu.PrefetchScalarGridSpec(
            num_scalar_prefetch=1,
            grid=(N_EXPERTS,),
            in_specs=[pl.BlockSpec(memory_space=pl.ANY),
                      weights((d, D_EXPERT // 2)), weights((d, D_EXPERT // 2)), weights((D_EXPERT, half))],
            out_specs=pl.BlockSpec(memory_space=pl.ANY),
            scratch_shapes=[pltpu.VMEM((d, 2 * D_EXPERT), BF16), pltpu.VMEM((D_EXPERT, d), BF16),
                            pltpu.VMEM((EXPERT_BUFS, ROW_TILE, half), I32),
                            pltpu.VMEM((EXPERT_BUFS, ROW_TILE, half), I32),
                            pltpu.SemaphoreType.DMA((EXPERT_BUFS,)), pltpu.SemaphoreType.DMA((EXPERT_BUFS,))],
        ),
        out_shape=jax.ShapeDtypeStruct((n_rows, half), I32),
        compiler_params=_cparams(("arbitrary",)),
        name="experts",
    )(bstart, xs, w_gate, w_up, w_down)


def _final_kernel(x1_ref, hp_ref, ou_ref, wts_ref, ada_ref, wgs_ref, wus_ref, wds_ref, gf_ref, o_ref):
    nchunk = ou_ref.shape[0]
    gt2 = ada_ref[0][5:6]
    hb = _unpack_bf16_pairs(hp_ref[...]).astype(BF16)
    g = jnp.dot(hb, wgs_ref[...], preferred_element_type=F32)
    u = jnp.dot(hb, wus_ref[...], preferred_element_type=F32)
    y = jnp.dot((jax.nn.silu(g) * u).astype(BF16), wds_ref[...], preferred_element_type=F32)
    wts = wts_ref[...]
    parts = []
    for c in range(nchunk):
        rs = slice(c * SC_CHUNK, (c + 1) * SC_CHUNK)
        yr = wts[rs, 0:1] * _unpack_bf16_pairs(ou_ref[c, 0])
        for kk in range(1, TOP_K):
            yr = yr + wts[rs, kk:kk + 1] * _unpack_bf16_pairs(ou_ref[c, kk])
        parts.append(yr)
    y = y + (parts[0] if nchunk == 1 else jnp.concatenate(parts, axis=0))
    x2 = x1_ref[...] + gt2 * y
    o_ref[...] = _rms(x2, NORM_EPS) * gf_ref[...]


def _final(x1, hp, outu4, wts, ada3, wgs_b, wus_b, wds_b, g_final, seq):
    t, d = x1.shape
    tm = min(256, seq)
    nchunk = tm // SC_CHUNK
    tiles_per_seq = seq // tm
    full = lambda shape: pl.BlockSpec(shape, lambda i: (0,) * len(shape))
    row = lambda w: pl.BlockSpec((tm, w), lambda i: (i, 0))
    return pl.pallas_call(
        _final_kernel,
        grid=(t // tm,),
        in_specs=[row(d), row(d // 2),
                  pl.BlockSpec((nchunk, TOP_K, SC_CHUNK, d // 2), lambda i: (i, 0, 0, 0)),
                  row(TOP_K),
                  pl.BlockSpec((1, 6, d), lambda i: (i // tiles_per_seq, 0, 0)),
                  full((d, D_EXPERT)), full((d, D_EXPERT)), full((D_EXPERT, d)), full((1, d))],
        out_specs=row(d),
        out_shape=jax.ShapeDtypeStruct((t, d), F32),
        compiler_params=_cparams(("arbitrary",)),
        name="final",
    )(x1, hp, outu4, wts, ada3, wgs_b, wus_b, wds_b, g_final)


def kernel(x, c, positions, w_ada, b_ada, g_norm1, w_in, conv_w, g_conv_out, lam_q1, lam_k1, lam_q2, lam_k2, g_subln, w_out, g_norm2, w_router, router_bias, w_gate_e, w_up_e, w_down_e, w_gate_s, w_up_s, w_down_s, g_final):
    bsz, seq, d = x.shape
    t = bsz * seq
    xf = x.reshape(t, d)
    pos = positions.reshape(1, t)
    invf = (ROPE_THETA ** (-jnp.arange(0, ROT_DIM, 2, dtype=F32) / ROT_DIM)).reshape(ROT_DIM // 2, 1)

    ada3 = _ada(c, w_ada[0], b_ada[0]).reshape(bsz, 6, d)
    yconv, qt, k, vt = _inproj(xf, pos, ada3, g_norm1[0].reshape(1, d), w_in[0].astype(BF16), conv_w[0],
                             g_conv_out[0].reshape(1, D_CONV), invf, seq)
    lamp = jnp.stack([lam_q1[0], lam_k1[0], lam_q2[0], lam_k2[0]]).astype(F32)
    yattn = _attention(qt, k, vt, lamp, g_subln[0].reshape(V_DIM, 1), bsz, seq)

    x1, hp, idx_t, wts_t, rank_t, counts_f = _outproj_router(
        yconv, yattn, xf, ada3, g_norm2[0].reshape(1, d), w_out[0].astype(BF16),
        w_router[0].T.astype(BF16), router_bias[0].reshape(N_EXPERTS, 1), seq)

    counts = counts_f[:, 0].astype(I32)
    padded = ((counts + ROW_TILE - 1) // ROW_TILE) * ROW_TILE
    pad_end = jnp.cumsum(padded)
    pad_start = pad_end - padded
    nb = (t * TOP_K + N_EXPERTS * (ROW_TILE - 1)) // ROW_TILE
    bstart = (jnp.concatenate([pad_start, pad_end[-1:]]) // ROW_TILE).astype(I32)

    dest_t = _dest(idx_t, rank_t, pad_start.astype(F32).reshape(N_EXPERTS, 1))
    n_chunks = t // SC_CHUNK
    dest_ck = dest_t.reshape(TOP_K, n_chunks, SC_CHUNK).transpose(1, 0, 2)

    def pack3(w):
        e, r, wd = w.shape
        return _sc_pack_weights(w.reshape(e * r, wd)).reshape(e, r, wd // 2)

    wg_p, wu_p, wd_p = pack3(w_gate_e[0]), pack3(w_up_e[0]), pack3(w_down_e[0])

    xs = _sc_dispatch(hp, dest_ck, nb * ROW_TILE)
    outs = _experts(xs, bstart, wg_p, wu_p, wd_p)
    outu = _sc_gather(outs, dest_ck.reshape(n_chunks * TOP_K, 1, SC_CHUNK))

    out = _final(x1, hp, outu.reshape(n_chunks, TOP_K, SC_CHUNK, d // 2), wts_t.T, ada3,
                 w_gate_s[0].astype(BF16), w_up_s[0].astype(BF16), w_down_s[0].astype(BF16),
                 g_final.reshape(1, d), seq)
    return out.reshape(bsz, seq, d)
```

```python
import functools
import math

import jax
import jax.numpy as jnp
from jax import lax
from jax.experimental import pallas as pl
from jax.experimental.pallas import tpu as pltpu
from jax.experimental.pallas import tpu_sc as plsc

F32 = jnp.float32
BF16 = jnp.bfloat16
I32 = jnp.int32

D_CONV = 512
CONV_WIDTH = 3
N_HEADS = 4
HEAD_DIM = 64
V_DIM = 2 * HEAD_DIM
D_ATTN = N_HEADS * V_DIM
D_QK = N_HEADS * 2 * HEAD_DIM
ROT_DIM = HEAD_DIM // 4
ROPE_THETA = 500000.0
N_EXPERTS = 256
TOP_K = 8
N_GROUPS = 8
GROUP_SIZE = N_EXPERTS // N_GROUPS
TOPK_GROUPS = 4
D_EXPERT = 256
ROUTED_SCALE = 2.5
NORM_EPS = 1e-6
SUBLN_EPS = 1e-5
LAMBDA_INIT = 0.8 - 0.6 * math.exp(-0.3 * 0)

LANES = 128
SC_CORES = 2
SC_SUBCORES = 16
SC_WORKERS = SC_CORES * SC_SUBCORES
SC_CHUNK = 128

ROW_TILE = 128
EXPERT_GROUP = 4
EXPERT_BUFS = 12
NEG_BIG = -1e30
VMEM_LIMIT = 56 * 1024 * 1024


def _cparams(sem):
    return pltpu.CompilerParams(dimension_semantics=sem, vmem_limit_bytes=VMEM_LIMIT)


def _rms(x, eps):
    return x * lax.rsqrt(jnp.mean(x * x, axis=-1, keepdims=True) + eps)


def _pack_bf16_pairs(x):
    n = x.shape[1] // 2
    bits = lax.bitcast_convert_type(x.astype(BF16).astype(F32), I32)
    lo = lax.shift_right_logical(bits[:, :n], 16)
    return lo | bits[:, n:]


def _unpack_bf16_pairs(p):
    lo = lax.bitcast_convert_type(lax.shift_left(p, 16), F32)
    hi = lax.bitcast_convert_type(p & jnp.int32(-65536), F32)
    return jnp.concatenate([lo, hi], axis=1)


def _ada_kernel(c_ref, w_ref, b_ref, o_ref):
    ca = jax.nn.silu(c_ref[...])
    o_ref[...] = jnp.dot(ca.astype(BF16), w_ref[...].astype(BF16), preferred_element_type=F32) + b_ref[...]


def _ada(c, w_ada, b_ada):
    bsz, d = c.shape
    n = w_ada.shape[1]
    tn = n // 4
    return pl.pallas_call(
        _ada_kernel,
        grid=(n // tn,),
        in_specs=[pl.BlockSpec((bsz, d), lambda j: (0, 0)),
                  pl.BlockSpec((d, tn), lambda j: (0, j)),
                  pl.BlockSpec((1, tn), lambda j: (0, j))],
        out_specs=pl.BlockSpec((bsz, tn), lambda j: (0, j)),
        out_shape=jax.ShapeDtypeStruct((bsz, n), F32),
        compiler_params=_cparams(("arbitrary",)),
        name="ada",
    )(c, w_ada, b_ada.reshape(1, n))


def _inproj_kernel(tiles_per_seq, x_ref, pos_ref, ada_ref, g1_ref, win_ref, convw_ref, gconv_ref, invf_ref,
                   yconv_ref, qt_ref, k_ref, vt_ref, ubuf):
    i = pl.program_id(0)
    tm = x_ref.shape[0]
    x = x_ref[...]
    ada = ada_ref[0]
    sh1, sc1 = ada[0:1], ada[1:2]
    hn = _rms(x, NORM_EPS) * g1_ref[...] * (1.0 + sc1) + sh1
    hb = hn.astype(BF16)

    pc = jnp.dot(hb, win_ref[:, 0:3 * D_CONV], preferred_element_type=F32)
    u = pc[:, 2 * D_CONV:3 * D_CONV] * pc[:, 0:D_CONV]
    first = (i % tiles_per_seq) == 0

    @pl.when(first)
    def _():
        ubuf[0:8, :] = jnp.zeros((8, D_CONV), F32)

    @pl.when(jnp.logical_not(first))
    def _():
        ubuf[0:8, :] = ubuf[tm:tm + 8, :]

    ubuf[8:8 + tm, :] = u
    u1 = ubuf[7:7 + tm, :]
    u2 = ubuf[6:6 + tm, :]
    cw = convw_ref[...]
    yc = pc[:, D_CONV:2 * D_CONV] * (cw[0:1] * u2 + cw[1:2] * u1 + cw[2:3] * u)
    yconv_ref[...] = (_rms(yc, NORM_EPS) * gconv_ref[...]).astype(BF16)

    half = ROT_DIM // 2
    ang = invf_ref[...] * pos_ref[...].astype(F32)
    cos_ft = jnp.cos(ang)
    sin_ft = jnp.sin(ang)
    zero_h = jnp.zeros((half, tm), F32)
    zero_r = jnp.zeros((HEAD_DIM - ROT_DIM, tm), F32)
    lay = lambda a, b, r: jnp.concatenate([a, b, r, a, b, r], axis=0).T
    cos = lay(cos_ft, cos_ft, zero_r + 1.0)
    s_up = lay(-sin_ft, zero_h, zero_r)
    s_dn = lay(zero_h, sin_ft, zero_r)

    def rope(t):
        return t * cos + pltpu.roll(t, LANES - half, axis=1) * s_up + pltpu.roll(t, half, axis=1) * s_dn

    pq = jnp.dot(hb, win_ref[:, 3 * D_CONV:3 * D_CONV + D_QK], preferred_element_type=F32)
    pk = jnp.dot(hb, win_ref[:, 3 * D_CONV + D_QK:3 * D_CONV + 2 * D_QK], preferred_element_type=F32)
    scale = HEAD_DIM ** -0.5 * math.log2(math.e)
    qs = []
    for h in range(N_HEADS):
        sl = slice(h * LANES, (h + 1) * LANES)
        qs.append(rope(pq[:, sl]) * scale)
        k_ref[:, sl] = rope(pk[:, sl]).astype(BF16)
    qt_ref[0] = jnp.concatenate(qs, axis=1).T.astype(BF16)
    pv = jnp.dot(hb, win_ref[:, 3 * D_CONV + 2 * D_QK:], preferred_element_type=F32)
    vt_ref[0] = pv.T.astype(BF16)


def _inproj(xf, pos, ada3, g1, win_b, conv_w, g_conv, invf, seq):
    t, d = xf.shape
    tm = min(512, seq)
    tiles_per_seq = seq // tm
    n_in = win_b.shape[1]
    full = lambda shape: pl.BlockSpec(shape, lambda i: (0,) * len(shape))
    row = lambda w: pl.BlockSpec((tm, w), lambda i: (i, 0))
    colmajor = lambda w: pl.BlockSpec((1, w, tm), lambda i: (i // tiles_per_seq, 0, i % tiles_per_seq))
    return pl.pallas_call(
        functools.partial(_inproj_kernel, tiles_per_seq),
        grid=(t // tm,),
        in_specs=[row(d), pl.BlockSpec((1, tm), lambda i: (0, i)),
                  pl.BlockSpec((1, 6, d), lambda i: (i // tiles_per_seq, 0, 0)),
                  full((1, d)), full((d, n_in)), full((CONV_WIDTH, D_CONV)), full((1, D_CONV)),
                  full((ROT_DIM // 2, 1))],
        out_specs=[row(D_CONV), colmajor(D_QK), row(D_QK), colmajor(D_ATTN)],
        out_shape=[jax.ShapeDtypeStruct((t, D_CONV), BF16), jax.ShapeDtypeStruct((t // seq, D_QK, seq), BF16),
                   jax.ShapeDtypeStruct((t, D_QK), BF16), jax.ShapeDtypeStruct((t // seq, D_ATTN, seq), BF16)],
        scratch_shapes=[pltpu.VMEM((tm + 8, D_CONV), F32)],
        compiler_params=_cparams(("arbitrary",)),
        name="inproj",
    )(xf, pos, ada3, g1, win_b, conv_w, g_conv, invf)


def _attn_steps(nq):
    off = [(j, i) for j in range(nq) for i in range(j + 1, nq)]
    diag = [(i, i) for i in range(nq)]
    return off, diag


def _attn_kernel(steps_ref, qt_ref, k_ref, vt_ref, lamp_ref, gs_ref, o_ref,
                 qq_ref, m_ref, l_ref, acc_ref, s0_ref, s1_ref):
    nq = m_ref.shape[0]
    tq = s0_ref.shape[0]
    tk = tq
    off, diag = _attn_steps(nq)
    n_steps = len(off) + len(diag)
    bufs = (s0_ref, s1_ref)

    feat = lax.broadcasted_iota(I32, (LANES, tq), 0)
    for i in range(nq):
        qt = qt_ref[0, :, i * tq:(i + 1) * tq]
        zero = jnp.zeros_like(qt)
        qq_ref[:, 2 * i * tq:(2 * i + 1) * tq] = jnp.where(feat < HEAD_DIM, qt, zero)
        qq_ref[:, (2 * i + 1) * tq:(2 * i + 2) * tq] = jnp.where(feat >= HEAD_DIM, qt, zero)
    m_ref[...] = jnp.full(m_ref.shape, NEG_BIG, F32)
    l_ref[...] = jnp.zeros(l_ref.shape, F32)
    acc_ref[...] = jnp.zeros(acc_ref.shape, F32)

    def scores(j, i, dst):
        kstart = pl.multiple_of(j * tk, tk)
        qstart = pl.multiple_of(i * (2 * tq), 2 * tq)
        dst[...] = jnp.dot(k_ref[pl.ds(kstart, tk), :], qq_ref[:, pl.ds(qstart, 2 * tq)],
                           preferred_element_type=F32)

    def consume(src, j, i, masked):
        kstart = pl.multiple_of(j * tk, tk)
        vtb = vt_ref[0, :, pl.ds(kstart, tk)]
        s = src[...]
        if masked:
            key = lax.broadcasted_iota(I32, s.shape, 0)
            col = lax.broadcasted_iota(I32, s.shape, 1)
            qpos = jnp.where(col >= tq, col - tq, col)
            s = jnp.where(key <= qpos, s, NEG_BIG)
        m_prev = m_ref[i]
        m_new = jnp.maximum(m_prev, jnp.max(s, axis=0, keepdims=True))
        alpha = jnp.exp2(m_prev - m_new)
        p = jnp.exp2(s - m_new)
        l_ref[i] = alpha * l_ref[i] + jnp.sum(p, axis=0, keepdims=True)
        acc_ref[i] = alpha * acc_ref[i] + jnp.dot(vtb, p.astype(BF16), preferred_element_type=F32)
        m_ref[i] = m_new

    def at(n):
        return steps_ref[0, n], steps_ref[1, n]

    scores(*at(0), s0_ref)
    n_pairs = len(off) // 2

    def pair(pp, carry):
        n = 2 * pp
        scores(*at(n + 1), s1_ref)
        consume(s0_ref, *at(n), False)
        scores(*at(n + 2), s0_ref)
        consume(s1_ref, *at(n + 1), False)
        return carry

    lax.fori_loop(0, n_pairs, pair, 0)

    static_steps = (off + diag)[2 * n_pairs:]
    for n, (j, i) in enumerate(static_steps, start=2 * n_pairs):
        if n + 1 < n_steps:
            scores(*static_steps[n + 1 - 2 * n_pairs], bufs[(n + 1) % 2])
        consume(bufs[n % 2], j, i, masked=(j == i))

    lamp = lamp_ref[...]
    lam = (jnp.exp(jnp.sum(lamp[0:1] * lamp[1:2], axis=1, keepdims=True))
           - jnp.exp(jnp.sum(lamp[2:3] * lamp[3:4], axis=1, keepdims=True)) + LAMBDA_INIT)
    for i in range(nq):
        o = acc_ref[i] / l_ref[i]
        od = o[:, 0:tq] - lam * o[:, tq:2 * tq]
        y = od * lax.rsqrt(jnp.mean(od * od, axis=0, keepdims=True) + SUBLN_EPS) * gs_ref[...] * (1.0 - LAMBDA_INIT)
        o_ref[i * tq:(i + 1) * tq, :] = y.T.astype(BF16)


def _attention(qt, k, vt, lamp, g_subln, bsz, seq):
    t = k.shape[0]
    tq = min(512, seq)
    nq = seq // tq
    off, diag = _attn_steps(nq)
    steps = jnp.asarray(list(zip(*(off + diag))), I32)
    return pl.pallas_call(
        _attn_kernel,
        grid_spec=pltpu.PrefetchScalarGridSpec(
            num_scalar_prefetch=1,
            grid=(bsz, N_HEADS),
            in_specs=[pl.BlockSpec((1, LANES, seq), lambda b, h, st: (b, h, 0)),
                      pl.BlockSpec((seq, LANES), lambda b, h, st: (b, h)),
                      pl.BlockSpec((1, V_DIM, seq), lambda b, h, st: (b, h, 0)),
                      pl.BlockSpec((4, HEAD_DIM), lambda b, h, st: (0, 0)),
                      pl.BlockSpec((V_DIM, 1), lambda b, h, st: (0, 0))],
            out_specs=pl.BlockSpec((seq, LANES), lambda b, h, st: (b, h)),
            scratch_shapes=[pltpu.VMEM((LANES, 2 * seq), BF16), pltpu.VMEM((nq, 1, 2 * tq), F32),
                            pltpu.VMEM((nq, 1, 2 * tq), F32), pltpu.VMEM((nq, V_DIM, 2 * tq), F32),
                            pltpu.VMEM((tq, 2 * tq), F32), pltpu.VMEM((tq, 2 * tq), F32)],
        ),
        out_shape=jax.ShapeDtypeStruct((t, D_ATTN), BF16),
        compiler_params=_cparams(("arbitrary", "arbitrary")),
        name="attention",
    )(steps, qt, k, vt, lamp, g_subln)


def _first_index(hit, idx, size, axis):
    return jnp.min(jnp.where(hit, idx, size), axis=axis, keepdims=True)


def _outproj_router_kernel(yc_ref, ya_ref, x_ref, ada_ref, g2_ref, wout_ref, wrt_ref, rb_ref,
                           x1_ref, hp_ref, idx_ref, wts_ref, rank_ref, cnt_ref, base_ref):
    i = pl.program_id(0)
    tm = x_ref.shape[0]
    ada = ada_ref[0]
    gt1, sh2, sc2 = ada[2:3], ada[3:4], ada[4:5]
    mix = (jnp.dot(yc_ref[...], wout_ref[0:D_CONV, :], preferred_element_type=F32)
           + jnp.dot(ya_ref[...], wout_ref[D_CONV:, :], preferred_element_type=F32))
    x1 = x_ref[...] + gt1 * mix
    x1_ref[...] = x1
    hn2 = _rms(x1, NORM_EPS) * g2_ref[...] * (1.0 + sc2) + sh2
    hp_ref[...] = _pack_bf16_pairs(hn2)
    hb = hn2.astype(BF16)

    logits = lax.dot_general(wrt_ref[...], hb, (((1,), (1,)), ((), ())), preferred_element_type=F32)
    scores = jax.nn.sigmoid(logits)
    choice = scores + rb_ref[...]
    neg_inf = jnp.float32(-jnp.inf)

    ch3 = choice.reshape(N_GROUPS, GROUP_SIZE, tm)
    i3 = lax.broadcasted_iota(I32, ch3.shape, 1)
    m1 = jnp.max(ch3, axis=1, keepdims=True)
    f1 = _first_index(ch3 == m1, i3, GROUP_SIZE, 1)
    m2 = jnp.max(jnp.where(i3 == f1, neg_inf, ch3), axis=1, keepdims=True)
    gs = (m1 + m2).reshape(N_GROUPS, tm)

    gi = lax.broadcasted_iota(I32, gs.shape, 0)
    gkeep = jnp.zeros(gs.shape, F32)
    for _ in range(TOPK_GROUPS):
        m = jnp.max(gs, axis=0, keepdims=True)
        f = _first_index(gs == m, gi, N_GROUPS, 0)
        sel = gi == f
        gkeep = jnp.where(sel, 1.0, gkeep)
        gs = jnp.where(sel, neg_inf, gs)
    ekeep = jnp.broadcast_to(gkeep.reshape(N_GROUPS, 1, tm), (N_GROUPS, GROUP_SIZE, tm)).reshape(N_EXPERTS, tm)
    masked = jnp.where(ekeep > 0.0, choice, neg_inf)

    ei = lax.broadcasted_iota(I32, masked.shape, 0)
    picked = jnp.zeros(masked.shape, F32)
    idxs, ws = [], []
    for _ in range(TOP_K):
        m = jnp.max(masked, axis=0, keepdims=True)
        f = _first_index(masked == m, ei, N_EXPERTS, 0)
        sel = ei == f
        idxs.append(f)
        ws.append(jnp.sum(jnp.where(sel, scores, 0.0), axis=0, keepdims=True))
        picked = jnp.where(sel, 1.0, picked)
        masked = jnp.where(sel, neg_inf, masked)
    wsum = ws[0]
    for wk in ws[1:]:
        wsum = wsum + wk
    denom = wsum + 1e-20
    for kk in range(TOP_K):
        idx_ref[kk:kk + 1, :] = idxs[kk]
        wts_ref[kk:kk + 1, :] = ws[kk] / denom * ROUTED_SCALE

    @pl.when(i == 0)
    def _():
        base_ref[...] = jnp.zeros(base_ref.shape, F32)

    si = lax.broadcasted_iota(I32, (tm, tm), 0)
    ti = lax.broadcasted_iota(I32, (tm, tm), 1)
    earlier = jnp.where(si < ti, 1.0, 0.0).astype(BF16)
    before = jnp.dot(picked.astype(BF16), earlier, preferred_element_type=F32) + base_ref[...]
    for kk in range(TOP_K):
        rank_ref[kk:kk + 1, :] = jnp.sum(jnp.where(ei == idxs[kk], before, 0.0), axis=0, keepdims=True).astype(I32)
    base_ref[...] = base_ref[...] + jnp.sum(picked, axis=1, keepdims=True)
    cnt_ref[...] = base_ref[...]


def _outproj_router(yconv, yattn, xf, ada3, g2, wout_b, wrt_b, rbias, seq):
    t, d = xf.shape
    tm = min(512, seq)
    tiles_per_seq = seq // tm
    full = lambda shape: pl.BlockSpec(shape, lambda i: (0,) * len(shape))
    row = lambda w: pl.BlockSpec((tm, w), lambda i: (i, 0))
    col = pl.BlockSpec((TOP_K, tm), lambda i: (0, i))
    return pl.pallas_call(
        _outproj_router_kernel,
        grid=(t // tm,),
        in_specs=[row(D_CONV), row(D_ATTN), row(d),
                  pl.BlockSpec((1, 6, d), lambda i: (i // tiles_per_seq, 0, 0)),
                  full((1, d)), full((D_CONV + D_ATTN, d)), full((N_EXPERTS, d)), full((N_EXPERTS, 1))],
        out_specs=[row(d), row(d // 2), col, col, col, full((N_EXPERTS, 1))],
        out_shape=[jax.ShapeDtypeStruct((t, d), F32), jax.ShapeDtypeStruct((t, d // 2), I32),
                   jax.ShapeDtypeStruct((TOP_K, t), I32), jax.ShapeDtypeStruct((TOP_K, t), F32),
                   jax.ShapeDtypeStruct((TOP_K, t), I32), jax.ShapeDtypeStruct((N_EXPERTS, 1), F32)],
        scratch_shapes=[pltpu.VMEM((N_EXPERTS, 1), F32)],
        compiler_params=_cparams(("arbitrary",)),
        name="outproj_router",
    )(yconv, yattn, xf, ada3, g2, wout_b, wrt_b, rbias)


def _dest_kernel(idx_ref, rank_ref, ps_ref, dest_ref):
    tm = idx_ref.shape[1]
    ei = lax.broadcasted_iota(I32, (N_EXPERTS, tm), 0)
    ps = ps_ref[...]
    for kk in range(TOP_K):
        start = jnp.sum(jnp.where(ei == idx_ref[kk:kk + 1, :], ps, 0.0), axis=0, keepdims=True)
        dest_ref[kk:kk + 1, :] = start.astype(I32) + rank_ref[kk:kk + 1, :]


def _dest(idx_t, rank_t, pad_start_f):
    t = idx_t.shape[1]
    tm = min(2048, t)
    col = pl.BlockSpec((TOP_K, tm), lambda i: (0, i))
    return pl.pallas_call(
        _dest_kernel,
        grid=(t // tm,),
        in_specs=[col, col, pl.BlockSpec((N_EXPERTS, 1), lambda i: (0, 0))],
        out_specs=col,
        out_shape=jax.ShapeDtypeStruct((TOP_K, t), I32),
        compiler_params=_cparams(("arbitrary",)),
        name="dest",
    )(idx_t, rank_t, pad_start_f)


def _sc_dispatch(hp, dest_ck, n_rows):
    t, w = hp.shape
    nk = dest_ck.shape[1]
    per_worker = t // SC_CHUNK // SC_WORKERS
    mesh = plsc.VectorSubcoreMesh(core_axis_name="c", subcore_axis_name="s")

    @functools.partial(
        pl.kernel, mesh=mesh,
        out_type=jax.ShapeDtypeStruct((n_rows, w), hp.dtype),
        scratch_types=[pltpu.VMEM((nk, SC_CHUNK), I32), pltpu.VMEM((SC_CHUNK, w), hp.dtype), pltpu.SemaphoreType.DMA],
    )
    def k(hp_hbm, dest_hbm, xs_hbm, idx_v, rows_v, sem):
        wid = lax.axis_index("s") * SC_CORES + lax.axis_index("c")

        @pl.loop(0, per_worker)
        def _(j):
            c = wid * per_worker + j
            pltpu.sync_copy(dest_hbm.at[c], idx_v)
            pltpu.sync_copy(hp_hbm.at[pl.ds(c * SC_CHUNK, SC_CHUNK)], rows_v)
            for kk in range(nk):
                pltpu.async_copy(rows_v, xs_hbm.at[idx_v.at[kk]], sem)
            for kk in range(nk):
                pltpu.make_async_copy(rows_v, xs_hbm.at[idx_v.at[kk]], sem).wait()

    return k(hp, dest_ck)


def _sc_gather(src, idx_c):
    n_chunks = idx_c.shape[0]
    w = src.shape[1]
    per_worker = n_chunks // SC_WORKERS
    mesh = plsc.VectorSubcoreMesh(core_axis_name="c", subcore_axis_name="s")

    @functools.partial(
        pl.kernel, mesh=mesh,
        out_type=jax.ShapeDtypeStruct((n_chunks * SC_CHUNK, w), src.dtype),
        scratch_types=[pltpu.VMEM((1, SC_CHUNK), I32), pltpu.VMEM((SC_CHUNK, w), src.dtype), pltpu.SemaphoreType.DMA],
    )
    def k(src_hbm, idx_hbm, out_hbm, idx_v, rows_v, sem):
        wid = lax.axis_index("s") * SC_CORES + lax.axis_index("c")

        @pl.loop(0, per_worker)
        def _(j):
            c = wid * per_worker + j
            pltpu.sync_copy(idx_hbm.at[c], idx_v)
            pltpu.async_copy(src_hbm.at[idx_v.at[0]], rows_v, sem).wait()
            pltpu.sync_copy(rows_v, out_hbm.at[pl.ds(c * SC_CHUNK, SC_CHUNK)])

    return k(src, idx_c)


def _experts_kernel(bstart_ref, xs_hbm, wg_ref, wu_ref, wd_ref, out_hbm,
                    wgu_b, wd_b, xbuf, obuf, sem_in, sem_out):
    e = pl.program_id(0)
    b0 = bstart_ref[e]
    b1 = bstart_ref[e + 1]
    n_used = bstart_ref[N_EXPERTS]
    nbuf = EXPERT_BUFS

    def in_copy(g, slot):
        return pltpu.make_async_copy(xs_hbm.at[pl.ds(g * ROW_TILE, ROW_TILE)], xbuf.at[slot], sem_in.at[slot])

    def out_copy(g, slot):
        return pltpu.make_async_copy(obuf.at[slot], out_hbm.at[pl.ds(g * ROW_TILE, ROW_TILE)], sem_out.at[slot])

    @pl.when(e == 0)
    def _():
        for g in range(nbuf - 1):
            @pl.when(g < n_used)
            def _():
                in_copy(g, g).start()

    @pl.when(b1 > b0)
    def _():
        wgu_b[:, 0:D_EXPERT] = wg_ref[...].astype(BF16)
        wgu_b[:, D_EXPERT:2 * D_EXPERT] = wu_ref[...].astype(BF16)
        wd_b[...] = wd_ref[...].astype(BF16)

        def fetch(g):
            @pl.when(g < n_used)
            def _():
                in_copy(g, lax.rem(g, nbuf)).start()

        def run(blocks):
            slots = [lax.rem(g, nbuf) for g in blocks]
            for g, slot in zip(blocks, slots):
                in_copy(g, slot).wait()
            fetch(blocks[0] + nbuf - 1)
            for g, slot in zip(blocks, slots):
                @pl.when(g >= nbuf)
                def _():
                    out_copy(g - nbuf, slot).wait()
            xs_rows = [_unpack_bf16_pairs(xbuf[slot]).astype(BF16) for slot in slots]
            xb = xs_rows[0] if len(slots) == 1 else jnp.concatenate(xs_rows, axis=0)
            gu = jnp.dot(xb, wgu_b[...], preferred_element_type=F32)
            act = (jax.nn.silu(gu[:, 0:D_EXPERT]) * gu[:, D_EXPERT:2 * D_EXPERT]).astype(BF16)
            out = _pack_bf16_pairs(jnp.dot(act, wd_b[...], preferred_element_type=F32))
            for n, slot in enumerate(slots):
                obuf[slot] = out[n * ROW_TILE:(n + 1) * ROW_TILE]
            for g, slot in zip(blocks, slots):
                out_copy(g, slot).start()
            for g in blocks[:-1]:
                fetch(g + nbuf)

        group = EXPERT_GROUP

        def full_group(jj, carry):
            g = b0 + group * jj
            run([g + n for n in range(group)])
            return carry

        n_full = (b1 - b0) // group
        lax.fori_loop(0, n_full, full_group, 0)
        rest = (b1 - b0) - n_full * group
        tail = b0 + n_full * group
        for size in range(1, group):
            @pl.when(rest == size)
            def _():
                run([tail + n for n in range(size)])

    @pl.when(e == N_EXPERTS - 1)
    def _():
        for back in range(1, nbuf + 1):
            @pl.when(n_used >= back)
            def _():
                g = n_used - back
                out_copy(g, lax.rem(g, nbuf)).wait()


def _experts(xs, bstart, w_gate, w_up, w_down):
    n_rows, half = xs.shape
    d = 2 * half
    weights = lambda shape: pl.BlockSpec((None,) + shape, lambda e, bs: (e, 0, 0))
    return pl.pallas_call(
        _experts_kernel,
        grid_spec=pltpu.PrefetchScalarGridSpec(
            num_scalar_prefetch=1,
            grid=(N_EXPERTS,),
            in_specs=[pl.BlockSpec(memory_space=pl.ANY),
                      weights((d, D_EXPERT)), weights((d, D_EXPERT)), weights((D_EXPERT, d))],
            out_specs=pl.BlockSpec(memory_space=pl.ANY),
            scratch_shapes=[pltpu.VMEM((d, 2 * D_EXPERT), BF16), pltpu.VMEM((D_EXPERT, d), BF16),
                            pltpu.VMEM((EXPERT_BUFS, ROW_TILE, half), I32),
                            pltpu.VMEM((EXPERT_BUFS, ROW_TILE, half), I32),
                            pltpu.SemaphoreType.DMA((EXPERT_BUFS,)), pltpu.SemaphoreType.DMA((EXPERT_BUFS,))],
        ),
        out_shape=jax.ShapeDtypeStruct((n_rows, half), I32),
        compiler_params=_cparams(("arbitrary",)),
        name="experts",
    )(bstart, xs, w_gate, w_up, w_down)


def _final_kernel(x1_ref, hp_ref, ou_ref, wts_ref, ada_ref, wgs_ref, wus_ref, wds_ref, gf_ref, o_ref):
    nchunk = ou_ref.shape[0]
    gt2 = ada_ref[0][5:6]
    hb = _unpack_bf16_pairs(hp_ref[...]).astype(BF16)
    g = jnp.dot(hb, wgs_ref[...], preferred_element_type=F32)
    u = jnp.dot(hb, wus_ref[...], preferred_element_type=F32)
    y = jnp.dot((jax.nn.silu(g) * u).astype(BF16), wds_ref[...], preferred_element_type=F32)
    wts = wts_ref[...]
    parts = []
    for c in range(nchunk):
        rs = slice(c * SC_CHUNK, (c + 1) * SC_CHUNK)
        yr = wts[rs, 0:1] * _unpack_bf16_pairs(ou_ref[c, 0])
        for kk in range(1, TOP_K):
            yr = yr + wts[rs, kk:kk + 1] * _unpack_bf16_pairs(ou_ref[c, kk])
        parts.append(yr)
    y = y + (parts[0] if nchunk == 1 else jnp.concatenate(parts, axis=0))
    x2 = x1_ref[...] + gt2 * y
    o_ref[...] = _rms(x2, NORM_EPS) * gf_ref[...]


def _final(x1, hp, outu4, wts, ada3, wgs_b, wus_b, wds_b, g_final, seq):
    t, d = x1.shape
    tm = min(256, seq)
    nchunk = tm // SC_CHUNK
    tiles_per_seq = seq // tm
    full = lambda shape: pl.BlockSpec(shape, lambda i: (0,) * len(shape))
    row = lambda w: pl.BlockSpec((tm, w), lambda i: (i, 0))
    return pl.pallas_call(
        _final_kernel,
        grid=(t // tm,),
        in_specs=[row(d), row(d // 2),
                  pl.BlockSpec((nchunk, TOP_K, SC_CHUNK, d // 2), lambda i: (i, 0, 0, 0)),
                  row(TOP_K),
                  pl.BlockSpec((1, 6, d), lambda i: (i // tiles_per_seq, 0, 0)),
                  full((d, D_EXPERT)), full((d, D_EXPERT)), full((D_EXPERT, d)), full((1, d))],
        out_specs=row(d),
        out_shape=jax.ShapeDtypeStruct((t, d), F32),
        compiler_params=_cparams(("arbitrary",)),
        name="final",
    )(x1, hp, outu4, wts, ada3, wgs_b, wus_b, wds_b, g_final)


def kernel(x, c, positions, w_ada, b_ada, g_norm1, w_in, conv_w, g_conv_out, lam_q1, lam_k1, lam_q2, lam_k2, g_subln, w_out, g_norm2, w_router, router_bias, w_gate_e, w_up_e, w_down_e, w_gate_s, w_up_s, w_down_s, g_final):
    bsz, seq, d = x.shape
    t = bsz * seq
    xf = x.reshape(t, d)
    pos = positions.reshape(1, t)
    invf = (ROPE_THETA ** (-jnp.arange(0, ROT_DIM, 2, dtype=F32) / ROT_DIM)).reshape(ROT_DIM // 2, 1)

    ada3 = _ada(c, w_ada[0], b_ada[0]).reshape(bsz, 6, d)
    yconv, qt, k, vt = _inproj(xf, pos, ada3, g_norm1[0].reshape(1, d), w_in[0].astype(BF16), conv_w[0],
                             g_conv_out[0].reshape(1, D_CONV), invf, seq)
    lamp = jnp.stack([lam_q1[0], lam_k1[0], lam_q2[0], lam_k2[0]]).astype(F32)
    yattn = _attention(qt, k, vt, lamp, g_subln[0].reshape(V_DIM, 1), bsz, seq)

    x1, hp, idx_t, wts_t, rank_t, counts_f = _outproj_router(
        yconv, yattn, xf, ada3, g_norm2[0].reshape(1, d), w_out[0].astype(BF16),
        w_router[0].T.astype(BF16), router_bias[0].reshape(N_EXPERTS, 1), seq)

    counts = counts_f[:, 0].astype(I32)
    padded = ((counts + ROW_TILE - 1) // ROW_TILE) * ROW_TILE
    pad_end = jnp.cumsum(padded)
    pad_start = pad_end - padded
    nb = (t * TOP_K + N_EXPERTS * (ROW_TILE - 1)) // ROW_TILE
    bstart = (jnp.concatenate([pad_start, pad_end[-1:]]) // ROW_TILE).astype(I32)

    dest_t = _dest(idx_t, rank_t, pad_start.astype(F32).reshape(N_EXPERTS, 1))
    n_chunks = t // SC_CHUNK
    dest_ck = dest_t.reshape(TOP_K, n_chunks, SC_CHUNK).transpose(1, 0, 2)

    xs = _sc_dispatch(hp, dest_ck, nb * ROW_TILE)
    outs = _experts(xs, bstart, w_gate_e[0], w_up_e[0], w_down_e[0])
    outu = _sc_gather(outs, dest_ck.reshape(n_chunks * TOP_K, 1, SC_CHUNK))

    out = _final(x1, hp, outu.reshape(n_chunks, TOP_K, SC_CHUNK, d // 2), wts_t.T, ada3,
                 w_gate_s[0].astype(BF16), w_up_s[0].astype(BF16), w_down_s[0].astype(BF16),
                 g_final.reshape(1, d), seq)
    return out.reshape(bsz, seq, d)
```

```python
import functools
import math

import jax
import jax.numpy as jnp
from jax import lax
from jax.experimental import pallas as pl
from jax.experimental.pallas import tpu as pltpu
from jax.experimental.pallas import tpu_sc as plsc

F32 = jnp.float32
BF16 = jnp.bfloat16
I32 = jnp.int32

D_CONV = 512
CONV_WIDTH = 3
N_HEADS = 4
HEAD_DIM = 64
V_DIM = 2 * HEAD_DIM
D_ATTN = N_HEADS * V_DIM
D_QK = N_HEADS * 2 * HEAD_DIM
ROT_DIM = HEAD_DIM // 4
ROPE_THETA = 500000.0
N_EXPERTS = 256
TOP_K = 8
N_GROUPS = 8
GROUP_SIZE = N_EXPERTS // N_GROUPS
TOPK_GROUPS = 4
D_EXPERT = 256
ROUTED_SCALE = 2.5
NORM_EPS = 1e-6
SUBLN_EPS = 1e-5
LAMBDA_INIT = 0.8 - 0.6 * math.exp(-0.3 * 0)

LANES = 128
SC_CORES = 2
SC_SUBCORES = 16
SC_WORKERS = SC_CORES * SC_SUBCORES
SC_CHUNK = 128
COMBINE_SLOTS = 64
SC_LANES = 16

ROW_TILE = 128
EXPERT_GROUP = 4
EXPERT_BUFS = 12
NEG_BIG = -1e30
VMEM_LIMIT = 56 * 1024 * 1024


def _cparams(sem):
    return pltpu.CompilerParams(dimension_semantics=sem, vmem_limit_bytes=VMEM_LIMIT)


def _rms(x, eps):
    return x * lax.rsqrt(jnp.mean(x * x, axis=-1, keepdims=True) + eps)


def _pack_bf16_pairs(x):
    n = x.shape[1] // 2
    bits = lax.bitcast_convert_type(x.astype(BF16).astype(F32), I32)
    lo = lax.shift_right_logical(bits[:, :n], 16)
    return lo | bits[:, n:]


def _unpack_bf16_pairs(p):
    lo = lax.bitcast_convert_type(lax.shift_left(p, 16), F32)
    hi = lax.bitcast_convert_type(p & jnp.int32(-65536), F32)
    return jnp.concatenate([lo, hi], axis=1)


def _ada_kernel(c_ref, w_ref, b_ref, o_ref):
    ca = jax.nn.silu(c_ref[...])
    o_ref[...] = jnp.dot(ca.astype(BF16), w_ref[...].astype(BF16), preferred_element_type=F32) + b_ref[...]


def _ada(c, w_ada, b_ada):
    bsz, d = c.shape
    n = w_ada.shape[1]
    tn = n // 4
    return pl.pallas_call(
        _ada_kernel,
        grid=(n // tn,),
        in_specs=[pl.BlockSpec((bsz, d), lambda j: (0, 0)),
                  pl.BlockSpec((d, tn), lambda j: (0, j)),
                  pl.BlockSpec((1, tn), lambda j: (0, j))],
        out_specs=pl.BlockSpec((bsz, tn), lambda j: (0, j)),
        out_shape=jax.ShapeDtypeStruct((bsz, n), F32),
        compiler_params=_cparams(("arbitrary",)),
        name="ada",
    )(c, w_ada, b_ada.reshape(1, n))


def _inproj_kernel(tiles_per_seq, x_ref, pos_ref, ada_ref, g1_ref, win_ref, convw_ref, gconv_ref, invf_ref,
                   yconv_ref, qt_ref, k_ref, vt_ref, ubuf):
    i = pl.program_id(0)
    tm = x_ref.shape[0]
    x = x_ref[...]
    ada = ada_ref[0]
    sh1, sc1 = ada[0:1], ada[1:2]
    hn = _rms(x, NORM_EPS) * g1_ref[...] * (1.0 + sc1) + sh1
    hb = hn.astype(BF16)

    pc = jnp.dot(hb, win_ref[:, 0:3 * D_CONV], preferred_element_type=F32)
    u = pc[:, 2 * D_CONV:3 * D_CONV] * pc[:, 0:D_CONV]
    first = (i % tiles_per_seq) == 0

    @pl.when(first)
    def _():
        ubuf[0:8, :] = jnp.zeros((8, D_CONV), F32)

    @pl.when(jnp.logical_not(first))
    def _():
        ubuf[0:8, :] = ubuf[tm:tm + 8, :]

    ubuf[8:8 + tm, :] = u
    u1 = ubuf[7:7 + tm, :]
    u2 = ubuf[6:6 + tm, :]
    cw = convw_ref[...]
    yc = pc[:, D_CONV:2 * D_CONV] * (cw[0:1] * u2 + cw[1:2] * u1 + cw[2:3] * u)
    yconv_ref[...] = (_rms(yc, NORM_EPS) * gconv_ref[...]).astype(BF16)

    half = ROT_DIM // 2
    ang = invf_ref[...] * pos_ref[...].astype(F32)
    cos_ft = jnp.cos(ang)
    sin_ft = jnp.sin(ang)
    zero_h = jnp.zeros((half, tm), F32)
    zero_r = jnp.zeros((HEAD_DIM - ROT_DIM, tm), F32)
    lay = lambda a, b, r: jnp.concatenate([a, b, r, a, b, r], axis=0).T
    cos = lay(cos_ft, cos_ft, zero_r + 1.0)
    s_up = lay(-sin_ft, zero_h, zero_r)
    s_dn = lay(zero_h, sin_ft, zero_r)

    def rope(t):
        return t * cos + pltpu.roll(t, LANES - half, axis=1) * s_up + pltpu.roll(t, half, axis=1) * s_dn

    pq = jnp.dot(hb, win_ref[:, 3 * D_CONV:3 * D_CONV + D_QK], preferred_element_type=F32)
    pk = jnp.dot(hb, win_ref[:, 3 * D_CONV + D_QK:3 * D_CONV + 2 * D_QK], preferred_element_type=F32)
    scale = HEAD_DIM ** -0.5 * math.log2(math.e)
    qs = []
    for h in range(N_HEADS):
        sl = slice(h * LANES, (h + 1) * LANES)
        qs.append(rope(pq[:, sl]) * scale)
        k_ref[:, sl] = rope(pk[:, sl]).astype(BF16)
    qt_ref[0] = jnp.concatenate(qs, axis=1).T.astype(BF16)
    pv = jnp.dot(hb, win_ref[:, 3 * D_CONV + 2 * D_QK:], preferred_element_type=F32)
    vt_ref[0] = pv.T.astype(BF16)


def _inproj(xf, pos, ada3, g1, win_b, conv_w, g_conv, invf, seq):
    t, d = xf.shape
    tm = min(512, seq)
    tiles_per_seq = seq // tm
    n_in = win_b.shape[1]
    full = lambda shape: pl.BlockSpec(shape, lambda i: (0,) * len(shape))
    row = lambda w: pl.BlockSpec((tm, w), lambda i: (i, 0))
    colmajor = lambda w: pl.BlockSpec((1, w, tm), lambda i: (i // tiles_per_seq, 0, i % tiles_per_seq))
    return pl.pallas_call(
        functools.partial(_inproj_kernel, tiles_per_seq),
        grid=(t // tm,),
        in_specs=[row(d), pl.BlockSpec((1, tm), lambda i: (0, i)),
                  pl.BlockSpec((1, 6, d), lambda i: (i // tiles_per_seq, 0, 0)),
                  full((1, d)), full((d, n_in)), full((CONV_WIDTH, D_CONV)), full((1, D_CONV)),
                  full((ROT_DIM // 2, 1))],
        out_specs=[row(D_CONV), colmajor(D_QK), row(D_QK), colmajor(D_ATTN)],
        out_shape=[jax.ShapeDtypeStruct((t, D_CONV), BF16), jax.ShapeDtypeStruct((t // seq, D_QK, seq), BF16),
                   jax.ShapeDtypeStruct((t, D_QK), BF16), jax.ShapeDtypeStruct((t // seq, D_ATTN, seq), BF16)],
        scratch_shapes=[pltpu.VMEM((tm + 8, D_CONV), F32)],
        compiler_params=_cparams(("arbitrary",)),
        name="inproj",
    )(xf, pos, ada3, g1, win_b, conv_w, g_conv, invf)


def _attn_steps(nq):
    off = [(j, i) for j in range(nq) for i in range(j + 1, nq)]
    diag = [(i, i) for i in range(nq)]
    return off, diag


def _attn_kernel(steps_ref, qt_ref, k_ref, vt_ref, lamp_ref, gs_ref, o_ref,
                 qq_ref, m_ref, l_ref, acc_ref, s0_ref, s1_ref):
    nq = m_ref.shape[0]
    tq = s0_ref.shape[0]
    tk = tq
    off, diag = _attn_steps(nq)
    n_steps = len(off) + len(diag)
    bufs = (s0_ref, s1_ref)

    feat = lax.broadcasted_iota(I32, (LANES, tq), 0)
    for i in range(nq):
        qt = qt_ref[0, :, i * tq:(i + 1) * tq]
        zero = jnp.zeros_like(qt)
        qq_ref[:, 2 * i * tq:(2 * i + 1) * tq] = jnp.where(feat < HEAD_DIM, qt, zero)
        qq_ref[:, (2 * i + 1) * tq:(2 * i + 2) * tq] = jnp.where(feat >= HEAD_DIM, qt, zero)
    m_ref[...] = jnp.full(m_ref.shape, NEG_BIG, F32)
    l_ref[...] = jnp.zeros(l_ref.shape, F32)
    acc_ref[...] = jnp.zeros(acc_ref.shape, F32)

    def scores(j, i, dst):
        kstart = pl.multiple_of(j * tk, tk)
        qstart = pl.multiple_of(i * (2 * tq), 2 * tq)
        dst[...] = jnp.dot(k_ref[pl.ds(kstart, tk), :], qq_ref[:, pl.ds(qstart, 2 * tq)],
                           preferred_element_type=F32)

    def consume(src, j, i, masked):
        kstart = pl.multiple_of(j * tk, tk)
        vtb = vt_ref[0, :, pl.ds(kstart, tk)]
        s = src[...]
        if masked:
            key = lax.broadcasted_iota(I32, s.shape, 0)
            col = lax.broadcasted_iota(I32, s.shape, 1)
            qpos = jnp.where(col >= tq, col - tq, col)
            s = jnp.where(key <= qpos, s, NEG_BIG)
        m_prev = m_ref[i]
        m_new = jnp.maximum(m_prev, jnp.max(s, axis=0, keepdims=True))
        alpha = jnp.exp2(m_prev - m_new)
        p = jnp.exp2(s - m_new)
        l_ref[i] = alpha * l_ref[i] + jnp.sum(p, axis=0, keepdims=True)
        acc_ref[i] = alpha * acc_ref[i] + jnp.dot(vtb, p.astype(BF16), preferred_element_type=F32)
        m_ref[i] = m_new

    def at(n):
        return steps_ref[0, n], steps_ref[1, n]

    scores(*at(0), s0_ref)
    n_pairs = len(off) // 2

    def pair(pp, carry):
        n = 2 * pp
        scores(*at(n + 1), s1_ref)
        consume(s0_ref, *at(n), False)
        scores(*at(n + 2), s0_ref)
        consume(s1_ref, *at(n + 1), False)
        return carry

    lax.fori_loop(0, n_pairs, pair, 0)

    static_steps = (off + diag)[2 * n_pairs:]
    for n, (j, i) in enumerate(static_steps, start=2 * n_pairs):
        if n + 1 < n_steps:
            scores(*static_steps[n + 1 - 2 * n_pairs], bufs[(n + 1) % 2])
        consume(bufs[n % 2], j, i, masked=(j == i))

    lamp = lamp_ref[...]
    lam = (jnp.exp(jnp.sum(lamp[0:1] * lamp[1:2], axis=1, keepdims=True))
           - jnp.exp(jnp.sum(lamp[2:3] * lamp[3:4], axis=1, keepdims=True)) + LAMBDA_INIT)
    for i in range(nq):
        o = acc_ref[i] / l_ref[i]
        od = o[:, 0:tq] - lam * o[:, tq:2 * tq]
        y = od * lax.rsqrt(jnp.mean(od * od, axis=0, keepdims=True) + SUBLN_EPS) * gs_ref[...] * (1.0 - LAMBDA_INIT)
        o_ref[i * tq:(i + 1) * tq, :] = y.T.astype(BF16)


def _attention(qt, k, vt, lamp, g_subln, bsz, seq):
    t = k.shape[0]
    tq = min(512, seq)
    nq = seq // tq
    off, diag = _attn_steps(nq)
    steps = jnp.asarray(list(zip(*(off + diag))), I32)
    return pl.pallas_call(
        _attn_kernel,
        grid_spec=pltpu.PrefetchScalarGridSpec(
            num_scalar_prefetch=1,
            grid=(bsz, N_HEADS),
            in_specs=[pl.BlockSpec((1, LANES, seq), lambda b, h, st: (b, h, 0)),
                      pl.BlockSpec((seq, LANES), lambda b, h, st: (b, h)),
                      pl.BlockSpec((1, V_DIM, seq), lambda b, h, st: (b, h, 0)),
                      pl.BlockSpec((4, HEAD_DIM), lambda b, h, st: (0, 0)),
                      pl.BlockSpec((V_DIM, 1), lambda b, h, st: (0, 0))],
            out_specs=pl.BlockSpec((seq, LANES), lambda b, h, st: (b, h)),
            scratch_shapes=[pltpu.VMEM((LANES, 2 * seq), BF16), pltpu.VMEM((nq, 1, 2 * tq), F32),
                            pltpu.VMEM((nq, 1, 2 * tq), F32), pltpu.VMEM((nq, V_DIM, 2 * tq), F32),
                            pltpu.VMEM((tq, 2 * tq), F32), pltpu.VMEM((tq, 2 * tq), F32)],
        ),
        out_shape=jax.ShapeDtypeStruct((t, D_ATTN), BF16),
        compiler_params=_cparams(("arbitrary", "arbitrary")),
        name="attention",
    )(steps, qt, k, vt, lamp, g_subln)


def _first_index(hit, idx, size, axis):
    return jnp.min(jnp.where(hit, idx, size), axis=axis, keepdims=True)


def _outproj_router_kernel(yc_ref, ya_ref, x_ref, ada_ref, g2_ref, wout_ref, wrt_ref, rb_ref,
                           x1_ref, hp_ref, idx_ref, wts_ref, rank_ref, cnt_ref, base_ref):
    i = pl.program_id(0)
    tm = x_ref.shape[0]
    ada = ada_ref[0]
    gt1, sh2, sc2 = ada[2:3], ada[3:4], ada[4:5]
    mix = (jnp.dot(yc_ref[...], wout_ref[0:D_CONV, :], preferred_element_type=F32)
           + jnp.dot(ya_ref[...], wout_ref[D_CONV:, :], preferred_element_type=F32))
    x1 = x_ref[...] + gt1 * mix
    x1_ref[...] = x1
    hn2 = _rms(x1, NORM_EPS) * g2_ref[...] * (1.0 + sc2) + sh2
    hp_ref[...] = _pack_bf16_pairs(hn2)
    hb = hn2.astype(BF16)

    logits = lax.dot_general(wrt_ref[...], hb, (((1,), (1,)), ((), ())), preferred_element_type=F32)
    scores = jax.nn.sigmoid(logits)
    choice = scores + rb_ref[...]
    neg_inf = jnp.float32(-jnp.inf)

    ch3 = choice.reshape(N_GROUPS, GROUP_SIZE, tm)
    i3 = lax.broadcasted_iota(I32, ch3.shape, 1)
    m1 = jnp.max(ch3, axis=1, keepdims=True)
    f1 = _first_index(ch3 == m1, i3, GROUP_SIZE, 1)
    m2 = jnp.max(jnp.where(i3 == f1, neg_inf, ch3), axis=1, keepdims=True)
    gs = (m1 + m2).reshape(N_GROUPS, tm)

    gi = lax.broadcasted_iota(I32, gs.shape, 0)
    gkeep = jnp.zeros(gs.shape, F32)
    for _ in range(TOPK_GROUPS):
        m = jnp.max(gs, axis=0, keepdims=True)
        f = _first_index(gs == m, gi, N_GROUPS, 0)
        sel = gi == f
        gkeep = jnp.where(sel, 1.0, gkeep)
        gs = jnp.where(sel, neg_inf, gs)
    ekeep = jnp.broadcast_to(gkeep.reshape(N_GROUPS, 1, tm), (N_GROUPS, GROUP_SIZE, tm)).reshape(N_EXPERTS, tm)
    masked = jnp.where(ekeep > 0.0, choice, neg_inf)

    ei = lax.broadcasted_iota(I32, masked.shape, 0)
    picked = jnp.zeros(masked.shape, F32)
    idxs, ws = [], []
    for _ in range(TOP_K):
        m = jnp.max(masked, axis=0, keepdims=True)
        f = _first_index(masked == m, ei, N_EXPERTS, 0)
        sel = ei == f
        idxs.append(f)
        ws.append(jnp.sum(jnp.where(sel, scores, 0.0), axis=0, keepdims=True))
        picked = jnp.where(sel, 1.0, picked)
        masked = jnp.where(sel, neg_inf, masked)
    wsum = ws[0]
    for wk in ws[1:]:
        wsum = wsum + wk
    denom = wsum + 1e-20
    for kk in range(TOP_K):
        idx_ref[kk:kk + 1, :] = idxs[kk]
        wts_ref[kk:kk + 1, :] = ws[kk] / denom * ROUTED_SCALE

    @pl.when(i == 0)
    def _():
        base_ref[...] = jnp.zeros(base_ref.shape, F32)

    si = lax.broadcasted_iota(I32, (tm, tm), 0)
    ti = lax.broadcasted_iota(I32, (tm, tm), 1)
    earlier = jnp.where(si < ti, 1.0, 0.0).astype(BF16)
    before = jnp.dot(picked.astype(BF16), earlier, preferred_element_type=F32) + base_ref[...]
    for kk in range(TOP_K):
        rank_ref[kk:kk + 1, :] = jnp.sum(jnp.where(ei == idxs[kk], before, 0.0), axis=0, keepdims=True).astype(I32)
    base_ref[...] = base_ref[...] + jnp.sum(picked, axis=1, keepdims=True)
    cnt_ref[...] = base_ref[...]


def _outproj_router(yconv, yattn, xf, ada3, g2, wout_b, wrt_b, rbias, seq):
    t, d = xf.shape
    tm = min(512, seq)
    tiles_per_seq = seq // tm
    full = lambda shape: pl.BlockSpec(shape, lambda i: (0,) * len(shape))
    row = lambda w: pl.BlockSpec((tm, w), lambda i: (i, 0))
    col = pl.BlockSpec((TOP_K, tm), lambda i: (0, i))
    return pl.pallas_call(
        _outproj_router_kernel,
        grid=(t // tm,),
        in_specs=[row(D_CONV), row(D_ATTN), row(d),
                  pl.BlockSpec((1, 6, d), lambda i: (i // tiles_per_seq, 0, 0)),
                  full((1, d)), full((D_CONV + D_ATTN, d)), full((N_EXPERTS, d)), full((N_EXPERTS, 1))],
        out_specs=[row(d), row(d // 2), col, col, col, full((N_EXPERTS, 1))],
        out_shape=[jax.ShapeDtypeStruct((t, d), F32), jax.ShapeDtypeStruct((t, d // 2), I32),
                   jax.ShapeDtypeStruct((TOP_K, t), I32), jax.ShapeDtypeStruct((TOP_K, t), F32),
                   jax.ShapeDtypeStruct((TOP_K, t), I32), jax.ShapeDtypeStruct((N_EXPERTS, 1), F32)],
        scratch_shapes=[pltpu.VMEM((N_EXPERTS, 1), F32)],
        compiler_params=_cparams(("arbitrary",)),
        name="outproj_router",
    )(yconv, yattn, xf, ada3, g2, wout_b, wrt_b, rbias)


def _dest_kernel(idx_ref, rank_ref, ps_ref, dest_ref):
    tm = idx_ref.shape[1]
    ei = lax.broadcasted_iota(I32, (N_EXPERTS, tm), 0)
    ps = ps_ref[...]
    for kk in range(TOP_K):
        start = jnp.sum(jnp.where(ei == idx_ref[kk:kk + 1, :], ps, 0.0), axis=0, keepdims=True)
        dest_ref[kk:kk + 1, :] = start.astype(I32) + rank_ref[kk:kk + 1, :]


def _dest(idx_t, rank_t, pad_start_f):
    t = idx_t.shape[1]
    tm = min(2048, t)
    col = pl.BlockSpec((TOP_K, tm), lambda i: (0, i))
    return pl.pallas_call(
        _dest_kernel,
        grid=(t // tm,),
        in_specs=[col, col, pl.BlockSpec((N_EXPERTS, 1), lambda i: (0, 0))],
        out_specs=col,
        out_shape=jax.ShapeDtypeStruct((TOP_K, t), I32),
        compiler_params=_cparams(("arbitrary",)),
        name="dest",
    )(idx_t, rank_t, pad_start_f)


def _sc_dispatch(hp, dest_ck, n_rows):
    t, w = hp.shape
    nk = dest_ck.shape[1]
    per_worker = t // SC_CHUNK // SC_WORKERS
    mesh = plsc.VectorSubcoreMesh(core_axis_name="c", subcore_axis_name="s")

    @functools.partial(
        pl.kernel, mesh=mesh,
        out_type=jax.ShapeDtypeStruct((n_rows, w), hp.dtype),
        scratch_types=[pltpu.VMEM((nk, SC_CHUNK), I32), pltpu.VMEM((SC_CHUNK, w), hp.dtype), pltpu.SemaphoreType.DMA],
    )
    def k(hp_hbm, dest_hbm, xs_hbm, idx_v, rows_v, sem):
        wid = lax.axis_index("s") * SC_CORES + lax.axis_index("c")

        @pl.loop(0, per_worker)
        def _(j):
            c = wid * per_worker + j
            pltpu.sync_copy(dest_hbm.at[c], idx_v)
            pltpu.sync_copy(hp_hbm.at[pl.ds(c * SC_CHUNK, SC_CHUNK)], rows_v)
            for kk in range(nk):
                pltpu.async_copy(rows_v, xs_hbm.at[idx_v.at[kk]], sem)
            for kk in range(nk):
                pltpu.make_async_copy(rows_v, xs_hbm.at[idx_v.at[kk]], sem).wait()

    return k(hp, dest_ck)


def _sc_combine(src, idx2, wrep, n_tokens):
    half = src.shape[1]
    lanes = wrep.shape[1] // TOP_K
    ctok = COMBINE_SLOTS // TOP_K
    per_worker = n_tokens // ctok // SC_WORKERS
    mesh = plsc.VectorSubcoreMesh(core_axis_name="c", subcore_axis_name="s")

    @functools.partial(
        pl.kernel, mesh=mesh,
        out_type=jax.ShapeDtypeStruct((n_tokens, 2 * half), F32),
        scratch_types=[pltpu.VMEM((per_worker, COMBINE_SLOTS), I32),
                       pltpu.VMEM((2, COMBINE_SLOTS, half), I32), pltpu.VMEM((2, ctok, TOP_K * lanes), F32),
                       pltpu.VMEM((2, ctok, 2 * half), F32),
                       pltpu.SemaphoreType.DMA((2,)), pltpu.SemaphoreType.DMA((2,)), pltpu.SemaphoreType.DMA((2,))],
        compiler_params=pltpu.CompilerParams(needs_layout_passes=False),
    )
    def k(src_hbm, idx_hbm, w_hbm, y_hbm, idx_v, rows_v, w_v, out_v, sem_g, sem_w, sem_o):
        wid = lax.axis_index("s") * SC_CORES + lax.axis_index("c")
        base = wid * per_worker
        pltpu.sync_copy(idx_hbm.at[pl.ds(base, per_worker)], idx_v)

        def gather(c, slot):
            return pltpu.make_async_copy(src_hbm.at[idx_v.at[c]], rows_v.at[slot], sem_g.at[slot])

        def wload(c, slot):
            return pltpu.make_async_copy(w_hbm.at[pl.ds((base + c) * ctok, ctok)], w_v.at[slot], sem_w.at[slot])

        def store(c, slot):
            return pltpu.make_async_copy(out_v.at[slot], y_hbm.at[pl.ds((base + c) * ctok, ctok)], sem_o.at[slot])

        gather(0, 0).start()
        wload(0, 0).start()

        @pl.loop(0, per_worker, step=2)
        def _(c0):
            for slot in range(2):
                c = c0 + slot
                gather(c, slot).wait()
                wload(c, slot).wait()

                @pl.when(c + 1 < per_worker)
                def _():
                    gather(c + 1, 1 - slot).start()
                    wload(c + 1, 1 - slot).start()

                @pl.when(c >= 2)
                def _():
                    store(c - 2, slot).wait()

                @pl.loop(0, ctok)
                def _(tt):
                    ws = [w_v[slot, tt, pl.ds(kk * lanes, lanes)] for kk in range(TOP_K)]

                    @plsc.parallel_loop(0, half // lanes, unroll=2)
                    def _(v):
                        lo = jnp.zeros((lanes,), F32)
                        hi = jnp.zeros((lanes,), F32)
                        for kk in range(TOP_K):
                            x = rows_v[slot, tt * TOP_K + kk, pl.ds(v * lanes, lanes)]
                            x_lo, x_hi = plsc.unpack(plsc.bitcast(x, BF16), format=plsc.PackFormat.INTERLEAVED)
                            lo = lo + x_lo * ws[kk]
                            hi = hi + x_hi * ws[kk]
                        out_v[slot, tt, pl.ds(v * lanes, lanes)] = lo
                        out_v[slot, tt, pl.ds(half + v * lanes, lanes)] = hi

                store(c, slot).start()

        store(per_worker - 2, 0).wait()
        store(per_worker - 1, 1).wait()

    return k(src, idx2, wrep)


def _experts_kernel(bstart_ref, xs_hbm, wg_ref, wu_ref, wd_ref, out_hbm,
                    wgu_b, wd_b, xbuf, obuf, sem_in, sem_out):
    e = pl.program_id(0)
    b0 = bstart_ref[e]
    b1 = bstart_ref[e + 1]
    n_used = bstart_ref[N_EXPERTS]
    nbuf = EXPERT_BUFS

    def in_copy(g, slot):
        return pltpu.make_async_copy(xs_hbm.at[pl.ds(g * ROW_TILE, ROW_TILE)], xbuf.at[slot], sem_in.at[slot])

    def out_copy(g, slot):
        return pltpu.make_async_copy(obuf.at[slot], out_hbm.at[pl.ds(g * ROW_TILE, ROW_TILE)], sem_out.at[slot])

    @pl.when(e == 0)
    def _():
        for g in range(nbuf - 1):
            @pl.when(g < n_used)
            def _():
                in_copy(g, g).start()

    @pl.when(b1 > b0)
    def _():
        wgu_b[:, 0:D_EXPERT] = wg_ref[...].astype(BF16)
        wgu_b[:, D_EXPERT:2 * D_EXPERT] = wu_ref[...].astype(BF16)
        wd_b[...] = wd_ref[...].astype(BF16)

        def fetch(g):
            @pl.when(g < n_used)
            def _():
                in_copy(g, lax.rem(g, nbuf)).start()

        def run(blocks):
            slots = [lax.rem(g, nbuf) for g in blocks]
            for g, slot in zip(blocks, slots):
                in_copy(g, slot).wait()
            fetch(blocks[0] + nbuf - 1)
            for g, slot in zip(blocks, slots):
                @pl.when(g >= nbuf)
                def _():
                    out_copy(g - nbuf, slot).wait()
            xs_rows = [_unpack_bf16_pairs(xbuf[slot]).astype(BF16) for slot in slots]
            xb = xs_rows[0] if len(slots) == 1 else jnp.concatenate(xs_rows, axis=0)
            gu = jnp.dot(xb, wgu_b[...], preferred_element_type=F32)
            act = (jax.nn.silu(gu[:, 0:D_EXPERT]) * gu[:, D_EXPERT:2 * D_EXPERT]).astype(BF16)
            out = _pack_bf16_pairs(jnp.dot(act, wd_b[...], preferred_element_type=F32))
            for n, slot in enumerate(slots):
                obuf[slot] = out[n * ROW_TILE:(n + 1) * ROW_TILE]
            for g, slot in zip(blocks, slots):
                out_copy(g, slot).start()
            for g in blocks[:-1]:
                fetch(g + nbuf)

        group = EXPERT_GROUP

        def full_group(jj, carry):
            g = b0 + group * jj
            run([g + n for n in range(group)])
            return carry

        n_full = (b1 - b0) // group
        lax.fori_loop(0, n_full, full_group, 0)
        rest = (b1 - b0) - n_full * group
        tail = b0 + n_full * group
        for size in range(1, group):
            @pl.when(rest == size)
            def _():
                run([tail + n for n in range(size)])

    @pl.when(e == N_EXPERTS - 1)
    def _():
        for back in range(1, nbuf + 1):
            @pl.when(n_used >= back)
            def _():
                g = n_used - back
                out_copy(g, lax.rem(g, nbuf)).wait()


def _experts(xs, bstart, w_gate, w_up, w_down):
    n_rows, half = xs.shape
    d = 2 * half
    weights = lambda shape: pl.BlockSpec((None,) + shape, lambda e, bs: (e, 0, 0))
    return pl.pallas_call(
        _experts_kernel,
        grid_spec=pltpu.PrefetchScalarGridSpec(
            num_scalar_prefetch=1,
            grid=(N_EXPERTS,),
            in_specs=[pl.BlockSpec(memory_space=pl.ANY),
                      weights((d, D_EXPERT)), weights((d, D_EXPERT)), weights((D_EXPERT, d))],
            out_specs=pl.BlockSpec(memory_space=pl.ANY),
            scratch_shapes=[pltpu.VMEM((d, 2 * D_EXPERT), BF16), pltpu.VMEM((D_EXPERT, d), BF16),
                            pltpu.VMEM((EXPERT_BUFS, ROW_TILE, half), I32),
                            pltpu.VMEM((EXPERT_BUFS, ROW_TILE, half), I32),
                            pltpu.SemaphoreType.DMA((EXPERT_BUFS,)), pltpu.SemaphoreType.DMA((EXPERT_BUFS,))],
        ),
        out_shape=jax.ShapeDtypeStruct((n_rows, half), I32),
        compiler_params=_cparams(("arbitrary",)),
        name="experts",
    )(bstart, xs, w_gate, w_up, w_down)


def _final_kernel(x1_ref, hp_ref, yr_ref, ada_ref, wgs_ref, wus_ref, wds_ref, gf_ref, o_ref):
    gt2 = ada_ref[0][5:6]
    hb = _unpack_bf16_pairs(hp_ref[...]).astype(BF16)
    g = jnp.dot(hb, wgs_ref[...], preferred_element_type=F32)
    u = jnp.dot(hb, wus_ref[...], preferred_element_type=F32)
    y = jnp.dot((jax.nn.silu(g) * u).astype(BF16), wds_ref[...], preferred_element_type=F32)
    x2 = x1_ref[...] + gt2 * (yr_ref[...] + y)
    o_ref[...] = _rms(x2, NORM_EPS) * gf_ref[...]


def _final(x1, hp, y_routed, ada3, wgs_b, wus_b, wds_b, g_final, seq):
    t, d = x1.shape
    tm = min(512, seq)
    tiles_per_seq = seq // tm
    full = lambda shape: pl.BlockSpec(shape, lambda i: (0,) * len(shape))
    row = lambda w: pl.BlockSpec((tm, w), lambda i: (i, 0))
    return pl.pallas_call(
        _final_kernel,
        grid=(t // tm,),
        in_specs=[row(d), row(d // 2), row(d),
                  pl.BlockSpec((1, 6, d), lambda i: (i // tiles_per_seq, 0, 0)),
                  full((d, D_EXPERT)), full((d, D_EXPERT)), full((D_EXPERT, d)), full((1, d))],
        out_specs=row(d),
        out_shape=jax.ShapeDtypeStruct((t, d), F32),
        compiler_params=_cparams(("arbitrary",)),
        name="final",
    )(x1, hp, y_routed, ada3, wgs_b, wus_b, wds_b, g_final)


def kernel(x, c, positions, w_ada, b_ada, g_norm1, w_in, conv_w, g_conv_out, lam_q1, lam_k1, lam_q2, lam_k2, g_subln, w_out, g_norm2, w_router, router_bias, w_gate_e, w_up_e, w_down_e, w_gate_s, w_up_s, w_down_s, g_final):
    bsz, seq, d = x.shape
    t = bsz * seq
    xf = x.reshape(t, d)
    pos = positions.reshape(1, t)
    invf = (ROPE_THETA ** (-jnp.arange(0, ROT_DIM, 2, dtype=F32) / ROT_DIM)).reshape(ROT_DIM // 2, 1)

    ada3 = _ada(c, w_ada[0], b_ada[0]).reshape(bsz, 6, d)
    yconv, qt, k, vt = _inproj(xf, pos, ada3, g_norm1[0].reshape(1, d), w_in[0].astype(BF16), conv_w[0],
                             g_conv_out[0].reshape(1, D_CONV), invf, seq)
    lamp = jnp.stack([lam_q1[0], lam_k1[0], lam_q2[0], lam_k2[0]]).astype(F32)
    yattn = _attention(qt, k, vt, lamp, g_subln[0].reshape(V_DIM, 1), bsz, seq)

    x1, hp, idx_t, wts_t, rank_t, counts_f = _outproj_router(
        yconv, yattn, xf, ada3, g_norm2[0].reshape(1, d), w_out[0].astype(BF16),
        w_router[0].T.astype(BF16), router_bias[0].reshape(N_EXPERTS, 1), seq)

    counts = counts_f[:, 0].astype(I32)
    padded = ((counts + ROW_TILE - 1) // ROW_TILE) * ROW_TILE
    pad_end = jnp.cumsum(padded)
    pad_start = pad_end - padded
    nb = (t * TOP_K + N_EXPERTS * (ROW_TILE - 1)) // ROW_TILE
    bstart = (jnp.concatenate([pad_start, pad_end[-1:]]) // ROW_TILE).astype(I32)

    dest_t = _dest(idx_t, rank_t, pad_start.astype(F32).reshape(N_EXPERTS, 1))
    n_chunks = t // SC_CHUNK
    dest_ck = dest_t.reshape(TOP_K, n_chunks, SC_CHUNK).transpose(1, 0, 2)

    xs = _sc_dispatch(hp, dest_ck, nb * ROW_TILE)
    outs = _experts(xs, bstart, w_gate_e[0], w_up_e[0], w_down_e[0])
    slot_rows = dest_t.T.reshape(t * TOP_K // COMBINE_SLOTS, COMBINE_SLOTS)
    wrep = jnp.repeat(wts_t.T, SC_LANES, axis=1)
    y_routed = _sc_combine(outs, slot_rows, wrep, t)

    out = _final(x1, hp, y_routed, ada3,
                 w_gate_s[0].astype(BF16), w_up_s[0].astype(BF16), w_down_s[0].astype(BF16),
                 g_final.reshape(1, d), seq)
    return out.reshape(bsz, seq, d)
```

```python
import functools
import math

import jax
import jax.numpy as jnp
from jax import lax
from jax.experimental import pallas as pl
from jax.experimental.pallas import tpu as pltpu
from jax.experimental.pallas import tpu_sc as plsc

F32 = jnp.float32
BF16 = jnp.bfloat16
I32 = jnp.int32

D_CONV = 512
CONV_WIDTH = 3
N_HEADS = 4
HEAD_DIM = 64
V_DIM = 2 * HEAD_DIM
D_ATTN = N_HEADS * V_DIM
D_QK = N_HEADS * 2 * HEAD_DIM
ROT_DIM = HEAD_DIM // 4
ROPE_THETA = 500000.0
N_EXPERTS = 256
TOP_K = 8
N_GROUPS = 8
GROUP_SIZE = N_EXPERTS // N_GROUPS
TOPK_GROUPS = 4
D_EXPERT = 256
ROUTED_SCALE = 2.5
NORM_EPS = 1e-6
SUBLN_EPS = 1e-5
LAMBDA_INIT = 0.8 - 0.6 * math.exp(-0.3 * 0)

LANES = 128
SC_CORES = 2
SC_SUBCORES = 16
SC_WORKERS = SC_CORES * SC_SUBCORES
SC_CHUNK = 128
COMBINE_SLOTS = 64
SC_LANES = 16

ROW_TILE = 128
EXPERT_GROUP = 4
EXPERT_BUFS = 12
NEG_BIG = -1e30
VMEM_LIMIT = 56 * 1024 * 1024


def _cparams(sem):
    return pltpu.CompilerParams(dimension_semantics=sem, vmem_limit_bytes=VMEM_LIMIT)


def _rms(x, eps):
    return x * lax.rsqrt(jnp.mean(x * x, axis=-1, keepdims=True) + eps)


def _pack_bf16_pairs(x):
    n = x.shape[1] // 2
    bits = lax.bitcast_convert_type(x.astype(BF16).astype(F32), I32)
    lo = lax.shift_right_logical(bits[:, :n], 16)
    return lo | bits[:, n:]


def _unpack_bf16_pairs(p):
    lo = lax.bitcast_convert_type(lax.shift_left(p, 16), F32)
    hi = lax.bitcast_convert_type(p & jnp.int32(-65536), F32)
    return jnp.concatenate([lo, hi], axis=1)


def _ada_kernel(c_ref, w_ref, b_ref, o_ref):
    ca = jax.nn.silu(c_ref[...])
    o_ref[...] = jnp.dot(ca.astype(BF16), w_ref[...].astype(BF16), preferred_element_type=F32) + b_ref[...]


def _ada(c, w_ada, b_ada):
    bsz, d = c.shape
    n = w_ada.shape[1]
    tn = n // 4
    return pl.pallas_call(
        _ada_kernel,
        grid=(n // tn,),
        in_specs=[pl.BlockSpec((bsz, d), lambda j: (0, 0)),
                  pl.BlockSpec((d, tn), lambda j: (0, j)),
                  pl.BlockSpec((1, tn), lambda j: (0, j))],
        out_specs=pl.BlockSpec((bsz, tn), lambda j: (0, j)),
        out_shape=jax.ShapeDtypeStruct((bsz, n), F32),
        compiler_params=_cparams(("arbitrary",)),
        name="ada",
    )(c, w_ada, b_ada.reshape(1, n))


def _inproj_kernel(tiles_per_seq, x_ref, pos_ref, ada_ref, g1_ref, win_ref, convw_ref, gconv_ref, invf_ref,
                   yconv_ref, qt_ref, k_ref, vt_ref, ubuf):
    i = pl.program_id(0)
    tm = x_ref.shape[0]
    x = x_ref[...]
    ada = ada_ref[0]
    sh1, sc1 = ada[0:1], ada[1:2]
    hn = _rms(x, NORM_EPS) * g1_ref[...] * (1.0 + sc1) + sh1
    hb = hn.astype(BF16)

    pc = jnp.dot(hb, win_ref[:, 0:3 * D_CONV], preferred_element_type=F32)
    u = pc[:, 2 * D_CONV:3 * D_CONV] * pc[:, 0:D_CONV]
    first = (i % tiles_per_seq) == 0

    @pl.when(first)
    def _():
        ubuf[0:8, :] = jnp.zeros((8, D_CONV), F32)

    @pl.when(jnp.logical_not(first))
    def _():
        ubuf[0:8, :] = ubuf[tm:tm + 8, :]

    ubuf[8:8 + tm, :] = u
    u1 = ubuf[7:7 + tm, :]
    u2 = ubuf[6:6 + tm, :]
    cw = convw_ref[...]
    yc = pc[:, D_CONV:2 * D_CONV] * (cw[0:1] * u2 + cw[1:2] * u1 + cw[2:3] * u)
    yconv_ref[...] = (_rms(yc, NORM_EPS) * gconv_ref[...]).astype(BF16)

    half = ROT_DIM // 2
    ang = invf_ref[...] * pos_ref[...].astype(F32)
    cos_ft = jnp.cos(ang)
    sin_ft = jnp.sin(ang)
    zero_h = jnp.zeros((half, tm), F32)
    zero_r = jnp.zeros((HEAD_DIM - ROT_DIM, tm), F32)
    lay = lambda a, b, r: jnp.concatenate([a, b, r, a, b, r], axis=0).T
    cos = lay(cos_ft, cos_ft, zero_r + 1.0)
    s_up = lay(-sin_ft, zero_h, zero_r)
    s_dn = lay(zero_h, sin_ft, zero_r)

    def rope(t):
        return t * cos + pltpu.roll(t, LANES - half, axis=1) * s_up + pltpu.roll(t, half, axis=1) * s_dn

    pq = jnp.dot(hb, win_ref[:, 3 * D_CONV:3 * D_CONV + D_QK], preferred_element_type=F32)
    pk = jnp.dot(hb, win_ref[:, 3 * D_CONV + D_QK:3 * D_CONV + 2 * D_QK], preferred_element_type=F32)
    scale = HEAD_DIM ** -0.5 * math.log2(math.e)
    qs = []
    for h in range(N_HEADS):
        sl = slice(h * LANES, (h + 1) * LANES)
        qs.append(rope(pq[:, sl]) * scale)
        k_ref[:, sl] = rope(pk[:, sl]).astype(BF16)
    qt_ref[0] = jnp.concatenate(qs, axis=1).T.astype(BF16)
    pv = jnp.dot(hb, win_ref[:, 3 * D_CONV + 2 * D_QK:], preferred_element_type=F32)
    vt_ref[0] = pv.T.astype(BF16)


def _inproj(xf, pos, ada3, g1, win_b, conv_w, g_conv, invf, seq):
    t, d = xf.shape
    tm = min(512, seq)
    tiles_per_seq = seq // tm
    n_in = win_b.shape[1]
    full = lambda shape: pl.BlockSpec(shape, lambda i: (0,) * len(shape))
    row = lambda w: pl.BlockSpec((tm, w), lambda i: (i, 0))
    colmajor = lambda w: pl.BlockSpec((1, w, tm), lambda i: (i // tiles_per_seq, 0, i % tiles_per_seq))
    return pl.pallas_call(
        functools.partial(_inproj_kernel, tiles_per_seq),
        grid=(t // tm,),
        in_specs=[row(d), pl.BlockSpec((1, tm), lambda i: (0, i)),
                  pl.BlockSpec((1, 6, d), lambda i: (i // tiles_per_seq, 0, 0)),
                  full((1, d)), full((d, n_in)), full((CONV_WIDTH, D_CONV)), full((1, D_CONV)),
                  full((ROT_DIM // 2, 1))],
        out_specs=[row(D_CONV), colmajor(D_QK), row(D_QK), colmajor(D_ATTN)],
        out_shape=[jax.ShapeDtypeStruct((t, D_CONV), BF16), jax.ShapeDtypeStruct((t // seq, D_QK, seq), BF16),
                   jax.ShapeDtypeStruct((t, D_QK), BF16), jax.ShapeDtypeStruct((t // seq, D_ATTN, seq), BF16)],
        scratch_shapes=[pltpu.VMEM((tm + 8, D_CONV), F32)],
        compiler_params=_cparams(("arbitrary",)),
        name="inproj",
    )(xf, pos, ada3, g1, win_b, conv_w, g_conv, invf)


def _attn_steps(nq):
    off = [(j, i) for j in range(nq) for i in range(j + 1, nq)]
    diag = [(i, i) for i in range(nq)]
    return off, diag


def _attn_kernel(steps_ref, qt_ref, k_ref, vt_ref, lamp_ref, gs_ref, o_ref,
                 qq_ref, m_ref, l_ref, acc_ref, s0_ref, s1_ref):
    nq = m_ref.shape[0]
    tq = s0_ref.shape[0]
    tk = tq
    off, diag = _attn_steps(nq)
    n_steps = len(off) + len(diag)
    bufs = (s0_ref, s1_ref)

    feat = lax.broadcasted_iota(I32, (LANES, tq), 0)
    for i in range(nq):
        qt = qt_ref[0, :, i * tq:(i + 1) * tq]
        zero = jnp.zeros_like(qt)
        qq_ref[:, 2 * i * tq:(2 * i + 1) * tq] = jnp.where(feat < HEAD_DIM, qt, zero)
        qq_ref[:, (2 * i + 1) * tq:(2 * i + 2) * tq] = jnp.where(feat >= HEAD_DIM, qt, zero)
    m_ref[...] = jnp.full(m_ref.shape, NEG_BIG, F32)
    l_ref[...] = jnp.zeros(l_ref.shape, F32)
    acc_ref[...] = jnp.zeros(acc_ref.shape, F32)

    def scores(j, i, dst):
        kstart = pl.multiple_of(j * tk, tk)
        qstart = pl.multiple_of(i * (2 * tq), 2 * tq)
        dst[...] = jnp.dot(k_ref[pl.ds(kstart, tk), :], qq_ref[:, pl.ds(qstart, 2 * tq)],
                           preferred_element_type=F32)

    def consume(src, j, i, masked):
        kstart = pl.multiple_of(j * tk, tk)
        vtb = vt_ref[0, :, pl.ds(kstart, tk)]
        s = src[...]
        if masked:
            key = lax.broadcasted_iota(I32, s.shape, 0)
            col = lax.broadcasted_iota(I32, s.shape, 1)
            qpos = jnp.where(col >= tq, col - tq, col)
            s = jnp.where(key <= qpos, s, NEG_BIG)
        m_prev = m_ref[i]
        m_new = jnp.maximum(m_prev, jnp.max(s, axis=0, keepdims=True))
        alpha = jnp.exp2(m_prev - m_new)
        p = jnp.exp2(s - m_new)
        l_ref[i] = alpha * l_ref[i] + jnp.sum(p, axis=0, keepdims=True)
        acc_ref[i] = alpha * acc_ref[i] + jnp.dot(vtb, p.astype(BF16), preferred_element_type=F32)
        m_ref[i] = m_new

    def at(n):
        return steps_ref[0, n], steps_ref[1, n]

    scores(*at(0), s0_ref)
    n_pairs = len(off) // 2

    def pair(pp, carry):
        n = 2 * pp
        scores(*at(n + 1), s1_ref)
        consume(s0_ref, *at(n), False)
        scores(*at(n + 2), s0_ref)
        consume(s1_ref, *at(n + 1), False)
        return carry

    lax.fori_loop(0, n_pairs, pair, 0)

    static_steps = (off + diag)[2 * n_pairs:]
    for n, (j, i) in enumerate(static_steps, start=2 * n_pairs):
        if n + 1 < n_steps:
            scores(*static_steps[n + 1 - 2 * n_pairs], bufs[(n + 1) % 2])
        consume(bufs[n % 2], j, i, masked=(j == i))

    lamp = lamp_ref[...]
    lam = (jnp.exp(jnp.sum(lamp[0:1] * lamp[1:2], axis=1, keepdims=True))
           - jnp.exp(jnp.sum(lamp[2:3] * lamp[3:4], axis=1, keepdims=True)) + LAMBDA_INIT)
    for i in range(nq):
        o = acc_ref[i] / l_ref[i]
        od = o[:, 0:tq] - lam * o[:, tq:2 * tq]
        y = od * lax.rsqrt(jnp.mean(od * od, axis=0, keepdims=True) + SUBLN_EPS) * gs_ref[...] * (1.0 - LAMBDA_INIT)
        o_ref[i * tq:(i + 1) * tq, :] = y.T.astype(BF16)


def _attention(qt, k, vt, lamp, g_subln, bsz, seq):
    t = k.shape[0]
    tq = min(512, seq)
    nq = seq // tq
    off, diag = _attn_steps(nq)
    steps = jnp.asarray(list(zip(*(off + diag))), I32)
    return pl.pallas_call(
        _attn_kernel,
        grid_spec=pltpu.PrefetchScalarGridSpec(
            num_scalar_prefetch=1,
            grid=(bsz, N_HEADS),
            in_specs=[pl.BlockSpec((1, LANES, seq), lambda b, h, st: (b, h, 0)),
                      pl.BlockSpec((seq, LANES), lambda b, h, st: (b, h)),
                      pl.BlockSpec((1, V_DIM, seq), lambda b, h, st: (b, h, 0)),
                      pl.BlockSpec((4, HEAD_DIM), lambda b, h, st: (0, 0)),
                      pl.BlockSpec((V_DIM, 1), lambda b, h, st: (0, 0))],
            out_specs=pl.BlockSpec((seq, LANES), lambda b, h, st: (b, h)),
            scratch_shapes=[pltpu.VMEM((LANES, 2 * seq), BF16), pltpu.VMEM((nq, 1, 2 * tq), F32),
                            pltpu.VMEM((nq, 1, 2 * tq), F32), pltpu.VMEM((nq, V_DIM, 2 * tq), F32),
                            pltpu.VMEM((tq, 2 * tq), F32), pltpu.VMEM((tq, 2 * tq), F32)],
        ),
        out_shape=jax.ShapeDtypeStruct((t, D_ATTN), BF16),
        compiler_params=_cparams(("arbitrary", "arbitrary")),
        name="attention",
    )(steps, qt, k, vt, lamp, g_subln)


def _first_index(hit, idx, size, axis):
    return jnp.min(jnp.where(hit, idx, size), axis=axis, keepdims=True)


def _outproj_router_kernel(yc_ref, ya_ref, x_ref, ada_ref, g2_ref, wout_ref, wrt_ref, rb_ref,
                           x1_ref, hp_ref, idx_ref, wts_ref, rank_ref, cnt_ref, base_ref):
    i = pl.program_id(0)
    tm = x_ref.shape[0]
    ada = ada_ref[0]
    gt1, sh2, sc2 = ada[2:3], ada[3:4], ada[4:5]
    mix = (jnp.dot(yc_ref[...], wout_ref[0:D_CONV, :], preferred_element_type=F32)
           + jnp.dot(ya_ref[...], wout_ref[D_CONV:, :], preferred_element_type=F32))
    x1 = x_ref[...] + gt1 * mix
    x1_ref[...] = x1
    hn2 = _rms(x1, NORM_EPS) * g2_ref[...] * (1.0 + sc2) + sh2
    hp_ref[...] = _pack_bf16_pairs(hn2)
    hb = hn2.astype(BF16)

    logits = lax.dot_general(wrt_ref[...], hb, (((1,), (1,)), ((), ())), preferred_element_type=F32)
    scores = jax.nn.sigmoid(logits)
    choice = scores + rb_ref[...]
    neg_inf = jnp.float32(-jnp.inf)

    ch3 = choice.reshape(N_GROUPS, GROUP_SIZE, tm)
    i3 = lax.broadcasted_iota(I32, ch3.shape, 1)
    m1 = jnp.max(ch3, axis=1, keepdims=True)
    f1 = _first_index(ch3 == m1, i3, GROUP_SIZE, 1)
    m2 = jnp.max(jnp.where(i3 == f1, neg_inf, ch3), axis=1, keepdims=True)
    gs = (m1 + m2).reshape(N_GROUPS, tm)

    gi = lax.broadcasted_iota(I32, gs.shape, 0)
    gkeep = jnp.zeros(gs.shape, F32)
    for _ in range(TOPK_GROUPS):
        m = jnp.max(gs, axis=0, keepdims=True)
        f = _first_index(gs == m, gi, N_GROUPS, 0)
        sel = gi == f
        gkeep = jnp.where(sel, 1.0, gkeep)
        gs = jnp.where(sel, neg_inf, gs)
    ekeep = jnp.broadcast_to(gkeep.reshape(N_GROUPS, 1, tm), (N_GROUPS, GROUP_SIZE, tm)).reshape(N_EXPERTS, tm)
    masked = jnp.where(ekeep > 0.0, choice, neg_inf)

    ei = lax.broadcasted_iota(I32, masked.shape, 0)
    picked = jnp.zeros(masked.shape, F32)
    idxs, ws = [], []
    for _ in range(TOP_K):
        m = jnp.max(masked, axis=0, keepdims=True)
        f = _first_index(masked == m, ei, N_EXPERTS, 0)
        sel = ei == f
        idxs.append(f)
        ws.append(jnp.sum(jnp.where(sel, scores, 0.0), axis=0, keepdims=True))
        picked = jnp.where(sel, 1.0, picked)
        masked = jnp.where(sel, neg_inf, masked)
    wsum = ws[0]
    for wk in ws[1:]:
        wsum = wsum + wk
    denom = wsum + 1e-20
    for kk in range(TOP_K):
        idx_ref[kk:kk + 1, :] = idxs[kk]
    wrep = jnp.concatenate([jnp.broadcast_to(ws[kk] / denom * ROUTED_SCALE, (SC_LANES, tm)) for kk in range(TOP_K)],
                           axis=0)
    wts_ref[...] = wrep.T

    @pl.when(i == 0)
    def _():
        base_ref[...] = jnp.zeros(base_ref.shape, F32)

    si = lax.broadcasted_iota(I32, (tm, tm), 0)
    ti = lax.broadcasted_iota(I32, (tm, tm), 1)
    earlier = jnp.where(si < ti, 1.0, 0.0).astype(BF16)
    before = jnp.dot(picked.astype(BF16), earlier, preferred_element_type=F32) + base_ref[...]
    for kk in range(TOP_K):
        rank_ref[kk:kk + 1, :] = jnp.sum(jnp.where(ei == idxs[kk], before, 0.0), axis=0, keepdims=True).astype(I32)
    base_ref[...] = base_ref[...] + jnp.sum(picked, axis=1, keepdims=True)
    cnt_ref[...] = base_ref[...]


def _outproj_router(yconv, yattn, xf, ada3, g2, wout_b, wrt_b, rbias, seq):
    t, d = xf.shape
    tm = min(512, seq)
    tiles_per_seq = seq // tm
    full = lambda shape: pl.BlockSpec(shape, lambda i: (0,) * len(shape))
    row = lambda w: pl.BlockSpec((tm, w), lambda i: (i, 0))
    col = pl.BlockSpec((TOP_K, tm), lambda i: (0, i))
    return pl.pallas_call(
        _outproj_router_kernel,
        grid=(t // tm,),
        in_specs=[row(D_CONV), row(D_ATTN), row(d),
                  pl.BlockSpec((1, 6, d), lambda i: (i // tiles_per_seq, 0, 0)),
                  full((1, d)), full((D_CONV + D_ATTN, d)), full((N_EXPERTS, d)), full((N_EXPERTS, 1))],
        out_specs=[row(d), row(d // 2), col, row(TOP_K * SC_LANES), col, full((N_EXPERTS, 1))],
        out_shape=[jax.ShapeDtypeStruct((t, d), F32), jax.ShapeDtypeStruct((t, d // 2), I32),
                   jax.ShapeDtypeStruct((TOP_K, t), I32), jax.ShapeDtypeStruct((t, TOP_K * SC_LANES), F32),
                   jax.ShapeDtypeStruct((TOP_K, t), I32), jax.ShapeDtypeStruct((N_EXPERTS, 1), F32)],
        scratch_shapes=[pltpu.VMEM((N_EXPERTS, 1), F32)],
        compiler_params=_cparams(("arbitrary",)),
        name="outproj_router",
    )(yconv, yattn, xf, ada3, g2, wout_b, wrt_b, rbias)


def _dest_kernel(idx_ref, rank_ref, ps_ref, dest_ref):
    tm = idx_ref.shape[1]
    ei = lax.broadcasted_iota(I32, (N_EXPERTS, tm), 0)
    ps = ps_ref[...]
    for kk in range(TOP_K):
        start = jnp.sum(jnp.where(ei == idx_ref[kk:kk + 1, :], ps, 0.0), axis=0, keepdims=True)
        dest_ref[kk:kk + 1, :] = start.astype(I32) + rank_ref[kk:kk + 1, :]


def _dest(idx_t, rank_t, pad_start_f):
    t = idx_t.shape[1]
    tm = min(2048, t)
    col = pl.BlockSpec((TOP_K, tm), lambda i: (0, i))
    return pl.pallas_call(
        _dest_kernel,
        grid=(t // tm,),
        in_specs=[col, col, pl.BlockSpec((N_EXPERTS, 1), lambda i: (0, 0))],
        out_specs=col,
        out_shape=jax.ShapeDtypeStruct((TOP_K, t), I32),
        compiler_params=_cparams(("arbitrary",)),
        name="dest",
    )(idx_t, rank_t, pad_start_f)


def _sc_dispatch(hp, dest_ck, n_rows):
    t, w = hp.shape
    nk = dest_ck.shape[1]
    per_worker = t // SC_CHUNK // SC_WORKERS
    mesh = plsc.VectorSubcoreMesh(core_axis_name="c", subcore_axis_name="s")

    @functools.partial(
        pl.kernel, mesh=mesh,
        out_type=jax.ShapeDtypeStruct((n_rows, w), hp.dtype),
        scratch_types=[pltpu.VMEM((nk, SC_CHUNK), I32), pltpu.VMEM((SC_CHUNK, w), hp.dtype), pltpu.SemaphoreType.DMA],
    )
    def k(hp_hbm, dest_hbm, xs_hbm, idx_v, rows_v, sem):
        wid = lax.axis_index("s") * SC_CORES + lax.axis_index("c")

        @pl.loop(0, per_worker)
        def _(j):
            c = wid * per_worker + j
            pltpu.sync_copy(dest_hbm.at[c], idx_v)
            pltpu.sync_copy(hp_hbm.at[pl.ds(c * SC_CHUNK, SC_CHUNK)], rows_v)
            for kk in range(nk):
                pltpu.async_copy(rows_v, xs_hbm.at[idx_v.at[kk]], sem)
            for kk in range(nk):
                pltpu.make_async_copy(rows_v, xs_hbm.at[idx_v.at[kk]], sem).wait()

    return k(hp, dest_ck)


def _sc_combine(src, dest_ck, wrep, n_tokens):
    half = src.shape[1]
    lanes = wrep.shape[1] // TOP_K
    ctok = COMBINE_SLOTS // TOP_K
    per_worker = n_tokens // ctok // SC_WORKERS
    steps_per_chunk = SC_CHUNK // ctok
    chunks_per_worker = per_worker // steps_per_chunk
    mesh = plsc.VectorSubcoreMesh(core_axis_name="c", subcore_axis_name="s")

    @functools.partial(
        pl.kernel, mesh=mesh,
        out_type=jax.ShapeDtypeStruct((n_tokens, 2 * half), F32),
        scratch_types=[pltpu.VMEM((chunks_per_worker, TOP_K, SC_CHUNK), I32),
                       pltpu.VMEM((2, COMBINE_SLOTS, half), I32), pltpu.VMEM((2, ctok, TOP_K * lanes), F32),
                       pltpu.VMEM((2, ctok, 2 * half), F32),
                       pltpu.SemaphoreType.DMA((2,)), pltpu.SemaphoreType.DMA((2,)), pltpu.SemaphoreType.DMA((2,))],
        compiler_params=pltpu.CompilerParams(needs_layout_passes=False),
    )
    def k(src_hbm, idx_hbm, w_hbm, y_hbm, idx_v, rows_v, w_v, out_v, sem_g, sem_w, sem_o):
        wid = lax.axis_index("s") * SC_CORES + lax.axis_index("c")
        base = wid * per_worker
        pltpu.sync_copy(idx_hbm.at[pl.ds(wid * chunks_per_worker, chunks_per_worker)], idx_v)

        class _Gather:
            def __init__(self, c, slot):
                chunk = c // steps_per_chunk
                first = (c % steps_per_chunk) * ctok
                self.copies = [
                    pltpu.make_async_copy(src_hbm.at[idx_v.at[chunk, kk, pl.ds(first, ctok)]],
                                          rows_v.at[slot, pl.ds(kk * ctok, ctok)], sem_g.at[slot])
                    for kk in range(TOP_K)]

            def start(self):
                for cp in self.copies:
                    cp.start()

            def wait(self):
                for cp in self.copies:
                    cp.wait()

        gather = _Gather

        def wload(c, slot):
            return pltpu.make_async_copy(w_hbm.at[pl.ds((base + c) * ctok, ctok)], w_v.at[slot], sem_w.at[slot])

        def store(c, slot):
            return pltpu.make_async_copy(out_v.at[slot], y_hbm.at[pl.ds((base + c) * ctok, ctok)], sem_o.at[slot])

        gather(0, 0).start()
        wload(0, 0).start()

        @pl.loop(0, per_worker, step=2)
        def _(c0):
            for slot in range(2):
                c = c0 + slot
                gather(c, slot).wait()
                wload(c, slot).wait()

                @pl.when(c + 1 < per_worker)
                def _():
                    gather(c + 1, 1 - slot).start()
                    wload(c + 1, 1 - slot).start()

                @pl.when(c >= 2)
                def _():
                    store(c - 2, slot).wait()

                @pl.loop(0, ctok)
                def _(tt):
                    ws = [w_v[slot, tt, pl.ds(kk * lanes, lanes)] for kk in range(TOP_K)]

                    @plsc.parallel_loop(0, half // lanes, unroll=2)
                    def _(v):
                        lo = jnp.zeros((lanes,), F32)
                        hi = jnp.zeros((lanes,), F32)
                        for kk in range(TOP_K):
                            x = rows_v[slot, kk * ctok + tt, pl.ds(v * lanes, lanes)]
                            x_lo, x_hi = plsc.unpack(plsc.bitcast(x, BF16), format=plsc.PackFormat.INTERLEAVED)
                            lo = lo + x_lo * ws[kk]
                            hi = hi + x_hi * ws[kk]
                        out_v[slot, tt, pl.ds(v * lanes, lanes)] = lo
                        out_v[slot, tt, pl.ds(half + v * lanes, lanes)] = hi

                store(c, slot).start()

        store(per_worker - 2, 0).wait()
        store(per_worker - 1, 1).wait()

    return k(src, dest_ck, wrep)


def _experts_kernel(bstart_ref, xs_hbm, wg_ref, wu_ref, wd_ref, out_hbm,
                    wgu_b, wd_b, xbuf, obuf, sem_in, sem_out):
    e = pl.program_id(0)
    b0 = bstart_ref[e]
    b1 = bstart_ref[e + 1]
    n_used = bstart_ref[N_EXPERTS]
    nbuf = EXPERT_BUFS

    def in_copy(g, slot):
        return pltpu.make_async_copy(xs_hbm.at[pl.ds(g * ROW_TILE, ROW_TILE)], xbuf.at[slot], sem_in.at[slot])

    def out_copy(g, slot):
        return pltpu.make_async_copy(obuf.at[slot], out_hbm.at[pl.ds(g * ROW_TILE, ROW_TILE)], sem_out.at[slot])

    @pl.when(e == 0)
    def _():
        for g in range(nbuf - 1):
            @pl.when(g < n_used)
            def _():
                in_copy(g, g).start()

    @pl.when(b1 > b0)
    def _():
        wgu_b[:, 0:D_EXPERT] = wg_ref[...].astype(BF16)
        wgu_b[:, D_EXPERT:2 * D_EXPERT] = wu_ref[...].astype(BF16)
        wd_b[...] = wd_ref[...].astype(BF16)

        def fetch(g):
            @pl.when(g < n_used)
            def _():
                in_copy(g, lax.rem(g, nbuf)).start()

        def run(blocks):
            slots = [lax.rem(g, nbuf) for g in blocks]
            for g, slot in zip(blocks, slots):
                in_copy(g, slot).wait()
            fetch(blocks[0] + nbuf - 1)
            for g, slot in zip(blocks, slots):
                @pl.when(g >= nbuf)
                def _():
                    out_copy(g - nbuf, slot).wait()
            xs_rows = [_unpack_bf16_pairs(xbuf[slot]).astype(BF16) for slot in slots]
            xb = xs_rows[0] if len(slots) == 1 else jnp.concatenate(xs_rows, axis=0)
            gu = jnp.dot(xb, wgu_b[...], preferred_element_type=F32)
            act = (jax.nn.silu(gu[:, 0:D_EXPERT]) * gu[:, D_EXPERT:2 * D_EXPERT]).astype(BF16)
            out = _pack_bf16_pairs(jnp.dot(act, wd_b[...], preferred_element_type=F32))
            for n, slot in enumerate(slots):
                obuf[slot] = out[n * ROW_TILE:(n + 1) * ROW_TILE]
            for g, slot in zip(blocks, slots):
                out_copy(g, slot).start()
            for g in blocks[:-1]:
                fetch(g + nbuf)

        group = EXPERT_GROUP

        def full_group(jj, carry):
            g = b0 + group * jj
            run([g + n for n in range(group)])
            return carry

        n_full = (b1 - b0) // group
        lax.fori_loop(0, n_full, full_group, 0)
        rest = (b1 - b0) - n_full * group
        tail = b0 + n_full * group
        for size in range(1, group):
            @pl.when(rest == size)
            def _():
                run([tail + n for n in range(size)])

    @pl.when(e == N_EXPERTS - 1)
    def _():
        for back in range(1, nbuf + 1):
            @pl.when(n_used >= back)
            def _():
                g = n_used - back
                out_copy(g, lax.rem(g, nbuf)).wait()


def _experts(xs, bstart, w_gate, w_up, w_down):
    n_rows, half = xs.shape
    d = 2 * half
    weights = lambda shape: pl.BlockSpec((None,) + shape, lambda e, bs: (e, 0, 0))
    return pl.pallas_call(
        _experts_kernel,
        grid_spec=pltpu.PrefetchScalarGridSpec(
            num_scalar_prefetch=1,
            grid=(N_EXPERTS,),
            in_specs=[pl.BlockSpec(memory_space=pl.ANY),
                      weights((d, D_EXPERT)), weights((d, D_EXPERT)), weights((D_EXPERT, d))],
            out_specs=pl.BlockSpec(memory_space=pl.ANY),
            scratch_shapes=[pltpu.VMEM((d, 2 * D_EXPERT), BF16), pltpu.VMEM((D_EXPERT, d), BF16),
                            pltpu.VMEM((EXPERT_BUFS, ROW_TILE, half), I32),
                            pltpu.VMEM((EXPERT_BUFS, ROW_TILE, half), I32),
                            pltpu.SemaphoreType.DMA((EXPERT_BUFS,)), pltpu.SemaphoreType.DMA((EXPERT_BUFS,))],
        ),
        out_shape=jax.ShapeDtypeStruct((n_rows, half), I32),
        compiler_params=_cparams(("arbitrary",)),
        name="experts",
    )(bstart, xs, w_gate, w_up, w_down)


def _final_kernel(x1_ref, hp_ref, yr_ref, ada_ref, wgs_ref, wus_ref, wds_ref, gf_ref, o_ref):
    gt2 = ada_ref[0][5:6]
    hb = _unpack_bf16_pairs(hp_ref[...]).astype(BF16)
    g = jnp.dot(hb, wgs_ref[...], preferred_element_type=F32)
    u = jnp.dot(hb, wus_ref[...], preferred_element_type=F32)
    y = jnp.dot((jax.nn.silu(g) * u).astype(BF16), wds_ref[...], preferred_element_type=F32)
    x2 = x1_ref[...] + gt2 * (yr_ref[...] + y)
    o_ref[...] = _rms(x2, NORM_EPS) * gf_ref[...]


def _final(x1, hp, y_routed, ada3, wgs_b, wus_b, wds_b, g_final, seq):
    t, d = x1.shape
    tm = min(512, seq)
    tiles_per_seq = seq // tm
    full = lambda shape: pl.BlockSpec(shape, lambda i: (0,) * len(shape))
    row = lambda w: pl.BlockSpec((tm, w), lambda i: (i, 0))
    return pl.pallas_call(
        _final_kernel,
        grid=(t // tm,),
        in_specs=[row(d), row(d // 2), row(d),
                  pl.BlockSpec((1, 6, d), lambda i: (i // tiles_per_seq, 0, 0)),
                  full((d, D_EXPERT)), full((d, D_EXPERT)), full((D_EXPERT, d)), full((1, d))],
        out_specs=row(d),
        out_shape=jax.ShapeDtypeStruct((t, d), F32),
        compiler_params=_cparams(("arbitrary",)),
        name="final",
    )(x1, hp, y_routed, ada3, wgs_b, wus_b, wds_b, g_final)


def kernel(x, c, positions, w_ada, b_ada, g_norm1, w_in, conv_w, g_conv_out, lam_q1, lam_k1, lam_q2, lam_k2, g_subln, w_out, g_norm2, w_router, router_bias, w_gate_e, w_up_e, w_down_e, w_gate_s, w_up_s, w_down_s, g_final):
    bsz, seq, d = x.shape
    t = bsz * seq
    xf = x.reshape(t, d)
    pos = positions.reshape(1, t)
    invf = (ROPE_THETA ** (-jnp.arange(0, ROT_DIM, 2, dtype=F32) / ROT_DIM)).reshape(ROT_DIM // 2, 1)

    ada3 = _ada(c, w_ada[0], b_ada[0]).reshape(bsz, 6, d)
    yconv, qt, k, vt = _inproj(xf, pos, ada3, g_norm1[0].reshape(1, d), w_in[0].astype(BF16), conv_w[0],
                             g_conv_out[0].reshape(1, D_CONV), invf, seq)
    lamp = jnp.stack([lam_q1[0], lam_k1[0], lam_q2[0], lam_k2[0]]).astype(F32)
    yattn = _attention(qt, k, vt, lamp, g_subln[0].reshape(V_DIM, 1), bsz, seq)

    x1, hp, idx_t, wrep, rank_t, counts_f = _outproj_router(
        yconv, yattn, xf, ada3, g_norm2[0].reshape(1, d), w_out[0].astype(BF16),
        w_router[0].T.astype(BF16), router_bias[0].reshape(N_EXPERTS, 1), seq)

    counts = counts_f[:, 0].astype(I32)
    padded = ((counts + ROW_TILE - 1) // ROW_TILE) * ROW_TILE
    pad_end = jnp.cumsum(padded)
    pad_start = pad_end - padded
    nb = (t * TOP_K + N_EXPERTS * (ROW_TILE - 1)) // ROW_TILE
    bstart = (jnp.concatenate([pad_start, pad_end[-1:]]) // ROW_TILE).astype(I32)

    dest_t = _dest(idx_t, rank_t, pad_start.astype(F32).reshape(N_EXPERTS, 1))
    n_chunks = t // SC_CHUNK
    dest_ck = dest_t.reshape(TOP_K, n_chunks, SC_CHUNK).transpose(1, 0, 2)

    xs = _sc_dispatch(hp, dest_ck, nb * ROW_TILE)
    outs = _experts(xs, bstart, w_gate_e[0], w_up_e[0], w_down_e[0])
    y_routed = _sc_combine(outs, dest_ck, wrep, t)

    out = _final(x1, hp, y_routed, ada3,
                 w_gate_s[0].astype(BF16), w_up_s[0].astype(BF16), w_down_s[0].astype(BF16),
                 g_final.reshape(1, d), seq)
    return out.reshape(bsz, seq, d)
```

```python
import functools
import math

import jax
import jax.numpy as jnp
from jax import lax
from jax.experimental import pallas as pl
from jax.experimental.pallas import tpu as pltpu
from jax.experimental.pallas import tpu_sc as plsc

F32 = jnp.float32
BF16 = jnp.bfloat16
I32 = jnp.int32

D_CONV = 512
CONV_WIDTH = 3
N_HEADS = 4
HEAD_DIM = 64
V_DIM = 2 * HEAD_DIM
D_ATTN = N_HEADS * V_DIM
D_QK = N_HEADS * 2 * HEAD_DIM
ROT_DIM = HEAD_DIM // 4
ROPE_THETA = 500000.0
N_EXPERTS = 256
TOP_K = 8
N_GROUPS = 8
GROUP_SIZE = N_EXPERTS // N_GROUPS
TOPK_GROUPS = 4
D_EXPERT = 256
ROUTED_SCALE = 2.5
NORM_EPS = 1e-6
SUBLN_EPS = 1e-5
LAMBDA_INIT = 0.8 - 0.6 * math.exp(-0.3 * 0)

LANES = 128
SC_CORES = 2
SC_SUBCORES = 16
SC_WORKERS = SC_CORES * SC_SUBCORES
SC_CHUNK = 128
COMBINE_SLOTS = 64
SC_LANES = 16

ROW_TILE = 128
EXPERT_GROUP = 4
EXPERTS_PER_STEP = 2
EXPERT_BUFS = 12
NEG_BIG = -1e30
VMEM_LIMIT = 56 * 1024 * 1024


def _cparams(sem):
    return pltpu.CompilerParams(dimension_semantics=sem, vmem_limit_bytes=VMEM_LIMIT)


def _rms(x, eps):
    return x * lax.rsqrt(jnp.mean(x * x, axis=-1, keepdims=True) + eps)


def _pack_bf16_pairs(x):
    n = x.shape[1] // 2
    bits = lax.bitcast_convert_type(x.astype(BF16).astype(F32), I32)
    lo = lax.shift_right_logical(bits[:, :n], 16)
    return lo | bits[:, n:]


def _unpack_bf16_pairs(p):
    lo = lax.bitcast_convert_type(lax.shift_left(p, 16), F32)
    hi = lax.bitcast_convert_type(p & jnp.int32(-65536), F32)
    return jnp.concatenate([lo, hi], axis=1)


def _ada_kernel(c_ref, w_ref, b_ref, o_ref):
    ca = jax.nn.silu(c_ref[...])
    o_ref[...] = jnp.dot(ca.astype(BF16), w_ref[...].astype(BF16), preferred_element_type=F32) + b_ref[...]


def _ada(c, w_ada, b_ada):
    bsz, d = c.shape
    n = w_ada.shape[1]
    tn = n // 4
    return pl.pallas_call(
        _ada_kernel,
        grid=(n // tn,),
        in_specs=[pl.BlockSpec((bsz, d), lambda j: (0, 0)),
                  pl.BlockSpec((d, tn), lambda j: (0, j)),
                  pl.BlockSpec((1, tn), lambda j: (0, j))],
        out_specs=pl.BlockSpec((bsz, tn), lambda j: (0, j)),
        out_shape=jax.ShapeDtypeStruct((bsz, n), F32),
        compiler_params=_cparams(("arbitrary",)),
        name="ada",
    )(c, w_ada, b_ada.reshape(1, n))


def _inproj_kernel(tiles_per_seq, x_ref, pos_ref, ada_ref, g1_ref, win_ref, convw_ref, gconv_ref, invf_ref,
                   yconv_ref, qt_ref, k_ref, vt_ref, ubuf):
    i = pl.program_id(0)
    tm = x_ref.shape[0]
    x = x_ref[...]
    ada = ada_ref[0]
    sh1, sc1 = ada[0:1], ada[1:2]
    hn = _rms(x, NORM_EPS) * g1_ref[...] * (1.0 + sc1) + sh1
    hb = hn.astype(BF16)

    pc = jnp.dot(hb, win_ref[:, 0:3 * D_CONV], preferred_element_type=F32)
    u = pc[:, 2 * D_CONV:3 * D_CONV] * pc[:, 0:D_CONV]
    first = (i % tiles_per_seq) == 0

    @pl.when(first)
    def _():
        ubuf[0:8, :] = jnp.zeros((8, D_CONV), F32)

    @pl.when(jnp.logical_not(first))
    def _():
        ubuf[0:8, :] = ubuf[tm:tm + 8, :]

    ubuf[8:8 + tm, :] = u
    u1 = ubuf[7:7 + tm, :]
    u2 = ubuf[6:6 + tm, :]
    cw = convw_ref[...]
    yc = pc[:, D_CONV:2 * D_CONV] * (cw[0:1] * u2 + cw[1:2] * u1 + cw[2:3] * u)
    yconv_ref[...] = (_rms(yc, NORM_EPS) * gconv_ref[...]).astype(BF16)

    half = ROT_DIM // 2
    ang = invf_ref[...] * pos_ref[...].astype(F32)
    cos_ft = jnp.cos(ang)
    sin_ft = jnp.sin(ang)
    zero_h = jnp.zeros((half, tm), F32)
    zero_r = jnp.zeros((HEAD_DIM - ROT_DIM, tm), F32)
    lay = lambda a, b, r: jnp.concatenate([a, b, r, a, b, r], axis=0).T
    cos = lay(cos_ft, cos_ft, zero_r + 1.0)
    s_up = lay(-sin_ft, zero_h, zero_r)
    s_dn = lay(zero_h, sin_ft, zero_r)

    def rope(t):
        return t * cos + pltpu.roll(t, LANES - half, axis=1) * s_up + pltpu.roll(t, half, axis=1) * s_dn

    pq = jnp.dot(hb, win_ref[:, 3 * D_CONV:3 * D_CONV + D_QK], preferred_element_type=F32)
    pk = jnp.dot(hb, win_ref[:, 3 * D_CONV + D_QK:3 * D_CONV + 2 * D_QK], preferred_element_type=F32)
    scale = HEAD_DIM ** -0.5 * math.log2(math.e)
    qs = []
    for h in range(N_HEADS):
        sl = slice(h * LANES, (h + 1) * LANES)
        qs.append(rope(pq[:, sl]) * scale)
        k_ref[:, sl] = rope(pk[:, sl]).astype(BF16)
    qt_ref[0] = jnp.concatenate(qs, axis=1).T.astype(BF16)
    pv = jnp.dot(hb, win_ref[:, 3 * D_CONV + 2 * D_QK:], preferred_element_type=F32)
    vt_ref[0] = pv.T.astype(BF16)


def _inproj(xf, pos, ada3, g1, win_b, conv_w, g_conv, invf, seq):
    t, d = xf.shape
    tm = min(512, seq)
    tiles_per_seq = seq // tm
    n_in = win_b.shape[1]
    full = lambda shape: pl.BlockSpec(shape, lambda i: (0,) * len(shape))
    row = lambda w: pl.BlockSpec((tm, w), lambda i: (i, 0))
    colmajor = lambda w: pl.BlockSpec((1, w, tm), lambda i: (i // tiles_per_seq, 0, i % tiles_per_seq))
    return pl.pallas_call(
        functools.partial(_inproj_kernel, tiles_per_seq),
        grid=(t // tm,),
        in_specs=[row(d), pl.BlockSpec((1, tm), lambda i: (0, i)),
                  pl.BlockSpec((1, 6, d), lambda i: (i // tiles_per_seq, 0, 0)),
                  full((1, d)), full((d, n_in)), full((CONV_WIDTH, D_CONV)), full((1, D_CONV)),
                  full((ROT_DIM // 2, 1))],
        out_specs=[row(D_CONV), colmajor(D_QK), row(D_QK), colmajor(D_ATTN)],
        out_shape=[jax.ShapeDtypeStruct((t, D_CONV), BF16), jax.ShapeDtypeStruct((t // seq, D_QK, seq), BF16),
                   jax.ShapeDtypeStruct((t, D_QK), BF16), jax.ShapeDtypeStruct((t // seq, D_ATTN, seq), BF16)],
        scratch_shapes=[pltpu.VMEM((tm + 8, D_CONV), F32)],
        compiler_params=_cparams(("arbitrary",)),
        name="inproj",
    )(xf, pos, ada3, g1, win_b, conv_w, g_conv, invf)


def _attn_steps(nq):
    off = [(j, i) for j in range(nq) for i in range(j + 1, nq)]
    diag = [(i, i) for i in range(nq)]
    return off, diag


def _attn_kernel(steps_ref, qt_ref, k_ref, vt_ref, lamp_ref, gs_ref, o_ref,
                 qq_ref, m_ref, l_ref, acc_ref, s0_ref, s1_ref):
    nq = m_ref.shape[0]
    tq = s0_ref.shape[0]
    tk = tq
    off, diag = _attn_steps(nq)
    n_steps = len(off) + len(diag)
    bufs = (s0_ref, s1_ref)

    feat = lax.broadcasted_iota(I32, (LANES, tq), 0)
    for i in range(nq):
        qt = qt_ref[0, :, i * tq:(i + 1) * tq]
        zero = jnp.zeros_like(qt)
        qq_ref[:, 2 * i * tq:(2 * i + 1) * tq] = jnp.where(feat < HEAD_DIM, qt, zero)
        qq_ref[:, (2 * i + 1) * tq:(2 * i + 2) * tq] = jnp.where(feat >= HEAD_DIM, qt, zero)
    m_ref[...] = jnp.full(m_ref.shape, NEG_BIG, F32)
    l_ref[...] = jnp.zeros(l_ref.shape, F32)
    acc_ref[...] = jnp.zeros(acc_ref.shape, F32)

    def scores(j, i, dst):
        kstart = pl.multiple_of(j * tk, tk)
        qstart = pl.multiple_of(i * (2 * tq), 2 * tq)
        dst[...] = jnp.dot(k_ref[pl.ds(kstart, tk), :], qq_ref[:, pl.ds(qstart, 2 * tq)],
                           preferred_element_type=F32)

    def consume(src, j, i, masked):
        kstart = pl.multiple_of(j * tk, tk)
        vtb = vt_ref[0, :, pl.ds(kstart, tk)]
        s = src[...]
        if masked:
            key = lax.broadcasted_iota(I32, s.shape, 0)
            col = lax.broadcasted_iota(I32, s.shape, 1)
            qpos = jnp.where(col >= tq, col - tq, col)
            s = jnp.where(key <= qpos, s, NEG_BIG)
        m_prev = m_ref[i]
        m_new = jnp.maximum(m_prev, jnp.max(s, axis=0, keepdims=True))
        alpha = jnp.exp2(m_prev - m_new)
        p = jnp.exp2(s - m_new)
        l_ref[i] = alpha * l_ref[i] + jnp.sum(p, axis=0, keepdims=True)
        acc_ref[i] = alpha * acc_ref[i] + jnp.dot(vtb, p.astype(BF16), preferred_element_type=F32)
        m_ref[i] = m_new

    def at(n):
        return steps_ref[0, n], steps_ref[1, n]

    scores(*at(0), s0_ref)
    n_pairs = len(off) // 2

    def pair(pp, carry):
        n = 2 * pp
        scores(*at(n + 1), s1_ref)
        consume(s0_ref, *at(n), False)
        scores(*at(n + 2), s0_ref)
        consume(s1_ref, *at(n + 1), False)
        return carry

    lax.fori_loop(0, n_pairs, pair, 0)

    static_steps = (off + diag)[2 * n_pairs:]
    for n, (j, i) in enumerate(static_steps, start=2 * n_pairs):
        if n + 1 < n_steps:
            scores(*static_steps[n + 1 - 2 * n_pairs], bufs[(n + 1) % 2])
        consume(bufs[n % 2], j, i, masked=(j == i))

    lamp = lamp_ref[...]
    lam = (jnp.exp(jnp.sum(lamp[0:1] * lamp[1:2], axis=1, keepdims=True))
           - jnp.exp(jnp.sum(lamp[2:3] * lamp[3:4], axis=1, keepdims=True)) + LAMBDA_INIT)
    for i in range(nq):
        o = acc_ref[i] / l_ref[i]
        od = o[:, 0:tq] - lam * o[:, tq:2 * tq]
        y = od * lax.rsqrt(jnp.mean(od * od, axis=0, keepdims=True) + SUBLN_EPS) * gs_ref[...] * (1.0 - LAMBDA_INIT)
        o_ref[i * tq:(i + 1) * tq, :] = y.T.astype(BF16)


def _attention(qt, k, vt, lamp, g_subln, bsz, seq):
    t = k.shape[0]
    tq = min(512, seq)
    nq = seq // tq
    off, diag = _attn_steps(nq)
    steps = jnp.asarray(list(zip(*(off + diag))), I32)
    return pl.pallas_call(
        _attn_kernel,
        grid_spec=pltpu.PrefetchScalarGridSpec(
            num_scalar_prefetch=1,
            grid=(bsz, N_HEADS),
            in_specs=[pl.BlockSpec((1, LANES, seq), lambda b, h, st: (b, h, 0)),
                      pl.BlockSpec((seq, LANES), lambda b, h, st: (b, h)),
                      pl.BlockSpec((1, V_DIM, seq), lambda b, h, st: (b, h, 0)),
                      pl.BlockSpec((4, HEAD_DIM), lambda b, h, st: (0, 0)),
                      pl.BlockSpec((V_DIM, 1), lambda b, h, st: (0, 0))],
            out_specs=pl.BlockSpec((seq, LANES), lambda b, h, st: (b, h)),
            scratch_shapes=[pltpu.VMEM((LANES, 2 * seq), BF16), pltpu.VMEM((nq, 1, 2 * tq), F32),
                            pltpu.VMEM((nq, 1, 2 * tq), F32), pltpu.VMEM((nq, V_DIM, 2 * tq), F32),
                            pltpu.VMEM((tq, 2 * tq), F32), pltpu.VMEM((tq, 2 * tq), F32)],
        ),
        out_shape=jax.ShapeDtypeStruct((t, D_ATTN), BF16),
        compiler_params=_cparams(("arbitrary", "arbitrary")),
        name="attention",
    )(steps, qt, k, vt, lamp, g_subln)


def _first_index(hit, idx, size, axis):
    return jnp.min(jnp.where(hit, idx, size), axis=axis, keepdims=True)


def _outproj_router_kernel(yc_ref, ya_ref, x_ref, ada_ref, g2_ref, wout_ref, wrt_ref, rb_ref,
                           x1_ref, hp_ref, idx_ref, wts_ref, rank_ref, cnt_ref, base_ref):
    i = pl.program_id(0)
    tm = x_ref.shape[0]
    ada = ada_ref[0]
    gt1, sh2, sc2 = ada[2:3], ada[3:4], ada[4:5]
    mix = (jnp.dot(yc_ref[...], wout_ref[0:D_CONV, :], preferred_element_type=F32)
           + jnp.dot(ya_ref[...], wout_ref[D_CONV:, :], preferred_element_type=F32))
    x1 = x_ref[...] + gt1 * mix
    x1_ref[...] = x1
    hn2 = _rms(x1, NORM_EPS) * g2_ref[...] * (1.0 + sc2) + sh2
    hp_ref[...] = _pack_bf16_pairs(hn2)
    hb = hn2.astype(BF16)

    logits = lax.dot_general(wrt_ref[...], hb, (((1,), (1,)), ((), ())), preferred_element_type=F32)
    scores = jax.nn.sigmoid(logits)
    choice = scores + rb_ref[...]
    neg_inf = jnp.float32(-jnp.inf)

    ch3 = choice.reshape(N_GROUPS, GROUP_SIZE, tm)
    i3 = lax.broadcasted_iota(I32, ch3.shape, 1)
    m1 = jnp.max(ch3, axis=1, keepdims=True)
    f1 = _first_index(ch3 == m1, i3, GROUP_SIZE, 1)
    m2 = jnp.max(jnp.where(i3 == f1, neg_inf, ch3), axis=1, keepdims=True)
    gs = (m1 + m2).reshape(N_GROUPS, tm)

    gi = lax.broadcasted_iota(I32, gs.shape, 0)
    gkeep = jnp.zeros(gs.shape, F32)
    for _ in range(TOPK_GROUPS):
        m = jnp.max(gs, axis=0, keepdims=True)
        f = _first_index(gs == m, gi, N_GROUPS, 0)
        sel = gi == f
        gkeep = jnp.where(sel, 1.0, gkeep)
        gs = jnp.where(sel, neg_inf, gs)
    ekeep = jnp.broadcast_to(gkeep.reshape(N_GROUPS, 1, tm), (N_GROUPS, GROUP_SIZE, tm)).reshape(N_EXPERTS, tm)
    masked = jnp.where(ekeep > 0.0, choice, neg_inf)

    ei = lax.broadcasted_iota(I32, masked.shape, 0)
    picked = jnp.zeros(masked.shape, F32)
    idxs, ws = [], []
    for _ in range(TOP_K):
        m = jnp.max(masked, axis=0, keepdims=True)
        f = _first_index(masked == m, ei, N_EXPERTS, 0)
        sel = ei == f
        idxs.append(f)
        ws.append(jnp.sum(jnp.where(sel, scores, 0.0), axis=0, keepdims=True))
        picked = jnp.where(sel, 1.0, picked)
        masked = jnp.where(sel, neg_inf, masked)
    wsum = ws[0]
    for wk in ws[1:]:
        wsum = wsum + wk
    denom = wsum + 1e-20
    for kk in range(TOP_K):
        idx_ref[kk:kk + 1, :] = idxs[kk]
    wrep = jnp.concatenate([jnp.broadcast_to(ws[kk] / denom * ROUTED_SCALE, (SC_LANES, tm)) for kk in range(TOP_K)],
                           axis=0)
    wts_ref[...] = wrep.T

    @pl.when(i == 0)
    def _():
        base_ref[...] = jnp.zeros(base_ref.shape, F32)

    si = lax.broadcasted_iota(I32, (tm, tm), 0)
    ti = lax.broadcasted_iota(I32, (tm, tm), 1)
    earlier = jnp.where(si < ti, 1.0, 0.0).astype(BF16)
    before = jnp.dot(picked.astype(BF16), earlier, preferred_element_type=F32) + base_ref[...]
    for kk in range(TOP_K):
        rank_ref[kk:kk + 1, :] = jnp.sum(jnp.where(ei == idxs[kk], before, 0.0), axis=0, keepdims=True).astype(I32)
    base_ref[...] = base_ref[...] + jnp.sum(picked, axis=1, keepdims=True)
    cnt_ref[...] = base_ref[...]


def _outproj_router(yconv, yattn, xf, ada3, g2, wout_b, wrt_b, rbias, seq):
    t, d = xf.shape
    tm = min(512, seq)
    tiles_per_seq = seq // tm
    full = lambda shape: pl.BlockSpec(shape, lambda i: (0,) * len(shape))
    row = lambda w: pl.BlockSpec((tm, w), lambda i: (i, 0))
    col = pl.BlockSpec((TOP_K, tm), lambda i: (0, i))
    return pl.pallas_call(
        _outproj_router_kernel,
        grid=(t // tm,),
        in_specs=[row(D_CONV), row(D_ATTN), row(d),
                  pl.BlockSpec((1, 6, d), lambda i: (i // tiles_per_seq, 0, 0)),
                  full((1, d)), full((D_CONV + D_ATTN, d)), full((N_EXPERTS, d)), full((N_EXPERTS, 1))],
        out_specs=[row(d), row(d // 2), col, row(TOP_K * SC_LANES), col, full((N_EXPERTS, 1))],
        out_shape=[jax.ShapeDtypeStruct((t, d), F32), jax.ShapeDtypeStruct((t, d // 2), I32),
                   jax.ShapeDtypeStruct((TOP_K, t), I32), jax.ShapeDtypeStruct((t, TOP_K * SC_LANES), F32),
                   jax.ShapeDtypeStruct((TOP_K, t), I32), jax.ShapeDtypeStruct((N_EXPERTS, 1), F32)],
        scratch_shapes=[pltpu.VMEM((N_EXPERTS, 1), F32)],
        compiler_params=_cparams(("arbitrary",)),
        name="outproj_router",
    )(yconv, yattn, xf, ada3, g2, wout_b, wrt_b, rbias)


def _dest_kernel(idx_ref, rank_ref, ps_ref, dest_ref):
    tm = idx_ref.shape[1]
    ei = lax.broadcasted_iota(I32, (N_EXPERTS, tm), 0)
    ps = ps_ref[...]
    for kk in range(TOP_K):
        start = jnp.sum(jnp.where(ei == idx_ref[kk:kk + 1, :], ps, 0.0), axis=0, keepdims=True)
        dest_ref[kk:kk + 1, :] = start.astype(I32) + rank_ref[kk:kk + 1, :]


def _dest(idx_t, rank_t, pad_start_f):
    t = idx_t.shape[1]
    tm = min(2048, t)
    col = pl.BlockSpec((TOP_K, tm), lambda i: (0, i))
    return pl.pallas_call(
        _dest_kernel,
        grid=(t // tm,),
        in_specs=[col, col, pl.BlockSpec((N_EXPERTS, 1), lambda i: (0, 0))],
        out_specs=col,
        out_shape=jax.ShapeDtypeStruct((TOP_K, t), I32),
        compiler_params=_cparams(("arbitrary",)),
        name="dest",
    )(idx_t, rank_t, pad_start_f)


def _sc_dispatch(hp, dest_ck, n_rows):
    t, w = hp.shape
    nk = dest_ck.shape[1]
    per_worker = t // SC_CHUNK // SC_WORKERS
    mesh = plsc.VectorSubcoreMesh(core_axis_name="c", subcore_axis_name="s")

    @functools.partial(
        pl.kernel, mesh=mesh,
        out_type=jax.ShapeDtypeStruct((n_rows, w), hp.dtype),
        scratch_types=[pltpu.VMEM((nk, SC_CHUNK), I32), pltpu.VMEM((SC_CHUNK, w), hp.dtype), pltpu.SemaphoreType.DMA],
    )
    def k(hp_hbm, dest_hbm, xs_hbm, idx_v, rows_v, sem):
        wid = lax.axis_index("s") * SC_CORES + lax.axis_index("c")

        @pl.loop(0, per_worker)
        def _(j):
            c = wid * per_worker + j
            pltpu.sync_copy(dest_hbm.at[c], idx_v)
            pltpu.sync_copy(hp_hbm.at[pl.ds(c * SC_CHUNK, SC_CHUNK)], rows_v)
            for kk in range(nk):
                pltpu.async_copy(rows_v, xs_hbm.at[idx_v.at[kk]], sem)
            for kk in range(nk):
                pltpu.make_async_copy(rows_v, xs_hbm.at[idx_v.at[kk]], sem).wait()

    return k(hp, dest_ck)


def _sc_combine(src, dest_ck, wrep, n_tokens):
    half = src.shape[1]
    lanes = wrep.shape[1] // TOP_K
    ctok = COMBINE_SLOTS // TOP_K
    per_worker = n_tokens // ctok // SC_WORKERS
    steps_per_chunk = SC_CHUNK // ctok
    chunks_per_worker = per_worker // steps_per_chunk
    mesh = plsc.VectorSubcoreMesh(core_axis_name="c", subcore_axis_name="s")

    @functools.partial(
        pl.kernel, mesh=mesh,
        out_type=jax.ShapeDtypeStruct((n_tokens, 2 * half), F32),
        scratch_types=[pltpu.VMEM((chunks_per_worker, TOP_K, SC_CHUNK), I32),
                       pltpu.VMEM((2, COMBINE_SLOTS, half), I32), pltpu.VMEM((2, ctok, TOP_K * lanes), F32),
                       pltpu.VMEM((2, ctok, 2 * half), F32),
                       pltpu.SemaphoreType.DMA((2,)), pltpu.SemaphoreType.DMA((2,)), pltpu.SemaphoreType.DMA((2,))],
        compiler_params=pltpu.CompilerParams(needs_layout_passes=False),
    )
    def k(src_hbm, idx_hbm, w_hbm, y_hbm, idx_v, rows_v, w_v, out_v, sem_g, sem_w, sem_o):
        wid = lax.axis_index("s") * SC_CORES + lax.axis_index("c")
        base = wid * per_worker
        pltpu.sync_copy(idx_hbm.at[pl.ds(wid * chunks_per_worker, chunks_per_worker)], idx_v)

        class _Gather:
            def __init__(self, c, slot):
                chunk = c // steps_per_chunk
                first = (c % steps_per_chunk) * ctok
                self.copies = [
                    pltpu.make_async_copy(src_hbm.at[idx_v.at[chunk, kk, pl.ds(first, ctok)]],
                                          rows_v.at[slot, pl.ds(kk * ctok, ctok)], sem_g.at[slot])
                    for kk in range(TOP_K)]

            def start(self):
                for cp in self.copies:
                    cp.start()

            def wait(self):
                for cp in self.copies:
                    cp.wait()

        gather = _Gather

        def wload(c, slot):
            return pltpu.make_async_copy(w_hbm.at[pl.ds((base + c) * ctok, ctok)], w_v.at[slot], sem_w.at[slot])

        def store(c, slot):
            return pltpu.make_async_copy(out_v.at[slot], y_hbm.at[pl.ds((base + c) * ctok, ctok)], sem_o.at[slot])

        gather(0, 0).start()
        wload(0, 0).start()

        @pl.loop(0, per_worker, step=2)
        def _(c0):
            for slot in range(2):
                c = c0 + slot
                gather(c, slot).wait()
                wload(c, slot).wait()

                @pl.when(c + 1 < per_worker)
                def _():
                    gather(c + 1, 1 - slot).start()
                    wload(c + 1, 1 - slot).start()

                @pl.when(c >= 2)
                def _():
                    store(c - 2, slot).wait()

                @pl.loop(0, ctok)
                def _(tt):
                    ws = [w_v[slot, tt, pl.ds(kk * lanes, lanes)] for kk in range(TOP_K)]

                    @plsc.parallel_loop(0, half // lanes, unroll=2)
                    def _(v):
                        lo = jnp.zeros((lanes,), F32)
                        hi = jnp.zeros((lanes,), F32)
                        for kk in range(TOP_K):
                            x = rows_v[slot, kk * ctok + tt, pl.ds(v * lanes, lanes)]
                            x_lo, x_hi = plsc.unpack(plsc.bitcast(x, BF16), format=plsc.PackFormat.INTERLEAVED)
                            lo = lo + x_lo * ws[kk]
                            hi = hi + x_hi * ws[kk]
                        out_v[slot, tt, pl.ds(v * lanes, lanes)] = lo
                        out_v[slot, tt, pl.ds(half + v * lanes, lanes)] = hi

                store(c, slot).start()

        store(per_worker - 2, 0).wait()
        store(per_worker - 1, 1).wait()

    return k(src, dest_ck, wrep)


def _experts_kernel(bstart_ref, xs_hbm, wg_ref, wu_ref, wd_ref, out_hbm,
                    wgu_b, wd_b, xbuf, obuf, sem_in, sem_out):
    step = pl.program_id(0)
    n_used = bstart_ref[N_EXPERTS]
    nbuf = EXPERT_BUFS

    def in_copy(g, slot):
        return pltpu.make_async_copy(xs_hbm.at[pl.ds(g * ROW_TILE, ROW_TILE)], xbuf.at[slot], sem_in.at[slot])

    def out_copy(g, slot):
        return pltpu.make_async_copy(obuf.at[slot], out_hbm.at[pl.ds(g * ROW_TILE, ROW_TILE)], sem_out.at[slot])

    def fetch(g):
        @pl.when(g < n_used)
        def _():
            in_copy(g, lax.rem(g, nbuf)).start()

    def run(blocks):
        slots = [lax.rem(g, nbuf) for g in blocks]
        for g, slot in zip(blocks, slots):
            in_copy(g, slot).wait()
        fetch(blocks[0] + nbuf - 1)
        for g, slot in zip(blocks, slots):
            @pl.when(g >= nbuf)
            def _():
                out_copy(g - nbuf, slot).wait()
        xs_rows = [_unpack_bf16_pairs(xbuf[slot]).astype(BF16) for slot in slots]
        xb = xs_rows[0] if len(slots) == 1 else jnp.concatenate(xs_rows, axis=0)
        gu = jnp.dot(xb, wgu_b[...], preferred_element_type=F32)
        act = (jax.nn.silu(gu[:, 0:D_EXPERT]) * gu[:, D_EXPERT:2 * D_EXPERT]).astype(BF16)
        out = _pack_bf16_pairs(jnp.dot(act, wd_b[...], preferred_element_type=F32))
        for n, slot in enumerate(slots):
            obuf[slot] = out[n * ROW_TILE:(n + 1) * ROW_TILE]
        for g, slot in zip(blocks, slots):
            out_copy(g, slot).start()
        for g in blocks[:-1]:
            fetch(g + nbuf)

    @pl.when(step == 0)
    def _():
        for g in range(nbuf - 1):
            @pl.when(g < n_used)
            def _():
                in_copy(g, g).start()

    for sub in range(EXPERTS_PER_STEP):
        e = step * EXPERTS_PER_STEP + sub
        b0 = bstart_ref[e]
        b1 = bstart_ref[e + 1]

        @pl.when(b1 > b0)
        def _():
            wgu_b[:, 0:D_EXPERT] = wg_ref[sub].astype(BF16)
            wgu_b[:, D_EXPERT:2 * D_EXPERT] = wu_ref[sub].astype(BF16)
            wd_b[...] = wd_ref[sub].astype(BF16)
            group = EXPERT_GROUP

            def full_group(jj, carry):
                g = b0 + group * jj
                run([g + n for n in range(group)])
                return carry

            n_full = (b1 - b0) // group
            lax.fori_loop(0, n_full, full_group, 0)
            rest = (b1 - b0) - n_full * group
            tail = b0 + n_full * group
            for size in range(1, group):
                @pl.when(rest == size)
                def _():
                    run([tail + n for n in range(size)])

    @pl.when(step == pl.num_programs(0) - 1)
    def _():
        for back in range(1, nbuf + 1):
            @pl.when(n_used >= back)
            def _():
                g = n_used - back
                out_copy(g, lax.rem(g, nbuf)).wait()


def _experts(xs, bstart, w_gate, w_up, w_down):
    n_rows, half = xs.shape
    d = 2 * half
    weights = lambda shape: pl.BlockSpec((EXPERTS_PER_STEP,) + shape, lambda s, bs: (s, 0, 0))
    return pl.pallas_call(
        _experts_kernel,
        grid_spec=pltpu.PrefetchScalarGridSpec(
            num_scalar_prefetch=1,
            grid=(N_EXPERTS // EXPERTS_PER_STEP,),
            in_specs=[pl.BlockSpec(memory_space=pl.ANY),
                      weights((d, D_EXPERT)), weights((d, D_EXPERT)), weights((D_EXPERT, d))],
            out_specs=pl.BlockSpec(memory_space=pl.ANY),
            scratch_shapes=[pltpu.VMEM((d, 2 * D_EXPERT), BF16), pltpu.VMEM((D_EXPERT, d), BF16),
                            pltpu.VMEM((EXPERT_BUFS, ROW_TILE, half), I32),
                            pltpu.VMEM((EXPERT_BUFS, ROW_TILE, half), I32),
                            pltpu.SemaphoreType.DMA((EXPERT_BUFS,)), pltpu.SemaphoreType.DMA((EXPERT_BUFS,))],
        ),
        out_shape=jax.ShapeDtypeStruct((n_rows, half), I32),
        compiler_params=_cparams(("arbitrary",)),
        name="experts",
    )(bstart, xs, w_gate, w_up, w_down)


def _final_kernel(x1_ref, hp_ref, yr_ref, ada_ref, wgs_ref, wus_ref, wds_ref, gf_ref, o_ref):
    gt2 = ada_ref[0][5:6]
    hb = _unpack_bf16_pairs(hp_ref[...]).astype(BF16)
    g = jnp.dot(hb, wgs_ref[...], preferred_element_type=F32)
    u = jnp.dot(hb, wus_ref[...], preferred_element_type=F32)
    y = jnp.dot((jax.nn.silu(g) * u).astype(BF16), wds_ref[...], preferred_element_type=F32)
    x2 = x1_ref[...] + gt2 * (yr_ref[...] + y)
    o_ref[...] = _rms(x2, NORM_EPS) * gf_ref[...]


def _final(x1, hp, y_routed, ada3, wgs_b, wus_b, wds_b, g_final, seq):
    t, d = x1.shape
    tm = min(512, seq)
    tiles_per_seq = seq // tm
    full = lambda shape: pl.BlockSpec(shape, lambda i: (0,) * len(shape))
    row = lambda w: pl.BlockSpec((tm, w), lambda i: (i, 0))
    return pl.pallas_call(
        _final_kernel,
        grid=(t // tm,),
        in_specs=[row(d), row(d // 2), row(d),
                  pl.BlockSpec((1, 6, d), lambda i: (i // tiles_per_seq, 0, 0)),
                  full((d, D_EXPERT)), full((d, D_EXPERT)), full((D_EXPERT, d)), full((1, d))],
        out_specs=row(d),
        out_shape=jax.ShapeDtypeStruct((t, d), F32),
        compiler_params=_cparams(("arbitrary",)),
        name="final",
    )(x1, hp, y_routed, ada3, wgs_b, wus_b, wds_b, g_final)


def kernel(x, c, positions, w_ada, b_ada, g_norm1, w_in, conv_w, g_conv_out, lam_q1, lam_k1, lam_q2, lam_k2, g_subln, w_out, g_norm2, w_router, router_bias, w_gate_e, w_up_e, w_down_e, w_gate_s, w_up_s, w_down_s, g_final):
    bsz, seq, d = x.shape
    t = bsz * seq
    xf = x.reshape(t, d)
    pos = positions.reshape(1, t)
    invf = (ROPE_THETA ** (-jnp.arange(0, ROT_DIM, 2, dtype=F32) / ROT_DIM)).reshape(ROT_DIM // 2, 1)

    ada3 = _ada(c, w_ada[0], b_ada[0]).reshape(bsz, 6, d)
    yconv, qt, k, vt = _inproj(xf, pos, ada3, g_norm1[0].reshape(1, d), w_in[0].astype(BF16), conv_w[0],
                             g_conv_out[0].reshape(1, D_CONV), invf, seq)
    lamp = jnp.stack([lam_q1[0], lam_k1[0], lam_q2[0], lam_k2[0]]).astype(F32)
    yattn = _attention(qt, k, vt, lamp, g_subln[0].reshape(V_DIM, 1), bsz, seq)

    x1, hp, idx_t, wrep, rank_t, counts_f = _outproj_router(
        yconv, yattn, xf, ada3, g_norm2[0].reshape(1, d), w_out[0].astype(BF16),
        w_router[0].T.astype(BF16), router_bias[0].reshape(N_EXPERTS, 1), seq)

    counts = counts_f[:, 0].astype(I32)
    padded = ((counts + ROW_TILE - 1) // ROW_TILE) * ROW_TILE
    pad_end = jnp.cumsum(padded)
    pad_start = pad_end - padded
    nb = (t * TOP_K + N_EXPERTS * (ROW_TILE - 1)) // ROW_TILE
    bstart = (jnp.concatenate([pad_start, pad_end[-1:]]) // ROW_TILE).astype(I32)

    dest_t = _dest(idx_t, rank_t, pad_start.astype(F32).reshape(N_EXPERTS, 1))
    n_chunks = t // SC_CHUNK
    dest_ck = dest_t.reshape(TOP_K, n_chunks, SC_CHUNK).transpose(1, 0, 2)

    xs = _sc_dispatch(hp, dest_ck, nb * ROW_TILE)
    outs = _experts(xs, bstart, w_gate_e[0], w_up_e[0], w_down_e[0])
    y_routed = _sc_combine(outs, dest_ck, wrep, t)

    out = _final(x1, hp, y_routed, ada3,
                 w_gate_s[0].astype(BF16), w_up_s[0].astype(BF16), w_down_s[0].astype(BF16),
                 g_final.reshape(1, d), seq)
    return out.reshape(bsz, seq, d)
```

```python
import functools
import math

import jax
import jax.numpy as jnp
from jax import lax
from jax.experimental import pallas as pl
from jax.experimental.pallas import tpu as pltpu
from jax.experimental.pallas import tpu_sc as plsc

F32 = jnp.float32
BF16 = jnp.bfloat16
I32 = jnp.int32

D_CONV = 512
CONV_WIDTH = 3
N_HEADS = 4
HEAD_DIM = 64
V_DIM = 2 * HEAD_DIM
D_ATTN = N_HEADS * V_DIM
D_QK = N_HEADS * 2 * HEAD_DIM
ROT_DIM = HEAD_DIM // 4
ROPE_THETA = 500000.0
N_EXPERTS = 256
TOP_K = 8
N_GROUPS = 8
GROUP_SIZE = N_EXPERTS // N_GROUPS
TOPK_GROUPS = 4
D_EXPERT = 256
ROUTED_SCALE = 2.5
NORM_EPS = 1e-6
SUBLN_EPS = 1e-5
LAMBDA_INIT = 0.8 - 0.6 * math.exp(-0.3 * 0)

LANES = 128
SC_CORES = 2
SC_SUBCORES = 16
SC_WORKERS = SC_CORES * SC_SUBCORES
SC_CHUNK = 128
COMBINE_SLOTS = 64
COMBINE_BUFS = 2
SC_LANES = 16

ROW_TILE = 128
EXPERT_GROUP = 4
EXPERTS_PER_STEP = 2
EXPERT_BUFS = 12
ATTN_ONES_ROWS = 16
NEG_BIG = -1e30
VMEM_LIMIT = 56 * 1024 * 1024


def _cparams(sem):
    return pltpu.CompilerParams(dimension_semantics=sem, vmem_limit_bytes=VMEM_LIMIT)


def _rms(x, eps):
    return x * lax.rsqrt(jnp.mean(x * x, axis=-1, keepdims=True) + eps)


def _pack_bf16_pairs(x):
    n = x.shape[1] // 2
    bits = lax.bitcast_convert_type(x.astype(BF16).astype(F32), I32)
    lo = lax.shift_right_logical(bits[:, :n], 16)
    return lo | bits[:, n:]


def _unpack_bf16_pairs(p):
    lo = lax.bitcast_convert_type(lax.shift_left(p, 16), F32)
    hi = lax.bitcast_convert_type(p & jnp.int32(-65536), F32)
    return jnp.concatenate([lo, hi], axis=1)


def _ada_kernel(c_ref, w_ref, b_ref, o_ref):
    ca = jax.nn.silu(c_ref[...])
    o_ref[...] = jnp.dot(ca.astype(BF16), w_ref[...].astype(BF16), preferred_element_type=F32) + b_ref[...]


def _ada(c, w_ada, b_ada):
    bsz, d = c.shape
    n = w_ada.shape[1]
    tn = n // 4
    return pl.pallas_call(
        _ada_kernel,
        grid=(n // tn,),
        in_specs=[pl.BlockSpec((bsz, d), lambda j: (0, 0)),
                  pl.BlockSpec((d, tn), lambda j: (0, j)),
                  pl.BlockSpec((1, tn), lambda j: (0, j))],
        out_specs=pl.BlockSpec((bsz, tn), lambda j: (0, j)),
        out_shape=jax.ShapeDtypeStruct((bsz, n), F32),
        compiler_params=_cparams(("arbitrary",)),
        name="ada",
    )(c, w_ada, b_ada.reshape(1, n))


def _inproj_kernel(tiles_per_seq, x_ref, pos_ref, ada_ref, g1_ref, win_ref, convw_ref, gconv_ref, invf_ref,
                   yconv_ref, qt_ref, k_ref, vt_ref, ubuf):
    i = pl.program_id(0)
    tm = x_ref.shape[0]
    x = x_ref[...]
    ada = ada_ref[0]
    sh1, sc1 = ada[0:1], ada[1:2]
    hn = _rms(x, NORM_EPS) * g1_ref[...] * (1.0 + sc1) + sh1
    hb = hn.astype(BF16)

    pc = jnp.dot(hb, win_ref[:, 0:3 * D_CONV], preferred_element_type=F32)
    u = pc[:, 2 * D_CONV:3 * D_CONV] * pc[:, 0:D_CONV]
    first = (i % tiles_per_seq) == 0

    @pl.when(first)
    def _():
        ubuf[0:8, :] = jnp.zeros((8, D_CONV), F32)

    @pl.when(jnp.logical_not(first))
    def _():
        ubuf[0:8, :] = ubuf[tm:tm + 8, :]

    ubuf[8:8 + tm, :] = u
    u1 = ubuf[7:7 + tm, :]
    u2 = ubuf[6:6 + tm, :]
    cw = convw_ref[...]
    yc = pc[:, D_CONV:2 * D_CONV] * (cw[0:1] * u2 + cw[1:2] * u1 + cw[2:3] * u)
    yconv_ref[...] = (_rms(yc, NORM_EPS) * gconv_ref[...]).astype(BF16)

    half = ROT_DIM // 2
    ang = invf_ref[...] * pos_ref[...].astype(F32)
    cos_ft = jnp.cos(ang)
    sin_ft = jnp.sin(ang)
    zero_h = jnp.zeros((half, tm), F32)
    zero_r = jnp.zeros((HEAD_DIM - ROT_DIM, tm), F32)
    lay = lambda a, b, r: jnp.concatenate([a, b, r, a, b, r], axis=0).T
    cos = lay(cos_ft, cos_ft, zero_r + 1.0)
    s_up = lay(-sin_ft, zero_h, zero_r)
    s_dn = lay(zero_h, sin_ft, zero_r)

    def rope(t):
        return t * cos + pltpu.roll(t, LANES - half, axis=1) * s_up + pltpu.roll(t, half, axis=1) * s_dn

    pq = jnp.dot(hb, win_ref[:, 3 * D_CONV:3 * D_CONV + D_QK], preferred_element_type=F32)
    pk = jnp.dot(hb, win_ref[:, 3 * D_CONV + D_QK:3 * D_CONV + 2 * D_QK], preferred_element_type=F32)
    scale = HEAD_DIM ** -0.5 * math.log2(math.e)
    qs = []
    for h in range(N_HEADS):
        sl = slice(h * LANES, (h + 1) * LANES)
        qs.append(rope(pq[:, sl]) * scale)
        k_ref[:, sl] = rope(pk[:, sl]).astype(BF16)
    qt_ref[0] = jnp.concatenate(qs, axis=1).T.astype(BF16)
    pv = jnp.dot(hb, win_ref[:, 3 * D_CONV + 2 * D_QK:], preferred_element_type=F32)
    vt_ref[0] = pv.T.astype(BF16)


def _inproj(xf, pos, ada3, g1, win_b, conv_w, g_conv, invf, seq):
    t, d = xf.shape
    tm = min(1024, seq)
    tiles_per_seq = seq // tm
    n_in = win_b.shape[1]
    full = lambda shape: pl.BlockSpec(shape, lambda i: (0,) * len(shape))
    row = lambda w: pl.BlockSpec((tm, w), lambda i: (i, 0))
    colmajor = lambda w: pl.BlockSpec((1, w, tm), lambda i: (i // tiles_per_seq, 0, i % tiles_per_seq))
    return pl.pallas_call(
        functools.partial(_inproj_kernel, tiles_per_seq),
        grid=(t // tm,),
        in_specs=[row(d), pl.BlockSpec((1, tm), lambda i: (0, i)),
                  pl.BlockSpec((1, 6, d), lambda i: (i // tiles_per_seq, 0, 0)),
                  full((1, d)), full((d, n_in)), full((CONV_WIDTH, D_CONV)), full((1, D_CONV)),
                  full((ROT_DIM // 2, 1))],
        out_specs=[row(D_CONV), colmajor(D_QK), row(D_QK), colmajor(D_ATTN)],
        out_shape=[jax.ShapeDtypeStruct((t, D_CONV), BF16), jax.ShapeDtypeStruct((t // seq, D_QK, seq), BF16),
                   jax.ShapeDtypeStruct((t, D_QK), BF16), jax.ShapeDtypeStruct((t // seq, D_ATTN, seq), BF16)],
        scratch_shapes=[pltpu.VMEM((tm + 8, D_CONV), F32)],
        compiler_params=_cparams(("arbitrary",)),
        name="inproj",
    )(xf, pos, ada3, g1, win_b, conv_w, g_conv, invf)


def _attn_steps(nq):
    off = [(j, i) for j in range(nq) for i in range(j + 1, nq)]
    diag = [(i, i) for i in range(nq)]
    return off, diag


def _attn_kernel(steps_ref, qt_ref, k_ref, vt_ref, lamp_ref, gs_ref, o_ref,
                 qq_ref, m_ref, acc_ref, s0_ref, s1_ref):
    nq = m_ref.shape[0]
    tq = s0_ref.shape[0]
    tk = tq
    off, diag = _attn_steps(nq)
    n_steps = len(off) + len(diag)
    bufs = (s0_ref, s1_ref)

    feat = lax.broadcasted_iota(I32, (LANES, tq), 0)
    for i in range(nq):
        qt = qt_ref[0, :, i * tq:(i + 1) * tq]
        zero = jnp.zeros_like(qt)
        qq_ref[:, 2 * i * tq:(2 * i + 1) * tq] = jnp.where(feat < HEAD_DIM, qt, zero)
        qq_ref[:, (2 * i + 1) * tq:(2 * i + 2) * tq] = jnp.where(feat >= HEAD_DIM, qt, zero)
    m_ref[...] = jnp.full(m_ref.shape, NEG_BIG, F32)
    acc_ref[...] = jnp.zeros(acc_ref.shape, F32)
    ones_rows = jnp.ones((ATTN_ONES_ROWS, tk), BF16)

    def scores(j, i, dst):
        kstart = pl.multiple_of(j * tk, tk)
        qstart = pl.multiple_of(i * (2 * tq), 2 * tq)
        dst[...] = jnp.dot(k_ref[pl.ds(kstart, tk), :], qq_ref[:, pl.ds(qstart, 2 * tq)],
                           preferred_element_type=F32)

    def consume(src, j, i, masked):
        kstart = pl.multiple_of(j * tk, tk)
        vtb = jnp.concatenate([vt_ref[0, :, pl.ds(kstart, tk)], ones_rows], axis=0)
        s = src[...]
        if masked:
            key = lax.broadcasted_iota(I32, s.shape, 0)
            col = lax.broadcasted_iota(I32, s.shape, 1)
            qpos = jnp.where(col >= tq, col - tq, col)
            s = jnp.where(key <= qpos, s, NEG_BIG)
        m_prev = m_ref[i]
        m_new = jnp.maximum(m_prev, jnp.max(s, axis=0, keepdims=True))
        alpha = jnp.exp2(m_prev - m_new)
        p = jnp.exp2(s - m_new)
        acc_ref[i] = alpha * acc_ref[i] + jnp.dot(vtb, p.astype(BF16), preferred_element_type=F32)
        m_ref[i] = m_new

    def at(n):
        return steps_ref[0, n], steps_ref[1, n]

    scores(*at(0), s0_ref)
    n_pairs = len(off) // 2

    def pair(pp, carry):
        n = 2 * pp
        scores(*at(n + 1), s1_ref)
        consume(s0_ref, *at(n), False)
        scores(*at(n + 2), s0_ref)
        consume(s1_ref, *at(n + 1), False)
        return carry

    lax.fori_loop(0, n_pairs, pair, 0)

    static_steps = (off + diag)[2 * n_pairs:]
    for n, (j, i) in enumerate(static_steps, start=2 * n_pairs):
        if n + 1 < n_steps:
            scores(*static_steps[n + 1 - 2 * n_pairs], bufs[(n + 1) % 2])
        consume(bufs[n % 2], j, i, masked=(j == i))

    lamp = lamp_ref[...]
    lam = (jnp.exp(jnp.sum(lamp[0:1] * lamp[1:2], axis=1, keepdims=True))
           - jnp.exp(jnp.sum(lamp[2:3] * lamp[3:4], axis=1, keepdims=True)) + LAMBDA_INIT)
    for i in range(nq):
        o = acc_ref[i, 0:V_DIM] / acc_ref[i, V_DIM:V_DIM + 1]
        od = o[:, 0:tq] - lam * o[:, tq:2 * tq]
        y = od * lax.rsqrt(jnp.mean(od * od, axis=0, keepdims=True) + SUBLN_EPS) * gs_ref[...] * (1.0 - LAMBDA_INIT)
        o_ref[i * tq:(i + 1) * tq, :] = y.T.astype(BF16)


def _attention(qt, k, vt, lamp, g_subln, bsz, seq):
    t = k.shape[0]
    tq = min(512, seq)
    nq = seq // tq
    off, diag = _attn_steps(nq)
    steps = jnp.asarray(list(zip(*(off + diag))), I32)
    return pl.pallas_call(
        _attn_kernel,
        grid_spec=pltpu.PrefetchScalarGridSpec(
            num_scalar_prefetch=1,
            grid=(bsz, N_HEADS),
            in_specs=[pl.BlockSpec((1, LANES, seq), lambda b, h, st: (b, h, 0)),
                      pl.BlockSpec((seq, LANES), lambda b, h, st: (b, h)),
                      pl.BlockSpec((1, V_DIM, seq), lambda b, h, st: (b, h, 0)),
                      pl.BlockSpec((4, HEAD_DIM), lambda b, h, st: (0, 0)),
                      pl.BlockSpec((V_DIM, 1), lambda b, h, st: (0, 0))],
            out_specs=pl.BlockSpec((seq, LANES), lambda b, h, st: (b, h)),
            scratch_shapes=[pltpu.VMEM((LANES, 2 * seq), BF16), pltpu.VMEM((nq, 1, 2 * tq), F32),
                            pltpu.VMEM((nq, V_DIM + ATTN_ONES_ROWS, 2 * tq), F32),
                            pltpu.VMEM((tq, 2 * tq), F32), pltpu.VMEM((tq, 2 * tq), F32)],
        ),
        out_shape=jax.ShapeDtypeStruct((t, D_ATTN), BF16),
        compiler_params=_cparams(("arbitrary", "arbitrary")),
        name="attention",
    )(steps, qt, k, vt, lamp, g_subln)


def _first_index(hit, idx, size, axis):
    return jnp.min(jnp.where(hit, idx, size), axis=axis, keepdims=True)


def _outproj_router_kernel(yc_ref, ya_ref, x_ref, ada_ref, g2_ref, wout_ref, wrt_ref, rb_ref,
                           x1_ref, hp_ref, idx_ref, wts_ref, rank_ref, cnt_ref, base_ref):
    i = pl.program_id(0)
    tm = x_ref.shape[0]
    ada = ada_ref[0]
    gt1, sh2, sc2 = ada[2:3], ada[3:4], ada[4:5]
    mix = (jnp.dot(yc_ref[...], wout_ref[0:D_CONV, :], preferred_element_type=F32)
           + jnp.dot(ya_ref[...], wout_ref[D_CONV:, :], preferred_element_type=F32))
    x1 = x_ref[...] + gt1 * mix
    x1_ref[...] = x1
    hn2 = _rms(x1, NORM_EPS) * g2_ref[...] * (1.0 + sc2) + sh2
    hp_ref[...] = _pack_bf16_pairs(hn2)
    hb = hn2.astype(BF16)

    logits = lax.dot_general(wrt_ref[...], hb, (((1,), (1,)), ((), ())), preferred_element_type=F32)
    scores = jax.nn.sigmoid(logits)
    choice = scores + rb_ref[...]
    neg_inf = jnp.float32(-jnp.inf)

    ch3 = choice.reshape(N_GROUPS, GROUP_SIZE, tm)
    i3 = lax.broadcasted_iota(I32, ch3.shape, 1)
    m1 = jnp.max(ch3, axis=1, keepdims=True)
    f1 = _first_index(ch3 == m1, i3, GROUP_SIZE, 1)
    m2 = jnp.max(jnp.where(i3 == f1, neg_inf, ch3), axis=1, keepdims=True)
    gs = (m1 + m2).reshape(N_GROUPS, tm)

    gi = lax.broadcasted_iota(I32, gs.shape, 0)
    gkeep = jnp.zeros(gs.shape, F32)
    for _ in range(TOPK_GROUPS):
        m = jnp.max(gs, axis=0, keepdims=True)
        f = _first_index(gs == m, gi, N_GROUPS, 0)
        sel = gi == f
        gkeep = jnp.where(sel, 1.0, gkeep)
        gs = jnp.where(sel, neg_inf, gs)
    ekeep = jnp.broadcast_to(gkeep.reshape(N_GROUPS, 1, tm), (N_GROUPS, GROUP_SIZE, tm)).reshape(N_EXPERTS, tm)
    masked = jnp.where(ekeep > 0.0, choice, neg_inf)

    ei = lax.broadcasted_iota(I32, masked.shape, 0)
    picked = jnp.zeros(masked.shape, F32)
    idxs, ws = [], []
    for _ in range(TOP_K):
        m = jnp.max(masked, axis=0, keepdims=True)
        f = _first_index(masked == m, ei, N_EXPERTS, 0)
        sel = ei == f
        idxs.append(f)
        ws.append(jnp.sum(jnp.where(sel, scores, 0.0), axis=0, keepdims=True))
        picked = jnp.where(sel, 1.0, picked)
        masked = jnp.where(sel, neg_inf, masked)
    wsum = ws[0]
    for wk in ws[1:]:
        wsum = wsum + wk
    denom = wsum + 1e-20
    for kk in range(TOP_K):
        idx_ref[kk:kk + 1, :] = idxs[kk]
    wrep = jnp.concatenate([jnp.broadcast_to(ws[kk] / denom * ROUTED_SCALE, (SC_LANES, tm)) for kk in range(TOP_K)],
                           axis=0)
    wts_ref[...] = wrep.T

    @pl.when(i == 0)
    def _():
        base_ref[...] = jnp.zeros(base_ref.shape, F32)

    si = lax.broadcasted_iota(I32, (tm, tm), 0)
    ti = lax.broadcasted_iota(I32, (tm, tm), 1)
    earlier = jnp.where(si < ti, 1.0, 0.0).astype(BF16)
    before = jnp.dot(picked.astype(BF16), earlier, preferred_element_type=F32) + base_ref[...]
    for kk in range(TOP_K):
        rank_ref[kk:kk + 1, :] = jnp.sum(jnp.where(ei == idxs[kk], before, 0.0), axis=0, keepdims=True).astype(I32)
    base_ref[...] = base_ref[...] + jnp.sum(picked, axis=1, keepdims=True)
    cnt_ref[...] = base_ref[...]


def _outproj_router(yconv, yattn, xf, ada3, g2, wout_b, wrt_b, rbias, seq):
    t, d = xf.shape
    tm = min(512, seq)
    tiles_per_seq = seq // tm
    full = lambda shape: pl.BlockSpec(shape, lambda i: (0,) * len(shape))
    row = lambda w: pl.BlockSpec((tm, w), lambda i: (i, 0))
    col = pl.BlockSpec((TOP_K, tm), lambda i: (0, i))
    return pl.pallas_call(
        _outproj_router_kernel,
        grid=(t // tm,),
        in_specs=[row(D_CONV), row(D_ATTN), row(d),
                  pl.BlockSpec((1, 6, d), lambda i: (i // tiles_per_seq, 0, 0)),
                  full((1, d)), full((D_CONV + D_ATTN, d)), full((N_EXPERTS, d)), full((N_EXPERTS, 1))],
        out_specs=[row(d), row(d // 2), col, row(TOP_K * SC_LANES), col, full((N_EXPERTS, 1))],
        out_shape=[jax.ShapeDtypeStruct((t, d), F32), jax.ShapeDtypeStruct((t, d // 2), I32),
                   jax.ShapeDtypeStruct((TOP_K, t), I32), jax.ShapeDtypeStruct((t, TOP_K * SC_LANES), F32),
                   jax.ShapeDtypeStruct((TOP_K, t), I32), jax.ShapeDtypeStruct((N_EXPERTS, 1), F32)],
        scratch_shapes=[pltpu.VMEM((N_EXPERTS, 1), F32)],
        compiler_params=_cparams(("arbitrary",)),
        name="outproj_router",
    )(yconv, yattn, xf, ada3, g2, wout_b, wrt_b, rbias)


def _dest_kernel(idx_ref, rank_ref, ps_ref, dest_ref):
    tm = idx_ref.shape[1]
    ei = lax.broadcasted_iota(I32, (N_EXPERTS, tm), 0)
    ps = ps_ref[...]
    for kk in range(TOP_K):
        start = jnp.sum(jnp.where(ei == idx_ref[kk:kk + 1, :], ps, 0.0), axis=0, keepdims=True)
        dest_ref[kk:kk + 1, :] = start.astype(I32) + rank_ref[kk:kk + 1, :]


def _dest(idx_t, rank_t, pad_start_f):
    t = idx_t.shape[1]
    tm = min(2048, t)
    col = pl.BlockSpec((TOP_K, tm), lambda i: (0, i))
    return pl.pallas_call(
        _dest_kernel,
        grid=(t // tm,),
        in_specs=[col, col, pl.BlockSpec((N_EXPERTS, 1), lambda i: (0, 0))],
        out_specs=col,
        out_shape=jax.ShapeDtypeStruct((TOP_K, t), I32),
        compiler_params=_cparams(("arbitrary",)),
        name="dest",
    )(idx_t, rank_t, pad_start_f)


def _sc_dispatch(hp, dest_ck, n_rows):
    t, w = hp.shape
    nk = dest_ck.shape[1]
    per_worker = t // SC_CHUNK // SC_WORKERS
    mesh = plsc.VectorSubcoreMesh(core_axis_name="c", subcore_axis_name="s")

    @functools.partial(
        pl.kernel, mesh=mesh,
        out_type=jax.ShapeDtypeStruct((n_rows, w), hp.dtype),
        scratch_types=[pltpu.VMEM((nk, SC_CHUNK), I32), pltpu.VMEM((SC_CHUNK, w), hp.dtype), pltpu.SemaphoreType.DMA],
    )
    def k(hp_hbm, dest_hbm, xs_hbm, idx_v, rows_v, sem):
        wid = lax.axis_index("s") * SC_CORES + lax.axis_index("c")

        @pl.loop(0, per_worker)
        def _(j):
            c = wid * per_worker + j
            pltpu.sync_copy(dest_hbm.at[c], idx_v)
            pltpu.sync_copy(hp_hbm.at[pl.ds(c * SC_CHUNK, SC_CHUNK)], rows_v)
            for kk in range(nk):
                pltpu.async_copy(rows_v, xs_hbm.at[idx_v.at[kk]], sem)
            for kk in range(nk):
                pltpu.make_async_copy(rows_v, xs_hbm.at[idx_v.at[kk]], sem).wait()

    return k(hp, dest_ck)


def _sc_combine(src, dest_ck, wrep, n_tokens):
    half = src.shape[1]
    lanes = wrep.shape[1] // TOP_K
    ctok = COMBINE_SLOTS // TOP_K
    per_worker = n_tokens // ctok // SC_WORKERS
    steps_per_chunk = SC_CHUNK // ctok
    chunks_per_worker = per_worker // steps_per_chunk
    mesh = plsc.VectorSubcoreMesh(core_axis_name="c", subcore_axis_name="s")

    @functools.partial(
        pl.kernel, mesh=mesh,
        out_type=jax.ShapeDtypeStruct((n_tokens, 2 * half), F32),
        scratch_types=[pltpu.VMEM((chunks_per_worker, TOP_K, SC_CHUNK), I32),
                       pltpu.VMEM((COMBINE_BUFS, COMBINE_SLOTS, half), I32),
                       pltpu.VMEM((COMBINE_BUFS, ctok, TOP_K * lanes), F32),
                       pltpu.VMEM((COMBINE_BUFS, ctok, 2 * half), F32),
                       pltpu.SemaphoreType.DMA((COMBINE_BUFS,)), pltpu.SemaphoreType.DMA((COMBINE_BUFS,)),
                       pltpu.SemaphoreType.DMA((COMBINE_BUFS,))],
        compiler_params=pltpu.CompilerParams(needs_layout_passes=False),
    )
    def k(src_hbm, idx_hbm, w_hbm, y_hbm, idx_v, rows_v, w_v, out_v, sem_g, sem_w, sem_o):
        wid = lax.axis_index("s") * SC_CORES + lax.axis_index("c")
        base = wid * per_worker
        pltpu.sync_copy(idx_hbm.at[pl.ds(wid * chunks_per_worker, chunks_per_worker)], idx_v)

        class _Gather:
            def __init__(self, c, slot):
                chunk = c // steps_per_chunk
                first = (c % steps_per_chunk) * ctok
                self.copies = [
                    pltpu.make_async_copy(src_hbm.at[idx_v.at[chunk, kk, pl.ds(first, ctok)]],
                                          rows_v.at[slot, pl.ds(kk * ctok, ctok)], sem_g.at[slot])
                    for kk in range(TOP_K)]

            def start(self):
                for cp in self.copies:
                    cp.start()

            def wait(self):
                for cp in self.copies:
                    cp.wait()

        gather = _Gather

        def wload(c, slot):
            return pltpu.make_async_copy(w_hbm.at[pl.ds((base + c) * ctok, ctok)], w_v.at[slot], sem_w.at[slot])

        def store(c, slot):
            return pltpu.make_async_copy(out_v.at[slot], y_hbm.at[pl.ds((base + c) * ctok, ctok)], sem_o.at[slot])

        nbuf = COMBINE_BUFS
        for c in range(nbuf - 1):
            gather(c, c).start()
            wload(c, c).start()

        @pl.loop(0, pl.cdiv(per_worker, nbuf) * nbuf, step=nbuf)
        def _(c0):
            for slot in range(nbuf):
                c = c0 + slot

                @pl.when(c < per_worker)
                def _():
                    gather(c, slot).wait()
                    wload(c, slot).wait()

                    @pl.when(c + nbuf - 1 < per_worker)
                    def _():
                        gather(c + nbuf - 1, (slot + nbuf - 1) % nbuf).start()
                        wload(c + nbuf - 1, (slot + nbuf - 1) % nbuf).start()

                    @pl.when(c >= nbuf)
                    def _():
                        store(c - nbuf, slot).wait()

                    @pl.loop(0, ctok)
                    def _(tt):
                        ws = [w_v[slot, tt, pl.ds(kk * lanes, lanes)] for kk in range(TOP_K)]

                        @plsc.parallel_loop(0, half // lanes, unroll=2)
                        def _(v):
                            lo = jnp.zeros((lanes,), F32)
                            hi = jnp.zeros((lanes,), F32)
                            for kk in range(TOP_K):
                                x = rows_v[slot, kk * ctok + tt, pl.ds(v * lanes, lanes)]
                                x_lo, x_hi = plsc.unpack(plsc.bitcast(x, BF16), format=plsc.PackFormat.INTERLEAVED)
                                lo = lo + x_lo * ws[kk]
                                hi = hi + x_hi * ws[kk]
                            out_v[slot, tt, pl.ds(v * lanes, lanes)] = lo
                            out_v[slot, tt, pl.ds(half + v * lanes, lanes)] = hi

                    store(c, slot).start()

        for c in range(per_worker - nbuf, per_worker):
            store(c, c % nbuf).wait()

    return k(src, dest_ck, wrep)


def _experts_kernel(bstart_ref, xs_hbm, wg_ref, wu_ref, wd_ref, out_hbm,
                    wgu_b, wd_b, xbuf, obuf, sem_in, sem_out):
    step = pl.program_id(0)
    n_used = bstart_ref[N_EXPERTS]
    nbuf = EXPERT_BUFS

    def in_copy(g, slot):
        return pltpu.make_async_copy(xs_hbm.at[pl.ds(g * ROW_TILE, ROW_TILE)], xbuf.at[slot], sem_in.at[slot])

    def out_copy(g, slot):
        return pltpu.make_async_copy(obuf.at[slot], out_hbm.at[pl.ds(g * ROW_TILE, ROW_TILE)], sem_out.at[slot])

    def fetch(g):
        @pl.when(g < n_used)
        def _():
            in_copy(g, lax.rem(g, nbuf)).start()

    def run(blocks):
        slots = [lax.rem(g, nbuf) for g in blocks]
        for g, slot in zip(blocks, slots):
            in_copy(g, slot).wait()
        fetch(blocks[0] + nbuf - 1)
        for g, slot in zip(blocks, slots):
            @pl.when(g >= nbuf)
            def _():
                out_copy(g - nbuf, slot).wait()
        xs_rows = [_unpack_bf16_pairs(xbuf[slot]).astype(BF16) for slot in slots]
        xb = xs_rows[0] if len(slots) == 1 else jnp.concatenate(xs_rows, axis=0)
        gu = jnp.dot(xb, wgu_b[...], preferred_element_type=F32)
        act = (jax.nn.silu(gu[:, 0:D_EXPERT]) * gu[:, D_EXPERT:2 * D_EXPERT]).astype(BF16)
        out = _pack_bf16_pairs(jnp.dot(act, wd_b[...], preferred_element_type=F32))
        for n, slot in enumerate(slots):
            obuf[slot] = out[n * ROW_TILE:(n + 1) * ROW_TILE]
        for g, slot in zip(blocks, slots):
            out_copy(g, slot).start()
        for g in blocks[:-1]:
            fetch(g + nbuf)

    @pl.when(step == 0)
    def _():
        for g in range(nbuf - 1):
            @pl.when(g < n_used)
            def _():
                in_copy(g, g).start()

    for sub in range(EXPERTS_PER_STEP):
        e = step * EXPERTS_PER_STEP + sub
        b0 = bstart_ref[e]
        b1 = bstart_ref[e + 1]

        @pl.when(b1 > b0)
        def _():
            wgu_b[:, 0:D_EXPERT] = wg_ref[sub].astype(BF16)
            wgu_b[:, D_EXPERT:2 * D_EXPERT] = wu_ref[sub].astype(BF16)
            wd_b[...] = wd_ref[sub].astype(BF16)
            group = EXPERT_GROUP

            def full_group(jj, carry):
                g = b0 + group * jj
                run([g + n for n in range(group)])
                return carry

            n_full = (b1 - b0) // group
            lax.fori_loop(0, n_full, full_group, 0)
            rest = (b1 - b0) - n_full * group
            tail = b0 + n_full * group
            for size in range(1, group):
                @pl.when(rest == size)
                def _():
                    run([tail + n for n in range(size)])

    @pl.when(step == pl.num_programs(0) - 1)
    def _():
        for back in range(1, nbuf + 1):
            @pl.when(n_used >= back)
            def _():
                g = n_used - back
                out_copy(g, lax.rem(g, nbuf)).wait()


def _experts(xs, bstart, w_gate, w_up, w_down):
    n_rows, half = xs.shape
    d = 2 * half
    weights = lambda shape: pl.BlockSpec((EXPERTS_PER_STEP,) + shape, lambda s, bs: (s, 0, 0))
    return pl.pallas_call(
        _experts_kernel,
        grid_spec=pltpu.PrefetchScalarGridSpec(
            num_scalar_prefetch=1,
            grid=(N_EXPERTS // EXPERTS_PER_STEP,),
            in_specs=[pl.BlockSpec(memory_space=pl.ANY),
                      weights((d, D_EXPERT)), weights((d, D_EXPERT)), weights((D_EXPERT, d))],
            out_specs=pl.BlockSpec(memory_space=pl.ANY),
            scratch_shapes=[pltpu.VMEM((d, 2 * D_EXPERT), BF16), pltpu.VMEM((D_EXPERT, d), BF16),
                            pltpu.VMEM((EXPERT_BUFS, ROW_TILE, half), I32),
                            pltpu.VMEM((EXPERT_BUFS, ROW_TILE, half), I32),
                            pltpu.SemaphoreType.DMA((EXPERT_BUFS,)), pltpu.SemaphoreType.DMA((EXPERT_BUFS,))],
        ),
        out_shape=jax.ShapeDtypeStruct((n_rows, half), I32),
        compiler_params=_cparams(("arbitrary",)),
        name="experts",
    )(bstart, xs, w_gate, w_up, w_down)


def _final_kernel(x1_ref, hp_ref, yr_ref, ada_ref, wgs_ref, wus_ref, wds_ref, gf_ref, o_ref):
    gt2 = ada_ref[0][5:6]
    hb = _unpack_bf16_pairs(hp_ref[...]).astype(BF16)
    g = jnp.dot(hb, wgs_ref[...], preferred_element_type=F32)
    u = jnp.dot(hb, wus_ref[...], preferred_element_type=F32)
    y = jnp.dot((jax.nn.silu(g) * u).astype(BF16), wds_ref[...], preferred_element_type=F32)
    x2 = x1_ref[...] + gt2 * (yr_ref[...] + y)
    o_ref[...] = _rms(x2, NORM_EPS) * gf_ref[...]


def _final(x1, hp, y_routed, ada3, wgs_b, wus_b, wds_b, g_final, seq):
    t, d = x1.shape
    tm = min(512, seq)
    tiles_per_seq = seq // tm
    full = lambda shape: pl.BlockSpec(shape, lambda i: (0,) * len(shape))
    row = lambda w: pl.BlockSpec((tm, w), lambda i: (i, 0))
    return pl.pallas_call(
        _final_kernel,
        grid=(t // tm,),
        in_specs=[row(d), row(d // 2), row(d),
                  pl.BlockSpec((1, 6, d), lambda i: (i // tiles_per_seq, 0, 0)),
                  full((d, D_EXPERT)), full((d, D_EXPERT)), full((D_EXPERT, d)), full((1, d))],
        out_specs=row(d),
        out_shape=jax.ShapeDtypeStruct((t, d), F32),
        compiler_params=_cparams(("arbitrary",)),
        name="final",
    )(x1, hp, y_routed, ada3, wgs_b, wus_b, wds_b, g_final)


def kernel(x, c, positions, w_ada, b_ada, g_norm1, w_in, conv_w, g_conv_out, lam_q1, lam_k1, lam_q2, lam_k2, g_subln, w_out, g_norm2, w_router, router_bias, w_gate_e, w_up_e, w_down_e, w_gate_s, w_up_s, w_down_s, g_final):
    bsz, seq, d = x.shape
    t = bsz * seq
    xf = x.reshape(t, d)
    pos = positions.reshape(1, t)
    invf = (ROPE_THETA ** (-jnp.arange(0, ROT_DIM, 2, dtype=F32) / ROT_DIM)).reshape(ROT_DIM // 2, 1)

    ada3 = _ada(c, w_ada[0], b_ada[0]).reshape(bsz, 6, d)
    yconv, qt, k, vt = _inproj(xf, pos, ada3, g_norm1[0].reshape(1, d), w_in[0].astype(BF16), conv_w[0],
                             g_conv_out[0].reshape(1, D_CONV), invf, seq)
    lamp = jnp.stack([lam_q1[0], lam_k1[0], lam_q2[0], lam_k2[0]]).astype(F32)
    yattn = _attention(qt, k, vt, lamp, g_subln[0].reshape(V_DIM, 1), bsz, seq)

    x1, hp, idx_t, wrep, rank_t, counts_f = _outproj_router(
        yconv, yattn, xf, ada3, g_norm2[0].reshape(1, d), w_out[0].astype(BF16),
        w_router[0].T.astype(BF16), router_bias[0].reshape(N_EXPERTS, 1), seq)

    counts = counts_f[:, 0].astype(I32)
    padded = ((counts + ROW_TILE - 1) // ROW_TILE) * ROW_TILE
    pad_end = jnp.cumsum(padded)
    pad_start = pad_end - padded
    nb = (t * TOP_K + N_EXPERTS * (ROW_TILE - 1)) // ROW_TILE
    bstart = (jnp.concatenate([pad_start, pad_end[-1:]]) // ROW_TILE).astype(I32)

    dest_t = _dest(idx_t, rank_t, pad_start.astype(F32).reshape(N_EXPERTS, 1))
    n_chunks = t // SC_CHUNK
    dest_ck = dest_t.reshape(TOP_K, n_chunks, SC_CHUNK).transpose(1, 0, 2)

    xs = _sc_dispatch(hp, dest_ck, nb * ROW_TILE)
    outs = _experts(xs, bstart, w_gate_e[0], w_up_e[0], w_down_e[0])
    y_routed = _sc_combine(outs, dest_ck, wrep, t)

    out = _final(x1, hp, y_routed, ada3,
                 w_gate_s[0].astype(BF16), w_up_s[0].astype(BF16), w_down_s[0].astype(BF16),
                 g_final.reshape(1, d), seq)
    return out.reshape(bsz, seq, d)
```

```python
import functools
import math

import jax
import jax.numpy as jnp
from jax import lax
from jax.experimental import pallas as pl
from jax.experimental.pallas import tpu as pltpu
from jax.experimental.pallas import tpu_sc as plsc

F32 = jnp.float32
BF16 = jnp.bfloat16
I32 = jnp.int32

D_CONV = 512
CONV_WIDTH = 3
N_HEADS = 4
HEAD_DIM = 64
V_DIM = 2 * HEAD_DIM
D_ATTN = N_HEADS * V_DIM
D_QK = N_HEADS * 2 * HEAD_DIM
ROT_DIM = HEAD_DIM // 4
ROPE_THETA = 500000.0
N_EXPERTS = 256
TOP_K = 8
N_GROUPS = 8
GROUP_SIZE = N_EXPERTS // N_GROUPS
TOPK_GROUPS = 4
D_EXPERT = 256
ROUTED_SCALE = 2.5
NORM_EPS = 1e-6
SUBLN_EPS = 1e-5
LAMBDA_INIT = 0.8 - 0.6 * math.exp(-0.3 * 0)

LANES = 128
SC_CORES = 2
SC_SUBCORES = 16
SC_WORKERS = SC_CORES * SC_SUBCORES
SC_CHUNK = 128
COMBINE_SLOTS = 64
COMBINE_BUFS = 2
SC_LANES = 16

ROW_TILE = 128
EXPERT_GROUP = 4
EXPERTS_PER_STEP = 2
EXPERT_BUFS = 12
ATTN_ONES_ROWS = 16
NEG_BIG = -1e30
VMEM_LIMIT = 56 * 1024 * 1024


def _cparams(sem):
    return pltpu.CompilerParams(dimension_semantics=sem, vmem_limit_bytes=VMEM_LIMIT)


def _rms(x, eps):
    return x * lax.rsqrt(jnp.mean(x * x, axis=-1, keepdims=True) + eps)


def _pack_bf16_pairs(x):
    n = x.shape[1] // 2
    bits = lax.bitcast_convert_type(x.astype(BF16).astype(F32), I32)
    lo = lax.shift_right_logical(bits[:, :n], 16)
    return lo | bits[:, n:]


def _unpack_bf16_pairs(p):
    lo = lax.bitcast_convert_type(lax.shift_left(p, 16), F32)
    hi = lax.bitcast_convert_type(p & jnp.int32(-65536), F32)
    return jnp.concatenate([lo, hi], axis=1)


def _ada_kernel(c_ref, w_ref, b_ref, o_ref):
    ca = jax.nn.silu(c_ref[...])
    o_ref[...] = jnp.dot(ca.astype(BF16), w_ref[...].astype(BF16), preferred_element_type=F32) + b_ref[...]


def _ada(c, w_ada, b_ada):
    bsz, d = c.shape
    n = w_ada.shape[1]
    tn = n // 4
    return pl.pallas_call(
        _ada_kernel,
        grid=(n // tn,),
        in_specs=[pl.BlockSpec((bsz, d), lambda j: (0, 0)),
                  pl.BlockSpec((d, tn), lambda j: (0, j)),
                  pl.BlockSpec((1, tn), lambda j: (0, j))],
        out_specs=pl.BlockSpec((bsz, tn), lambda j: (0, j)),
        out_shape=jax.ShapeDtypeStruct((bsz, n), F32),
        compiler_params=_cparams(("arbitrary",)),
        name="ada",
    )(c, w_ada, b_ada.reshape(1, n))


def _inproj_kernel(tiles_per_seq, x_ref, pos_ref, ada_ref, g1_ref, win_ref, convw_ref, gconv_ref, invf_ref,
                   yconv_ref, qt_ref, k_ref, vt_ref, ubuf):
    i = pl.program_id(0)
    tm = x_ref.shape[0]

    @pl.when(i == 0)
    def _():
        ubuf[tm:tm + 8, :] = jnp.zeros((8, D_CONV), F32)

    x = x_ref[...]
    ada = ada_ref[0]
    sh1, sc1 = ada[0:1], ada[1:2]
    hn = _rms(x, NORM_EPS) * g1_ref[...] * (1.0 + sc1) + sh1
    hb = hn.astype(BF16)

    pc = jnp.dot(hb, win_ref[:, 0:3 * D_CONV], preferred_element_type=F32)
    u = pc[:, 2 * D_CONV:3 * D_CONV] * pc[:, 0:D_CONV]
    first = (i % tiles_per_seq) == 0
    ubuf[0:8, :] = jnp.where(first, 0.0, ubuf[tm:tm + 8, :])
    ubuf[8:8 + tm, :] = u
    u1 = ubuf[7:7 + tm, :]
    u2 = ubuf[6:6 + tm, :]
    cw = convw_ref[...]
    yc = pc[:, D_CONV:2 * D_CONV] * (cw[0:1] * u2 + cw[1:2] * u1 + cw[2:3] * u)
    yconv_ref[...] = (_rms(yc, NORM_EPS) * gconv_ref[...]).astype(BF16)

    half = ROT_DIM // 2
    ang = invf_ref[...] * pos_ref[...].astype(F32)
    cos_ft = jnp.cos(ang)
    sin_ft = jnp.sin(ang)
    zero_h = jnp.zeros((half, tm), F32)
    zero_r = jnp.zeros((HEAD_DIM - ROT_DIM, tm), F32)
    lay = lambda a, b, r: jnp.concatenate([a, b, r, a, b, r], axis=0).T
    cos = lay(cos_ft, cos_ft, zero_r + 1.0)
    s_up = lay(-sin_ft, zero_h, zero_r)
    s_dn = lay(zero_h, sin_ft, zero_r)

    def rope(t):
        return t * cos + pltpu.roll(t, LANES - half, axis=1) * s_up + pltpu.roll(t, half, axis=1) * s_dn

    pq = jnp.dot(hb, win_ref[:, 3 * D_CONV:3 * D_CONV + D_QK], preferred_element_type=F32)
    pk = jnp.dot(hb, win_ref[:, 3 * D_CONV + D_QK:3 * D_CONV + 2 * D_QK], preferred_element_type=F32)
    scale = HEAD_DIM ** -0.5 * math.log2(math.e)
    qs = []
    for h in range(N_HEADS):
        sl = slice(h * LANES, (h + 1) * LANES)
        qs.append(rope(pq[:, sl]) * scale)
        k_ref[:, sl] = rope(pk[:, sl]).astype(BF16)
    qt_ref[0] = jnp.concatenate(qs, axis=1).T.astype(BF16)
    pv = jnp.dot(hb, win_ref[:, 3 * D_CONV + 2 * D_QK:], preferred_element_type=F32)
    vt_ref[0] = pv.T.astype(BF16)


def _inproj(xf, pos, ada3, g1, win_b, conv_w, g_conv, invf, seq):
    t, d = xf.shape
    tm = min(1024, seq)
    tiles_per_seq = seq // tm
    n_in = win_b.shape[1]
    full = lambda shape: pl.BlockSpec(shape, lambda i: (0,) * len(shape))
    row = lambda w: pl.BlockSpec((tm, w), lambda i: (i, 0))
    colmajor = lambda w: pl.BlockSpec((1, w, tm), lambda i: (i // tiles_per_seq, 0, i % tiles_per_seq))
    return pl.pallas_call(
        functools.partial(_inproj_kernel, tiles_per_seq),
        grid=(t // tm,),
        in_specs=[row(d), pl.BlockSpec((1, tm), lambda i: (0, i)),
                  pl.BlockSpec((1, 6, d), lambda i: (i // tiles_per_seq, 0, 0)),
                  full((1, d)), full((d, n_in)), full((CONV_WIDTH, D_CONV)), full((1, D_CONV)),
                  full((ROT_DIM // 2, 1))],
        out_specs=[row(D_CONV), colmajor(D_QK), row(D_QK), colmajor(D_ATTN)],
        out_shape=[jax.ShapeDtypeStruct((t, D_CONV), BF16), jax.ShapeDtypeStruct((t // seq, D_QK, seq), BF16),
                   jax.ShapeDtypeStruct((t, D_QK), BF16), jax.ShapeDtypeStruct((t // seq, D_ATTN, seq), BF16)],
        scratch_shapes=[pltpu.VMEM((tm + 8, D_CONV), F32)],
        compiler_params=_cparams(("arbitrary",)),
        name="inproj",
    )(xf, pos, ada3, g1, win_b, conv_w, g_conv, invf)


def _attn_steps(nq):
    off = [(j, i) for j in range(nq) for i in range(j + 1, nq)]
    diag = [(i, i) for i in range(nq)]
    return off, diag


def _attn_kernel(steps_ref, qt_ref, k_ref, vt_ref, lamp_ref, gs_ref, o_ref,
                 qq_ref, m_ref, acc_ref, s0_ref, s1_ref):
    nq = m_ref.shape[0]
    tq = s0_ref.shape[0]
    tk = tq
    off, diag = _attn_steps(nq)
    n_steps = len(off) + len(diag)
    bufs = (s0_ref, s1_ref)

    feat = lax.broadcasted_iota(I32, (LANES, tq), 0)
    for i in range(nq):
        qt = qt_ref[0, :, i * tq:(i + 1) * tq]
        zero = jnp.zeros_like(qt)
        qq_ref[:, 2 * i * tq:(2 * i + 1) * tq] = jnp.where(feat < HEAD_DIM, qt, zero)
        qq_ref[:, (2 * i + 1) * tq:(2 * i + 2) * tq] = jnp.where(feat >= HEAD_DIM, qt, zero)
    m_ref[...] = jnp.full(m_ref.shape, NEG_BIG, F32)
    acc_ref[...] = jnp.zeros(acc_ref.shape, F32)
    ones_rows = jnp.ones((ATTN_ONES_ROWS, tk), BF16)

    def scores(j, i, dst):
        kstart = pl.multiple_of(j * tk, tk)
        qstart = pl.multiple_of(i * (2 * tq), 2 * tq)
        dst[...] = jnp.dot(k_ref[pl.ds(kstart, tk), :], qq_ref[:, pl.ds(qstart, 2 * tq)],
                           preferred_element_type=F32)

    def consume(src, j, i, masked):
        kstart = pl.multiple_of(j * tk, tk)
        vtb = jnp.concatenate([vt_ref[0, :, pl.ds(kstart, tk)], ones_rows], axis=0)
        s = src[...]
        if masked:
            key = lax.broadcasted_iota(I32, s.shape, 0)
            col = lax.broadcasted_iota(I32, s.shape, 1)
            qpos = jnp.where(col >= tq, col - tq, col)
            s = jnp.where(key <= qpos, s, NEG_BIG)
        m_prev = m_ref[i]
        m_new = jnp.maximum(m_prev, jnp.max(s, axis=0, keepdims=True))
        alpha = jnp.exp2(m_prev - m_new)
        p = jnp.exp2(s - m_new)
        acc_ref[i] = alpha * acc_ref[i] + jnp.dot(vtb, p.astype(BF16), preferred_element_type=F32)
        m_ref[i] = m_new

    def at(n):
        return steps_ref[0, n], steps_ref[1, n]

    scores(*at(0), s0_ref)
    n_pairs = len(off) // 2

    def pair(pp, carry):
        n = 2 * pp
        scores(*at(n + 1), s1_ref)
        consume(s0_ref, *at(n), False)
        scores(*at(n + 2), s0_ref)
        consume(s1_ref, *at(n + 1), False)
        return carry

    lax.fori_loop(0, n_pairs, pair, 0)

    static_steps = (off + diag)[2 * n_pairs:]
    for n, (j, i) in enumerate(static_steps, start=2 * n_pairs):
        if n + 1 < n_steps:
            scores(*static_steps[n + 1 - 2 * n_pairs], bufs[(n + 1) % 2])
        consume(bufs[n % 2], j, i, masked=(j == i))

    lamp = lamp_ref[...]
    lam = (jnp.exp(jnp.sum(lamp[0:1] * lamp[1:2], axis=1, keepdims=True))
           - jnp.exp(jnp.sum(lamp[2:3] * lamp[3:4], axis=1, keepdims=True)) + LAMBDA_INIT)
    for i in range(nq):
        o = acc_ref[i, 0:V_DIM] / acc_ref[i, V_DIM:V_DIM + 1]
        od = o[:, 0:tq] - lam * o[:, tq:2 * tq]
        y = od * lax.rsqrt(jnp.mean(od * od, axis=0, keepdims=True) + SUBLN_EPS) * gs_ref[...] * (1.0 - LAMBDA_INIT)
        o_ref[i * tq:(i + 1) * tq, :] = y.T.astype(BF16)


def _attention(qt, k, vt, lamp, g_subln, bsz, seq):
    t = k.shape[0]
    tq = min(512, seq)
    nq = seq // tq
    off, diag = _attn_steps(nq)
    steps = jnp.asarray(list(zip(*(off + diag))), I32)
    return pl.pallas_call(
        _attn_kernel,
        grid_spec=pltpu.PrefetchScalarGridSpec(
            num_scalar_prefetch=1,
            grid=(bsz, N_HEADS),
            in_specs=[pl.BlockSpec((1, LANES, seq), lambda b, h, st: (b, h, 0)),
                      pl.BlockSpec((seq, LANES), lambda b, h, st: (b, h)),
                      pl.BlockSpec((1, V_DIM, seq), lambda b, h, st: (b, h, 0)),
                      pl.BlockSpec((4, HEAD_DIM), lambda b, h, st: (0, 0)),
                      pl.BlockSpec((V_DIM, 1), lambda b, h, st: (0, 0))],
            out_specs=pl.BlockSpec((seq, LANES), lambda b, h, st: (b, h)),
            scratch_shapes=[pltpu.VMEM((LANES, 2 * seq), BF16), pltpu.VMEM((nq, 1, 2 * tq), F32),
                            pltpu.VMEM((nq, V_DIM + ATTN_ONES_ROWS, 2 * tq), F32),
                            pltpu.VMEM((tq, 2 * tq), F32), pltpu.VMEM((tq, 2 * tq), F32)],
        ),
        out_shape=jax.ShapeDtypeStruct((t, D_ATTN), BF16),
        compiler_params=_cparams(("arbitrary", "arbitrary")),
        name="attention",
    )(steps, qt, k, vt, lamp, g_subln)


def _first_index(hit, idx, size, axis):
    return jnp.min(jnp.where(hit, idx, size), axis=axis, keepdims=True)


def _outproj_router_kernel(yc_ref, ya_ref, x_ref, ada_ref, g2_ref, wout_ref, wrt_ref, rb_ref,
                           x1_ref, hp_ref, idx_ref, wts_ref, rank_ref, cnt_ref, base_ref):
    i = pl.program_id(0)
    tm = x_ref.shape[0]

    @pl.when(i == 0)
    def _():
        base_ref[...] = jnp.zeros(base_ref.shape, F32)

    ada = ada_ref[0]
    gt1, sh2, sc2 = ada[2:3], ada[3:4], ada[4:5]
    mix = (jnp.dot(yc_ref[...], wout_ref[0:D_CONV, :], preferred_element_type=F32)
           + jnp.dot(ya_ref[...], wout_ref[D_CONV:, :], preferred_element_type=F32))
    x1 = x_ref[...] + gt1 * mix
    x1_ref[...] = x1
    hn2 = _rms(x1, NORM_EPS) * g2_ref[...] * (1.0 + sc2) + sh2
    hp_ref[...] = _pack_bf16_pairs(hn2)
    hb = hn2.astype(BF16)

    logits = lax.dot_general(wrt_ref[...], hb, (((1,), (1,)), ((), ())), preferred_element_type=F32)
    scores = jax.nn.sigmoid(logits)
    choice = scores + rb_ref[...]
    neg_inf = jnp.float32(-jnp.inf)

    ch3 = choice.reshape(N_GROUPS, GROUP_SIZE, tm)
    i3 = lax.broadcasted_iota(I32, ch3.shape, 1)
    m1 = jnp.max(ch3, axis=1, keepdims=True)
    f1 = _first_index(ch3 == m1, i3, GROUP_SIZE, 1)
    m2 = jnp.max(jnp.where(i3 == f1, neg_inf, ch3), axis=1, keepdims=True)
    gs = (m1 + m2).reshape(N_GROUPS, tm)

    gi = lax.broadcasted_iota(I32, gs.shape, 0)
    gkeep = jnp.zeros(gs.shape, F32)
    for _ in range(TOPK_GROUPS):
        m = jnp.max(gs, axis=0, keepdims=True)
        f = _first_index(gs == m, gi, N_GROUPS, 0)
        sel = gi == f
        gkeep = jnp.where(sel, 1.0, gkeep)
        gs = jnp.where(sel, neg_inf, gs)
    ekeep = jnp.broadcast_to(gkeep.reshape(N_GROUPS, 1, tm), (N_GROUPS, GROUP_SIZE, tm)).reshape(N_EXPERTS, tm)
    masked = jnp.where(ekeep > 0.0, choice, neg_inf)

    ei = lax.broadcasted_iota(I32, masked.shape, 0)
    picked = jnp.zeros(masked.shape, F32)
    idxs, ws = [], []
    for _ in range(TOP_K):
        m = jnp.max(masked, axis=0, keepdims=True)
        f = _first_index(masked == m, ei, N_EXPERTS, 0)
        sel = ei == f
        idxs.append(f)
        ws.append(jnp.sum(jnp.where(sel, scores, 0.0), axis=0, keepdims=True))
        picked = jnp.where(sel, 1.0, picked)
        masked = jnp.where(sel, neg_inf, masked)
    wsum = ws[0]
    for wk in ws[1:]:
        wsum = wsum + wk
    denom = wsum + 1e-20
    for kk in range(TOP_K):
        idx_ref[kk:kk + 1, :] = idxs[kk]
    wrep = jnp.concatenate([jnp.broadcast_to(ws[kk] / denom * ROUTED_SCALE, (SC_LANES, tm)) for kk in range(TOP_K)],
                           axis=0)
    wts_ref[...] = wrep.T

    si = lax.broadcasted_iota(I32, (tm, tm), 0)
    ti = lax.broadcasted_iota(I32, (tm, tm), 1)
    earlier = jnp.where(si < ti, 1.0, 0.0).astype(BF16)
    before = jnp.dot(picked.astype(BF16), earlier, preferred_element_type=F32) + base_ref[...]
    for kk in range(TOP_K):
        rank_ref[kk:kk + 1, :] = jnp.sum(jnp.where(ei == idxs[kk], before, 0.0), axis=0, keepdims=True).astype(I32)
    base_ref[...] = base_ref[...] + jnp.sum(picked, axis=1, keepdims=True)
    cnt_ref[...] = base_ref[...]


def _outproj_router(yconv, yattn, xf, ada3, g2, wout_b, wrt_b, rbias, seq):
    t, d = xf.shape
    tm = min(512, seq)
    tiles_per_seq = seq // tm
    full = lambda shape: pl.BlockSpec(shape, lambda i: (0,) * len(shape))
    row = lambda w: pl.BlockSpec((tm, w), lambda i: (i, 0))
    col = pl.BlockSpec((TOP_K, tm), lambda i: (0, i))
    return pl.pallas_call(
        _outproj_router_kernel,
        grid=(t // tm,),
        in_specs=[row(D_CONV), row(D_ATTN), row(d),
                  pl.BlockSpec((1, 6, d), lambda i: (i // tiles_per_seq, 0, 0)),
                  full((1, d)), full((D_CONV + D_ATTN, d)), full((N_EXPERTS, d)), full((N_EXPERTS, 1))],
        out_specs=[row(d), row(d // 2), col, row(TOP_K * SC_LANES), col, full((N_EXPERTS, 1))],
        out_shape=[jax.ShapeDtypeStruct((t, d), F32), jax.ShapeDtypeStruct((t, d // 2), I32),
                   jax.ShapeDtypeStruct((TOP_K, t), I32), jax.ShapeDtypeStruct((t, TOP_K * SC_LANES), F32),
                   jax.ShapeDtypeStruct((TOP_K, t), I32), jax.ShapeDtypeStruct((N_EXPERTS, 1), F32)],
        scratch_shapes=[pltpu.VMEM((N_EXPERTS, 1), F32)],
        compiler_params=_cparams(("arbitrary",)),
        name="outproj_router",
    )(yconv, yattn, xf, ada3, g2, wout_b, wrt_b, rbias)


def _dest_kernel(idx_ref, rank_ref, ps_ref, dest_ref):
    tm = idx_ref.shape[1]
    ei = lax.broadcasted_iota(I32, (N_EXPERTS, tm), 0)
    ps = ps_ref[...]
    for kk in range(TOP_K):
        start = jnp.sum(jnp.where(ei == idx_ref[kk:kk + 1, :], ps, 0.0), axis=0, keepdims=True)
        dest_ref[kk:kk + 1, :] = start.astype(I32) + rank_ref[kk:kk + 1, :]


def _dest(idx_t, rank_t, pad_start_f):
    t = idx_t.shape[1]
    tm = min(2048, t)
    col = pl.BlockSpec((TOP_K, tm), lambda i: (0, i))
    return pl.pallas_call(
        _dest_kernel,
        grid=(t // tm,),
        in_specs=[col, col, pl.BlockSpec((N_EXPERTS, 1), lambda i: (0, 0))],
        out_specs=col,
        out_shape=jax.ShapeDtypeStruct((TOP_K, t), I32),
        compiler_params=_cparams(("arbitrary",)),
        name="dest",
    )(idx_t, rank_t, pad_start_f)


def _sc_dispatch(hp, dest_ck, n_rows):
    t, w = hp.shape
    nk = dest_ck.shape[1]
    per_worker = t // SC_CHUNK // SC_WORKERS
    mesh = plsc.VectorSubcoreMesh(core_axis_name="c", subcore_axis_name="s")

    @functools.partial(
        pl.kernel, mesh=mesh,
        out_type=jax.ShapeDtypeStruct((n_rows, w), hp.dtype),
        scratch_types=[pltpu.VMEM((nk, SC_CHUNK), I32), pltpu.VMEM((SC_CHUNK, w), hp.dtype), pltpu.SemaphoreType.DMA],
    )
    def k(hp_hbm, dest_hbm, xs_hbm, idx_v, rows_v, sem):
        wid = lax.axis_index("s") * SC_CORES + lax.axis_index("c")

        @pl.loop(0, per_worker)
        def _(j):
            c = wid * per_worker + j
            pltpu.sync_copy(dest_hbm.at[c], idx_v)
            pltpu.sync_copy(hp_hbm.at[pl.ds(c * SC_CHUNK, SC_CHUNK)], rows_v)
            for kk in range(nk):
                pltpu.async_copy(rows_v, xs_hbm.at[idx_v.at[kk]], sem)
            for kk in range(nk):
                pltpu.make_async_copy(rows_v, xs_hbm.at[idx_v.at[kk]], sem).wait()

    return k(hp, dest_ck)


def _sc_combine(src, dest_ck, wrep, n_tokens):
    half = src.shape[1]
    lanes = wrep.shape[1] // TOP_K
    ctok = COMBINE_SLOTS // TOP_K
    per_worker = n_tokens // ctok // SC_WORKERS
    steps_per_chunk = SC_CHUNK // ctok
    chunks_per_worker = per_worker // steps_per_chunk
    mesh = plsc.VectorSubcoreMesh(core_axis_name="c", subcore_axis_name="s")

    @functools.partial(
        pl.kernel, mesh=mesh,
        out_type=jax.ShapeDtypeStruct((n_tokens, 2 * half), F32),
        scratch_types=[pltpu.VMEM((chunks_per_worker, TOP_K, SC_CHUNK), I32),
                       pltpu.VMEM((COMBINE_BUFS, COMBINE_SLOTS, half), I32),
                       pltpu.VMEM((COMBINE_BUFS, ctok, TOP_K * lanes), F32),
                       pltpu.VMEM((COMBINE_BUFS, ctok, 2 * half), F32),
                       pltpu.SemaphoreType.DMA((COMBINE_BUFS,)), pltpu.SemaphoreType.DMA((COMBINE_BUFS,)),
                       pltpu.SemaphoreType.DMA((COMBINE_BUFS,))],
        compiler_params=pltpu.CompilerParams(needs_layout_passes=False),
    )
    def k(src_hbm, idx_hbm, w_hbm, y_hbm, idx_v, rows_v, w_v, out_v, sem_g, sem_w, sem_o):
        wid = lax.axis_index("s") * SC_CORES + lax.axis_index("c")
        base = wid * per_worker
        pltpu.sync_copy(idx_hbm.at[pl.ds(wid * chunks_per_worker, chunks_per_worker)], idx_v)

        class _Gather:
            def __init__(self, c, slot):
                chunk = c // steps_per_chunk
                first = (c % steps_per_chunk) * ctok
                self.copies = [
                    pltpu.make_async_copy(src_hbm.at[idx_v.at[chunk, kk, pl.ds(first, ctok)]],
                                          rows_v.at[slot, pl.ds(kk * ctok, ctok)], sem_g.at[slot])
                    for kk in range(TOP_K)]

            def start(self):
                for cp in self.copies:
                    cp.start()

            def wait(self):
                for cp in self.copies:
                    cp.wait()

        gather = _Gather

        def wload(c, slot):
            return pltpu.make_async_copy(w_hbm.at[pl.ds((base + c) * ctok, ctok)], w_v.at[slot], sem_w.at[slot])

        def store(c, slot):
            return pltpu.make_async_copy(out_v.at[slot], y_hbm.at[pl.ds((base + c) * ctok, ctok)], sem_o.at[slot])

        nbuf = COMBINE_BUFS
        for c in range(nbuf - 1):
            gather(c, c).start()
            wload(c, c).start()

        @pl.loop(0, pl.cdiv(per_worker, nbuf) * nbuf, step=nbuf)
        def _(c0):
            for slot in range(nbuf):
                c = c0 + slot

                @pl.when(c < per_worker)
                def _():
                    gather(c, slot).wait()
                    wload(c, slot).wait()

                    @pl.when(c + nbuf - 1 < per_worker)
                    def _():
                        gather(c + nbuf - 1, (slot + nbuf - 1) % nbuf).start()
                        wload(c + nbuf - 1, (slot + nbuf - 1) % nbuf).start()

                    @pl.when(c >= nbuf)
                    def _():
                        store(c - nbuf, slot).wait()

                    @pl.loop(0, ctok)
                    def _(tt):
                        ws = [w_v[slot, tt, pl.ds(kk * lanes, lanes)] for kk in range(TOP_K)]

                        @plsc.parallel_loop(0, half // lanes, unroll=2)
                        def _(v):
                            lo = jnp.zeros((lanes,), F32)
                            hi = jnp.zeros((lanes,), F32)
                            for kk in range(TOP_K):
                                x = rows_v[slot, kk * ctok + tt, pl.ds(v * lanes, lanes)]
                                x_lo, x_hi = plsc.unpack(plsc.bitcast(x, BF16), format=plsc.PackFormat.INTERLEAVED)
                                lo = lo + x_lo * ws[kk]
                                hi = hi + x_hi * ws[kk]
                            out_v[slot, tt, pl.ds(v * lanes, lanes)] = lo
                            out_v[slot, tt, pl.ds(half + v * lanes, lanes)] = hi

                    store(c, slot).start()

        for c in range(per_worker - nbuf, per_worker):
            store(c, c % nbuf).wait()

    return k(src, dest_ck, wrep)


def _experts_kernel(bstart_ref, xs_hbm, wg_ref, wu_ref, wd_ref, out_hbm,
                    wgu_b, wd_b, xbuf, obuf, sem_in, sem_out):
    step = pl.program_id(0)
    n_used = bstart_ref[N_EXPERTS]
    nbuf = EXPERT_BUFS

    def in_copy(g, slot):
        return pltpu.make_async_copy(xs_hbm.at[pl.ds(g * ROW_TILE, ROW_TILE)], xbuf.at[slot], sem_in.at[slot])

    def out_copy(g, slot):
        return pltpu.make_async_copy(obuf.at[slot], out_hbm.at[pl.ds(g * ROW_TILE, ROW_TILE)], sem_out.at[slot])

    def fetch(g):
        @pl.when(g < n_used)
        def _():
            in_copy(g, lax.rem(g, nbuf)).start()

    def run(blocks):
        slots = [lax.rem(g, nbuf) for g in blocks]
        for g, slot in zip(blocks, slots):
            in_copy(g, slot).wait()
        fetch(blocks[0] + nbuf - 1)
        for g, slot in zip(blocks, slots):
            @pl.when(g >= nbuf)
            def _():
                out_copy(g - nbuf, slot).wait()
        xs_rows = [_unpack_bf16_pairs(xbuf[slot]).astype(BF16) for slot in slots]
        xb = xs_rows[0] if len(slots) == 1 else jnp.concatenate(xs_rows, axis=0)
        gu = jnp.dot(xb, wgu_b[...], preferred_element_type=F32)
        act = (jax.nn.silu(gu[:, 0:D_EXPERT]) * gu[:, D_EXPERT:2 * D_EXPERT]).astype(BF16)
        out = _pack_bf16_pairs(jnp.dot(act, wd_b[...], preferred_element_type=F32))
        for n, slot in enumerate(slots):
            obuf[slot] = out[n * ROW_TILE:(n + 1) * ROW_TILE]
        for g, slot in zip(blocks, slots):
            out_copy(g, slot).start()
        for g in blocks[:-1]:
            fetch(g + nbuf)

    @pl.when(step == 0)
    def _():
        for g in range(nbuf - 1):
            @pl.when(g < n_used)
            def _():
                in_copy(g, g).start()

    for sub in range(EXPERTS_PER_STEP):
        e = step * EXPERTS_PER_STEP + sub
        b0 = bstart_ref[e]
        b1 = bstart_ref[e + 1]

        @pl.when(b1 > b0)
        def _():
            wgu_b[:, 0:D_EXPERT] = wg_ref[sub].astype(BF16)
            wgu_b[:, D_EXPERT:2 * D_EXPERT] = wu_ref[sub].astype(BF16)
            wd_b[...] = wd_ref[sub].astype(BF16)
            group = EXPERT_GROUP

            def full_group(jj, carry):
                g = b0 + group * jj
                run([g + n for n in range(group)])
                return carry

            n_full = (b1 - b0) // group
            lax.fori_loop(0, n_full, full_group, 0)
            rest = (b1 - b0) - n_full * group
            tail = b0 + n_full * group
            for size in range(1, group):
                @pl.when(rest == size)
                def _():
                    run([tail + n for n in range(size)])

    @pl.when(step == pl.num_programs(0) - 1)
    def _():
        for back in range(1, nbuf + 1):
            @pl.when(n_used >= back)
            def _():
                g = n_used - back
                out_copy(g, lax.rem(g, nbuf)).wait()


def _experts(xs, bstart, w_gate, w_up, w_down):
    n_rows, half = xs.shape
    d = 2 * half
    weights = lambda shape: pl.BlockSpec((EXPERTS_PER_STEP,) + shape, lambda s, bs: (s, 0, 0))
    return pl.pallas_call(
        _experts_kernel,
        grid_spec=pltpu.PrefetchScalarGridSpec(
            num_scalar_prefetch=1,
            grid=(N_EXPERTS // EXPERTS_PER_STEP,),
            in_specs=[pl.BlockSpec(memory_space=pl.ANY),
                      weights((d, D_EXPERT)), weights((d, D_EXPERT)), weights((D_EXPERT, d))],
            out_specs=pl.BlockSpec(memory_space=pl.ANY),
            scratch_shapes=[pltpu.VMEM((d, 2 * D_EXPERT), BF16), pltpu.VMEM((D_EXPERT, d), BF16),
                            pltpu.VMEM((EXPERT_BUFS, ROW_TILE, half), I32),
                            pltpu.VMEM((EXPERT_BUFS, ROW_TILE, half), I32),
                            pltpu.SemaphoreType.DMA((EXPERT_BUFS,)), pltpu.SemaphoreType.DMA((EXPERT_BUFS,))],
        ),
        out_shape=jax.ShapeDtypeStruct((n_rows, half), I32),
        compiler_params=_cparams(("arbitrary",)),
        name="experts",
    )(bstart, xs, w_gate, w_up, w_down)


def _final_kernel(x1_ref, hp_ref, yr_ref, ada_ref, wgs_ref, wus_ref, wds_ref, gf_ref, o_ref):
    gt2 = ada_ref[0][5:6]
    hb = _unpack_bf16_pairs(hp_ref[...]).astype(BF16)
    g = jnp.dot(hb, wgs_ref[...], preferred_element_type=F32)
    u = jnp.dot(hb, wus_ref[...], preferred_element_type=F32)
    y = jnp.dot((jax.nn.silu(g) * u).astype(BF16), wds_ref[...], preferred_element_type=F32)
    x2 = x1_ref[...] + gt2 * (yr_ref[...] + y)
    o_ref[...] = _rms(x2, NORM_EPS) * gf_ref[...]


def _final(x1, hp, y_routed, ada3, wgs_b, wus_b, wds_b, g_final, seq):
    t, d = x1.shape
    tm = min(512, seq)
    tiles_per_seq = seq // tm
    full = lambda shape: pl.BlockSpec(shape, lambda i: (0,) * len(shape))
    row = lambda w: pl.BlockSpec((tm, w), lambda i: (i, 0))
    return pl.pallas_call(
        _final_kernel,
        grid=(t // tm,),
        in_specs=[row(d), row(d // 2), row(d),
                  pl.BlockSpec((1, 6, d), lambda i: (i // tiles_per_seq, 0, 0)),
                  full((d, D_EXPERT)), full((d, D_EXPERT)), full((D_EXPERT, d)), full((1, d))],
        out_specs=row(d),
        out_shape=jax.ShapeDtypeStruct((t, d), F32),
        compiler_params=_cparams(("arbitrary",)),
        name="final",
    )(x1, hp, y_routed, ada3, wgs_b, wus_b, wds_b, g_final)


def kernel(x, c, positions, w_ada, b_ada, g_norm1, w_in, conv_w, g_conv_out, lam_q1, lam_k1, lam_q2, lam_k2, g_subln, w_out, g_norm2, w_router, router_bias, w_gate_e, w_up_e, w_down_e, w_gate_s, w_up_s, w_down_s, g_final):
    bsz, seq, d = x.shape
    t = bsz * seq
    xf = x.reshape(t, d)
    pos = positions.reshape(1, t)
    invf = (ROPE_THETA ** (-jnp.arange(0, ROT_DIM, 2, dtype=F32) / ROT_DIM)).reshape(ROT_DIM // 2, 1)

    ada3 = _ada(c, w_ada[0], b_ada[0]).reshape(bsz, 6, d)
    yconv, qt, k, vt = _inproj(xf, pos, ada3, g_norm1[0].reshape(1, d), w_in[0].astype(BF16), conv_w[0],
                             g_conv_out[0].reshape(1, D_CONV), invf, seq)
    lamp = jnp.stack([lam_q1[0], lam_k1[0], lam_q2[0], lam_k2[0]]).astype(F32)
    yattn = _attention(qt, k, vt, lamp, g_subln[0].reshape(V_DIM, 1), bsz, seq)

    x1, hp, idx_t, wrep, rank_t, counts_f = _outproj_router(
        yconv, yattn, xf, ada3, g_norm2[0].reshape(1, d), w_out[0].astype(BF16),
        w_router[0].T.astype(BF16), router_bias[0].reshape(N_EXPERTS, 1), seq)

    counts = counts_f[:, 0].astype(I32)
    padded = ((counts + ROW_TILE - 1) // ROW_TILE) * ROW_TILE
    pad_end = jnp.cumsum(padded)
    pad_start = pad_end - padded
    nb = (t * TOP_K + N_EXPERTS * (ROW_TILE - 1)) // ROW_TILE
    bstart = (jnp.concatenate([pad_start, pad_end[-1:]]) // ROW_TILE).astype(I32)

    dest_t = _dest(idx_t, rank_t, pad_start.astype(F32).reshape(N_EXPERTS, 1))
    n_chunks = t // SC_CHUNK
    dest_ck = dest_t.reshape(TOP_K, n_chunks, SC_CHUNK).transpose(1, 0, 2)

    xs = _sc_dispatch(hp, dest_ck, nb * ROW_TILE)
    outs = _experts(xs, bstart, w_gate_e[0], w_up_e[0], w_down_e[0])
    y_routed = _sc_combine(outs, dest_ck, wrep, t)

    out = _final(x1, hp, y_routed, ada3,
                 w_gate_s[0].astype(BF16), w_up_s[0].astype(BF16), w_down_s[0].astype(BF16),
                 g_final.reshape(1, d), seq)
    return out.reshape(bsz, seq, d)
```

```python
import functools
import math

import jax
import jax.numpy as jnp
from jax import lax
from jax.experimental import pallas as pl
from jax.experimental.pallas import tpu as pltpu
from jax.experimental.pallas import tpu_sc as plsc

F32 = jnp.float32
BF16 = jnp.bfloat16
I32 = jnp.int32

D_CONV = 512
CONV_WIDTH = 3
N_HEADS = 4
HEAD_DIM = 64
V_DIM = 2 * HEAD_DIM
D_ATTN = N_HEADS * V_DIM
D_QK = N_HEADS * 2 * HEAD_DIM
ROT_DIM = HEAD_DIM // 4
ROPE_THETA = 500000.0
N_EXPERTS = 256
TOP_K = 8
N_GROUPS = 8
GROUP_SIZE = N_EXPERTS // N_GROUPS
TOPK_GROUPS = 4
D_EXPERT = 256
ROUTED_SCALE = 2.5
NORM_EPS = 1e-6
SUBLN_EPS = 1e-5
LAMBDA_INIT = 0.8 - 0.6 * math.exp(-0.3 * 0)

LANES = 128
SC_CORES = 2
SC_SUBCORES = 16
SC_WORKERS = SC_CORES * SC_SUBCORES
SC_CHUNK = 128
COMBINE_SLOTS = 64
COMBINE_BUFS = 2
SC_LANES = 16

ROW_TILE = 128
EXPERT_GROUP = 4
EXPERTS_PER_STEP = 2
EXPERT_BUFS = 12
ATTN_ONES_ROWS = 16
NEG_BIG = -1e30
VMEM_LIMIT = 56 * 1024 * 1024


def _cparams(sem):
    return pltpu.CompilerParams(dimension_semantics=sem, vmem_limit_bytes=VMEM_LIMIT)


def _rms(x, eps):
    return x * lax.rsqrt(jnp.mean(x * x, axis=-1, keepdims=True) + eps)


def _pack_bf16_pairs(x):
    n = x.shape[1] // 2
    bits = lax.bitcast_convert_type(x.astype(BF16).astype(F32), I32)
    lo = lax.shift_right_logical(bits[:, :n], 16)
    return lo | bits[:, n:]


def _unpack_bf16_pairs(p):
    lo = lax.bitcast_convert_type(lax.shift_left(p, 16), F32)
    hi = lax.bitcast_convert_type(p & jnp.int32(-65536), F32)
    return jnp.concatenate([lo, hi], axis=1)


def _ada_kernel(c_ref, w_ref, b_ref, o_ref):
    ca = jax.nn.silu(c_ref[...])
    o_ref[...] = jnp.dot(ca.astype(BF16), w_ref[...].astype(BF16), preferred_element_type=F32) + b_ref[...]


def _ada(c, w_ada, b_ada):
    bsz, d = c.shape
    n = w_ada.shape[1]
    tn = n // 4
    return pl.pallas_call(
        _ada_kernel,
        grid=(n // tn,),
        in_specs=[pl.BlockSpec((bsz, d), lambda j: (0, 0)),
                  pl.BlockSpec((d, tn), lambda j: (0, j)),
                  pl.BlockSpec((1, tn), lambda j: (0, j))],
        out_specs=pl.BlockSpec((bsz, tn), lambda j: (0, j)),
        out_shape=jax.ShapeDtypeStruct((bsz, n), F32),
        compiler_params=_cparams(("arbitrary",)),
        name="ada",
    )(c, w_ada, b_ada.reshape(1, n))


def _inproj_kernel(tiles_per_seq, x_ref, pos_ref, ada_ref, g1_ref, win_ref, convw_ref, gconv_ref, invf_ref,
                   yconv_ref, qt_ref, k_ref, vt_ref, ubuf):
    i = pl.program_id(0)
    tm = x_ref.shape[0]

    @pl.when(i == 0)
    def _():
        ubuf[tm:tm + 8, :] = jnp.zeros((8, D_CONV), F32)

    x = x_ref[...]
    ada = ada_ref[0]
    sh1, sc1 = ada[0:1], ada[1:2]
    hn = _rms(x, NORM_EPS) * g1_ref[...] * (1.0 + sc1) + sh1
    hb = hn.astype(BF16)

    pc = jnp.dot(hb, win_ref[:, 0:3 * D_CONV], preferred_element_type=F32)
    u = pc[:, 2 * D_CONV:3 * D_CONV] * pc[:, 0:D_CONV]
    first = (i % tiles_per_seq) == 0
    ubuf[0:8, :] = jnp.where(first, 0.0, ubuf[tm:tm + 8, :])
    ubuf[8:8 + tm, :] = u
    u1 = ubuf[7:7 + tm, :]
    u2 = ubuf[6:6 + tm, :]
    cw = convw_ref[...]
    yc = pc[:, D_CONV:2 * D_CONV] * (cw[0:1] * u2 + cw[1:2] * u1 + cw[2:3] * u)
    yconv_ref[...] = (_rms(yc, NORM_EPS) * gconv_ref[...]).astype(BF16)

    half = ROT_DIM // 2
    ang = invf_ref[...] * pos_ref[...].astype(F32)
    cos_ft = jnp.cos(ang)
    sin_ft = jnp.sin(ang)
    zero_h = jnp.zeros((half, tm), F32)
    zero_r = jnp.zeros((HEAD_DIM - ROT_DIM, tm), F32)
    lay = lambda a, b, r: jnp.concatenate([a, b, r, a, b, r], axis=0).T
    cos = lay(cos_ft, cos_ft, zero_r + 1.0)
    s_up = lay(-sin_ft, zero_h, zero_r)
    s_dn = lay(zero_h, sin_ft, zero_r)

    def rope(t):
        return t * cos + pltpu.roll(t, LANES - half, axis=1) * s_up + pltpu.roll(t, half, axis=1) * s_dn

    pq = jnp.dot(hb, win_ref[:, 3 * D_CONV:3 * D_CONV + D_QK], preferred_element_type=F32)
    pk = jnp.dot(hb, win_ref[:, 3 * D_CONV + D_QK:3 * D_CONV + 2 * D_QK], preferred_element_type=F32)
    scale = HEAD_DIM ** -0.5 * math.log2(math.e)
    qs = []
    for h in range(N_HEADS):
        sl = slice(h * LANES, (h + 1) * LANES)
        qs.append(rope(pq[:, sl]) * scale)
        k_ref[:, sl] = rope(pk[:, sl]).astype(BF16)
    qt_ref[0] = jnp.concatenate(qs, axis=1).T.astype(BF16)
    pv = jnp.dot(hb, win_ref[:, 3 * D_CONV + 2 * D_QK:], preferred_element_type=F32)
    vt_ref[0] = pv.T.astype(BF16)


def _inproj(xf, pos, ada3, g1, win_b, conv_w, g_conv, invf, seq):
    t, d = xf.shape
    tm = min(1024, seq)
    tiles_per_seq = seq // tm
    n_in = win_b.shape[1]
    full = lambda shape: pl.BlockSpec(shape, lambda i: (0,) * len(shape))
    row = lambda w: pl.BlockSpec((tm, w), lambda i: (i, 0))
    colmajor = lambda w: pl.BlockSpec((1, w, tm), lambda i: (i // tiles_per_seq, 0, i % tiles_per_seq))
    return pl.pallas_call(
        functools.partial(_inproj_kernel, tiles_per_seq),
        grid=(t // tm,),
        in_specs=[row(d), pl.BlockSpec((1, tm), lambda i: (0, i)),
                  pl.BlockSpec((1, 6, d), lambda i: (i // tiles_per_seq, 0, 0)),
                  full((1, d)), full((d, n_in)), full((CONV_WIDTH, D_CONV)), full((1, D_CONV)),
                  full((ROT_DIM // 2, 1))],
        out_specs=[row(D_CONV), colmajor(D_QK), row(D_QK), colmajor(D_ATTN)],
        out_shape=[jax.ShapeDtypeStruct((t, D_CONV), BF16), jax.ShapeDtypeStruct((t // seq, D_QK, seq), BF16),
                   jax.ShapeDtypeStruct((t, D_QK), BF16), jax.ShapeDtypeStruct((t // seq, D_ATTN, seq), BF16)],
        scratch_shapes=[pltpu.VMEM((tm + 8, D_CONV), F32)],
        compiler_params=_cparams(("arbitrary",)),
        name="inproj",
    )(xf, pos, ada3, g1, win_b, conv_w, g_conv, invf)


def _attn_steps(nq):
    off = [(j, i) for j in range(nq) for i in range(j + 1, nq)]
    diag = [(i, i) for i in range(nq)]
    return off, diag


def _attn_kernel(steps_ref, qt_ref, k_ref, vt_ref, lamp_ref, gs_ref, o_ref,
                 qq_ref, m_ref, acc_ref, s0_ref, s1_ref):
    nq = m_ref.shape[0]
    tq = s0_ref.shape[0]
    tk = tq
    off, diag = _attn_steps(nq)
    n_steps = len(off) + len(diag)
    bufs = (s0_ref, s1_ref)

    feat = lax.broadcasted_iota(I32, (LANES, tq), 0)
    for i in range(nq):
        qt = qt_ref[0, :, i * tq:(i + 1) * tq]
        zero = jnp.zeros_like(qt)
        qq_ref[:, 2 * i * tq:(2 * i + 1) * tq] = jnp.where(feat < HEAD_DIM, qt, zero)
        qq_ref[:, (2 * i + 1) * tq:(2 * i + 2) * tq] = jnp.where(feat >= HEAD_DIM, qt, zero)
    m_ref[...] = jnp.full(m_ref.shape, NEG_BIG, F32)
    acc_ref[...] = jnp.zeros(acc_ref.shape, F32)
    ones_rows = jnp.ones((ATTN_ONES_ROWS, tk), BF16)

    def scores(j, i, dst):
        kstart = pl.multiple_of(j * tk, tk)
        qstart = pl.multiple_of(i * (2 * tq), 2 * tq)
        dst[...] = jnp.dot(k_ref[pl.ds(kstart, tk), :], qq_ref[:, pl.ds(qstart, 2 * tq)],
                           preferred_element_type=F32)

    def consume(src, j, i, masked):
        kstart = pl.multiple_of(j * tk, tk)
        vtb = jnp.concatenate([vt_ref[0, :, pl.ds(kstart, tk)], ones_rows], axis=0)
        s = src[...]
        if masked:
            key = lax.broadcasted_iota(I32, s.shape, 0)
            col = lax.broadcasted_iota(I32, s.shape, 1)
            qpos = jnp.where(col >= tq, col - tq, col)
            s = jnp.where(key <= qpos, s, NEG_BIG)
        m_prev = m_ref[i]
        m_new = jnp.maximum(m_prev, jnp.max(s, axis=0, keepdims=True))
        alpha = jnp.exp2(m_prev - m_new)
        p = jnp.exp2(s - m_new)
        acc_ref[i] = alpha * acc_ref[i] + jnp.dot(vtb, p.astype(BF16), preferred_element_type=F32)
        m_ref[i] = m_new

    def at(n):
        return steps_ref[0, n], steps_ref[1, n]

    scores(*at(0), s0_ref)
    n_pairs = len(off) // 2

    def pair(pp, carry):
        n = 2 * pp
        scores(*at(n + 1), s1_ref)
        consume(s0_ref, *at(n), False)
        scores(*at(n + 2), s0_ref)
        consume(s1_ref, *at(n + 1), False)
        return carry

    lax.fori_loop(0, n_pairs, pair, 0)

    static_steps = (off + diag)[2 * n_pairs:]
    for n, (j, i) in enumerate(static_steps, start=2 * n_pairs):
        if n + 1 < n_steps:
            scores(*static_steps[n + 1 - 2 * n_pairs], bufs[(n + 1) % 2])
        consume(bufs[n % 2], j, i, masked=(j == i))

    lamp = lamp_ref[...]
    lam = (jnp.exp(jnp.sum(lamp[0:1] * lamp[1:2], axis=1, keepdims=True))
           - jnp.exp(jnp.sum(lamp[2:3] * lamp[3:4], axis=1, keepdims=True)) + LAMBDA_INIT)
    for i in range(nq):
        o = acc_ref[i, 0:V_DIM] / acc_ref[i, V_DIM:V_DIM + 1]
        od = o[:, 0:tq] - lam * o[:, tq:2 * tq]
        y = od * lax.rsqrt(jnp.mean(od * od, axis=0, keepdims=True) + SUBLN_EPS) * gs_ref[...] * (1.0 - LAMBDA_INIT)
        o_ref[i * tq:(i + 1) * tq, :] = y.T.astype(BF16)


def _attention(qt, k, vt, lamp, g_subln, bsz, seq):
    t = k.shape[0]
    tq = min(512, seq)
    nq = seq // tq
    off, diag = _attn_steps(nq)
    steps = jnp.asarray(list(zip(*(off + diag))), I32)
    return pl.pallas_call(
        _attn_kernel,
        grid_spec=pltpu.PrefetchScalarGridSpec(
            num_scalar_prefetch=1,
            grid=(bsz, N_HEADS),
            in_specs=[pl.BlockSpec((1, LANES, seq), lambda b, h, st: (b, h, 0)),
                      pl.BlockSpec((seq, LANES), lambda b, h, st: (b, h)),
                      pl.BlockSpec((1, V_DIM, seq), lambda b, h, st: (b, h, 0)),
                      pl.BlockSpec((4, HEAD_DIM), lambda b, h, st: (0, 0)),
                      pl.BlockSpec((V_DIM, 1), lambda b, h, st: (0, 0))],
            out_specs=pl.BlockSpec((seq, LANES), lambda b, h, st: (b, h)),
            scratch_shapes=[pltpu.VMEM((LANES, 2 * seq), BF16), pltpu.VMEM((nq, 1, 2 * tq), F32),
                            pltpu.VMEM((nq, V_DIM + ATTN_ONES_ROWS, 2 * tq), F32),
                            pltpu.VMEM((tq, 2 * tq), F32), pltpu.VMEM((tq, 2 * tq), F32)],
        ),
        out_shape=jax.ShapeDtypeStruct((t, D_ATTN), BF16),
        compiler_params=_cparams(("arbitrary", "arbitrary")),
        name="attention",
    )(steps, qt, k, vt, lamp, g_subln)


def _first_index(hit, idx, size, axis):
    return jnp.min(jnp.where(hit, idx, size), axis=axis, keepdims=True)


def _outproj_router_kernel(yc_ref, ya_ref, x_ref, ada_ref, g2_ref, wout_ref, wrt_ref, rb_ref,
                           x1_ref, hp_ref, idx_ref, wts_ref, rank_ref, cnt_ref, base_ref):
    i = pl.program_id(0)
    tm = x_ref.shape[0]

    @pl.when(i == 0)
    def _():
        base_ref[...] = jnp.zeros(base_ref.shape, F32)

    ada = ada_ref[0]
    gt1, sh2, sc2 = ada[2:3], ada[3:4], ada[4:5]
    mix = (jnp.dot(yc_ref[...], wout_ref[0:D_CONV, :], preferred_element_type=F32)
           + jnp.dot(ya_ref[...], wout_ref[D_CONV:, :], preferred_element_type=F32))
    x1 = x_ref[...] + gt1 * mix
    x1_ref[...] = x1
    hn2 = _rms(x1, NORM_EPS) * g2_ref[...] * (1.0 + sc2) + sh2
    hp_ref[...] = _pack_bf16_pairs(hn2)
    hb = hn2.astype(BF16)

    logits = lax.dot_general(wrt_ref[...], hb, (((1,), (1,)), ((), ())), preferred_element_type=F32)
    scores = jax.nn.sigmoid(logits)
    choice = scores + rb_ref[...]
    neg_inf = jnp.float32(-jnp.inf)

    ch3 = choice.reshape(N_GROUPS, GROUP_SIZE, tm)
    i3 = lax.broadcasted_iota(I32, ch3.shape, 1)
    m1 = jnp.max(ch3, axis=1, keepdims=True)
    f1 = _first_index(ch3 == m1, i3, GROUP_SIZE, 1)
    m2 = jnp.max(jnp.where(i3 == f1, neg_inf, ch3), axis=1, keepdims=True)
    gs = (m1 + m2).reshape(N_GROUPS, tm)

    gi = lax.broadcasted_iota(I32, gs.shape, 0)
    gkeep = jnp.zeros(gs.shape, F32)
    for _ in range(TOPK_GROUPS):
        m = jnp.max(gs, axis=0, keepdims=True)
        f = _first_index(gs == m, gi, N_GROUPS, 0)
        sel = gi == f
        gkeep = jnp.where(sel, 1.0, gkeep)
        gs = jnp.where(sel, neg_inf, gs)
    ekeep = jnp.broadcast_to(gkeep.reshape(N_GROUPS, 1, tm), (N_GROUPS, GROUP_SIZE, tm)).reshape(N_EXPERTS, tm)
    masked = jnp.where(ekeep > 0.0, choice, neg_inf)

    ei = lax.broadcasted_iota(I32, masked.shape, 0)
    idxs, ws = [], []
    for _ in range(TOP_K):
        m = jnp.max(masked, axis=0, keepdims=True)
        f = _first_index(masked == m, ei, N_EXPERTS, 0)
        sel = ei == f
        idxs.append(f)
        ws.append(jnp.sum(jnp.where(sel, scores, 0.0), axis=0, keepdims=True))
        masked = jnp.where(sel, neg_inf, masked)
    picked = jnp.where(ekeep > 0.0, jnp.where(masked == neg_inf, 1.0, 0.0), 0.0)
    wsum = ws[0]
    for wk in ws[1:]:
        wsum = wsum + wk
    denom = wsum + 1e-20
    idx_all = jnp.concatenate(idxs, axis=0)
    for cc in range(tm // SC_CHUNK):
        idx_ref[cc] = idx_all[:, cc * SC_CHUNK:(cc + 1) * SC_CHUNK]
    wrep = jnp.concatenate([jnp.broadcast_to(ws[kk] / denom * ROUTED_SCALE, (SC_LANES, tm)) for kk in range(TOP_K)],
                           axis=0)
    wts_ref[...] = wrep.T

    si = lax.broadcasted_iota(I32, (tm, tm), 0)
    ti = lax.broadcasted_iota(I32, (tm, tm), 1)
    earlier = jnp.where(si < ti, 1.0, 0.0).astype(BF16)
    before = jnp.dot(picked.astype(BF16), earlier, preferred_element_type=F32) + base_ref[...]
    rank_all = jnp.concatenate(
        [jnp.sum(jnp.where(ei == idxs[kk], before, 0.0), axis=0, keepdims=True) for kk in range(TOP_K)],
        axis=0).astype(I32)
    for cc in range(tm // SC_CHUNK):
        rank_ref[cc] = rank_all[:, cc * SC_CHUNK:(cc + 1) * SC_CHUNK]
    base_ref[...] = base_ref[...] + jnp.sum(picked, axis=1, keepdims=True)
    cnt_ref[...] = base_ref[...]


def _outproj_router(yconv, yattn, xf, ada3, g2, wout_b, wrt_b, rbias, seq):
    t, d = xf.shape
    tm = min(512, seq)
    tiles_per_seq = seq // tm
    full = lambda shape: pl.BlockSpec(shape, lambda i: (0,) * len(shape))
    row = lambda w: pl.BlockSpec((tm, w), lambda i: (i, 0))
    col = pl.BlockSpec((tm // SC_CHUNK, TOP_K, SC_CHUNK), lambda i: (i, 0, 0))
    return pl.pallas_call(
        _outproj_router_kernel,
        grid=(t // tm,),
        in_specs=[row(D_CONV), row(D_ATTN), row(d),
                  pl.BlockSpec((1, 6, d), lambda i: (i // tiles_per_seq, 0, 0)),
                  full((1, d)), full((D_CONV + D_ATTN, d)), full((N_EXPERTS, d)), full((N_EXPERTS, 1))],
        out_specs=[row(d), row(d // 2), col, row(TOP_K * SC_LANES), col, full((N_EXPERTS, 1))],
        out_shape=[jax.ShapeDtypeStruct((t, d), F32), jax.ShapeDtypeStruct((t, d // 2), I32),
                   jax.ShapeDtypeStruct((t // SC_CHUNK, TOP_K, SC_CHUNK), I32),
                   jax.ShapeDtypeStruct((t, TOP_K * SC_LANES), F32),
                   jax.ShapeDtypeStruct((t // SC_CHUNK, TOP_K, SC_CHUNK), I32),
                   jax.ShapeDtypeStruct((N_EXPERTS, 1), F32)],
        scratch_shapes=[pltpu.VMEM((N_EXPERTS, 1), F32)],
        compiler_params=_cparams(("arbitrary",)),
        name="outproj_router",
    )(yconv, yattn, xf, ada3, g2, wout_b, wrt_b, rbias)


def _sc_dispatch(hp, idx_ck, rank_ck, pad_start, n_rows):
    t, w = hp.shape
    n_chunks, nk, _ = idx_ck.shape
    per_worker = n_chunks // SC_WORKERS
    mesh = plsc.VectorSubcoreMesh(core_axis_name="c", subcore_axis_name="s")

    @functools.partial(
        pl.kernel, mesh=mesh,
        out_type=(jax.ShapeDtypeStruct((n_rows, w), hp.dtype), jax.ShapeDtypeStruct(idx_ck.shape, I32)),
        scratch_types=[pltpu.VMEM(pad_start.shape, I32), pltpu.VMEM((nk, SC_CHUNK), I32),
                       pltpu.VMEM((nk, SC_CHUNK), I32), pltpu.VMEM((nk, SC_CHUNK), I32),
                       pltpu.VMEM((SC_CHUNK, w), hp.dtype), pltpu.SemaphoreType.DMA],
        compiler_params=pltpu.CompilerParams(needs_layout_passes=False),
    )
    def k(hp_hbm, idx_hbm, rank_hbm, ps_hbm, xs_hbm, dest_hbm, ps_v, idx_v, rank_v, dest_v, rows_v, sem):
        wid = lax.axis_index("s") * SC_CORES + lax.axis_index("c")
        pltpu.sync_copy(ps_hbm, ps_v)

        @pl.loop(0, per_worker)
        def _(j):
            c = wid * per_worker + j
            pltpu.sync_copy(idx_hbm.at[c], idx_v)
            pltpu.sync_copy(rank_hbm.at[c], rank_v)
            pltpu.sync_copy(hp_hbm.at[pl.ds(c * SC_CHUNK, SC_CHUNK)], rows_v)
            for kk in range(nk):
                for v in range(SC_CHUNK // SC_LANES):
                    lanes = pl.ds(v * SC_LANES, SC_LANES)
                    dest_v[kk, lanes] = plsc.load_gather(ps_v, [idx_v[kk, lanes]]) + rank_v[kk, lanes]
            for kk in range(nk):
                pltpu.async_copy(rows_v, xs_hbm.at[dest_v.at[kk]], sem)
            pltpu.sync_copy(dest_v, dest_hbm.at[c])
            for kk in range(nk):
                pltpu.make_async_copy(rows_v, xs_hbm.at[dest_v.at[kk]], sem).wait()

    return k(hp, idx_ck, rank_ck, pad_start)


def _sc_combine(src, dest_ck, wrep, n_tokens):
    half = src.shape[1]
    lanes = wrep.shape[1] // TOP_K
    ctok = COMBINE_SLOTS // TOP_K
    per_worker = n_tokens // ctok // SC_WORKERS
    steps_per_chunk = SC_CHUNK // ctok
    chunks_per_worker = per_worker // steps_per_chunk
    mesh = plsc.VectorSubcoreMesh(core_axis_name="c", subcore_axis_name="s")

    @functools.partial(
        pl.kernel, mesh=mesh,
        out_type=jax.ShapeDtypeStruct((n_tokens, 2 * half), F32),
        scratch_types=[pltpu.VMEM((chunks_per_worker, TOP_K, SC_CHUNK), I32),
                       pltpu.VMEM((COMBINE_BUFS, COMBINE_SLOTS, half), I32),
                       pltpu.VMEM((COMBINE_BUFS, ctok, TOP_K * lanes), F32),
                       pltpu.VMEM((COMBINE_BUFS, ctok, 2 * half), F32),
                       pltpu.SemaphoreType.DMA((COMBINE_BUFS,)), pltpu.SemaphoreType.DMA((COMBINE_BUFS,)),
                       pltpu.SemaphoreType.DMA((COMBINE_BUFS,))],
        compiler_params=pltpu.CompilerParams(needs_layout_passes=False),
    )
    def k(src_hbm, idx_hbm, w_hbm, y_hbm, idx_v, rows_v, w_v, out_v, sem_g, sem_w, sem_o):
        wid = lax.axis_index("s") * SC_CORES + lax.axis_index("c")
        base = wid * per_worker
        pltpu.sync_copy(idx_hbm.at[pl.ds(wid * chunks_per_worker, chunks_per_worker)], idx_v)

        class _Gather:
            def __init__(self, c, slot):
                chunk = c // steps_per_chunk
                first = (c % steps_per_chunk) * ctok
                self.copies = [
                    pltpu.make_async_copy(src_hbm.at[idx_v.at[chunk, kk, pl.ds(first, ctok)]],
                                          rows_v.at[slot, pl.ds(kk * ctok, ctok)], sem_g.at[slot])
                    for kk in range(TOP_K)]

            def start(self):
                for cp in self.copies:
                    cp.start()

            def wait(self):
                for cp in self.copies:
                    cp.wait()

        gather = _Gather

        def wload(c, slot):
            return pltpu.make_async_copy(w_hbm.at[pl.ds((base + c) * ctok, ctok)], w_v.at[slot], sem_w.at[slot])

        def store(c, slot):
            return pltpu.make_async_copy(out_v.at[slot], y_hbm.at[pl.ds((base + c) * ctok, ctok)], sem_o.at[slot])

        nbuf = COMBINE_BUFS
        for c in range(nbuf - 1):
            gather(c, c).start()
            wload(c, c).start()

        @pl.loop(0, pl.cdiv(per_worker, nbuf) * nbuf, step=nbuf)
        def _(c0):
            for slot in range(nbuf):
                c = c0 + slot

                @pl.when(c < per_worker)
                def _():
                    gather(c, slot).wait()
                    wload(c, slot).wait()

                    @pl.when(c + nbuf - 1 < per_worker)
                    def _():
                        gather(c + nbuf - 1, (slot + nbuf - 1) % nbuf).start()
                        wload(c + nbuf - 1, (slot + nbuf - 1) % nbuf).start()

                    @pl.when(c >= nbuf)
                    def _():
                        store(c - nbuf, slot).wait()

                    @pl.loop(0, ctok)
                    def _(tt):
                        ws = [w_v[slot, tt, pl.ds(kk * lanes, lanes)] for kk in range(TOP_K)]

                        @plsc.parallel_loop(0, half // lanes, unroll=2)
                        def _(v):
                            lo = jnp.zeros((lanes,), F32)
                            hi = jnp.zeros((lanes,), F32)
                            for kk in range(TOP_K):
                                x = rows_v[slot, kk * ctok + tt, pl.ds(v * lanes, lanes)]
                                x_lo, x_hi = plsc.unpack(plsc.bitcast(x, BF16), format=plsc.PackFormat.INTERLEAVED)
                                lo = lo + x_lo * ws[kk]
                                hi = hi + x_hi * ws[kk]
                            out_v[slot, tt, pl.ds(v * lanes, lanes)] = lo
                            out_v[slot, tt, pl.ds(half + v * lanes, lanes)] = hi

                    store(c, slot).start()

        for c in range(per_worker - nbuf, per_worker):
            store(c, c % nbuf).wait()

    return k(src, dest_ck, wrep)


def _experts_kernel(bstart_ref, xs_hbm, wg_ref, wu_ref, wd_ref, out_hbm,
                    wgu_b, wd_b, xbuf, obuf, sem_in, sem_out):
    step = pl.program_id(0)
    n_used = bstart_ref[N_EXPERTS]
    nbuf = EXPERT_BUFS

    def in_copy(g, slot):
        return pltpu.make_async_copy(xs_hbm.at[pl.ds(g * ROW_TILE, ROW_TILE)], xbuf.at[slot], sem_in.at[slot])

    def out_copy(g, slot):
        return pltpu.make_async_copy(obuf.at[slot], out_hbm.at[pl.ds(g * ROW_TILE, ROW_TILE)], sem_out.at[slot])

    def fetch(g):
        @pl.when(g < n_used)
        def _():
            in_copy(g, lax.rem(g, nbuf)).start()

    def run(blocks):
        slots = [lax.rem(g, nbuf) for g in blocks]
        for g, slot in zip(blocks, slots):
            in_copy(g, slot).wait()
        fetch(blocks[0] + nbuf - 1)
        for g, slot in zip(blocks, slots):
            @pl.when(g >= nbuf)
            def _():
                out_copy(g - nbuf, slot).wait()
        xs_rows = [_unpack_bf16_pairs(xbuf[slot]).astype(BF16) for slot in slots]
        xb = xs_rows[0] if len(slots) == 1 else jnp.concatenate(xs_rows, axis=0)
        gu = jnp.dot(xb, wgu_b[...], preferred_element_type=F32)
        act = (jax.nn.silu(gu[:, 0:D_EXPERT]) * gu[:, D_EXPERT:2 * D_EXPERT]).astype(BF16)
        out = _pack_bf16_pairs(jnp.dot(act, wd_b[...], preferred_element_type=F32))
        for n, slot in enumerate(slots):
            obuf[slot] = out[n * ROW_TILE:(n + 1) * ROW_TILE]
        for g, slot in zip(blocks, slots):
            out_copy(g, slot).start()
        for g in blocks[:-1]:
            fetch(g + nbuf)

    @pl.when(step == 0)
    def _():
        for g in range(nbuf - 1):
            @pl.when(g < n_used)
            def _():
                in_copy(g, g).start()

    for sub in range(EXPERTS_PER_STEP):
        e = step * EXPERTS_PER_STEP + sub
        b0 = bstart_ref[e]
        b1 = bstart_ref[e + 1]

        @pl.when(b1 > b0)
        def _():
            wgu_b[:, 0:D_EXPERT] = wg_ref[sub].astype(BF16)
            wgu_b[:, D_EXPERT:2 * D_EXPERT] = wu_ref[sub].astype(BF16)
            wd_b[...] = wd_ref[sub].astype(BF16)
            group = EXPERT_GROUP

            def full_group(jj, carry):
                g = b0 + group * jj
                run([g + n for n in range(group)])
                return carry

            n_full = (b1 - b0) // group
            lax.fori_loop(0, n_full, full_group, 0)
            rest = (b1 - b0) - n_full * group
            tail = b0 + n_full * group
            for size in range(1, group):
                @pl.when(rest == size)
                def _():
                    run([tail + n for n in range(size)])

    @pl.when(step == pl.num_programs(0) - 1)
    def _():
        for back in range(1, nbuf + 1):
            @pl.when(n_used >= back)
            def _():
                g = n_used - back
                out_copy(g, lax.rem(g, nbuf)).wait()


def _experts(xs, bstart, w_gate, w_up, w_down):
    n_rows, half = xs.shape
    d = 2 * half
    weights = lambda shape: pl.BlockSpec((EXPERTS_PER_STEP,) + shape, lambda s, bs: (s, 0, 0))
    return pl.pallas_call(
        _experts_kernel,
        grid_spec=pltpu.PrefetchScalarGridSpec(
            num_scalar_prefetch=1,
            grid=(N_EXPERTS // EXPERTS_PER_STEP,),
            in_specs=[pl.BlockSpec(memory_space=pl.ANY),
                      weights((d, D_EXPERT)), weights((d, D_EXPERT)), weights((D_EXPERT, d))],
            out_specs=pl.BlockSpec(memory_space=pl.ANY),
            scratch_shapes=[pltpu.VMEM((d, 2 * D_EXPERT), BF16), pltpu.VMEM((D_EXPERT, d), BF16),
                            pltpu.VMEM((EXPERT_BUFS, ROW_TILE, half), I32),
                            pltpu.VMEM((EXPERT_BUFS, ROW_TILE, half), I32),
                            pltpu.SemaphoreType.DMA((EXPERT_BUFS,)), pltpu.SemaphoreType.DMA((EXPERT_BUFS,))],
        ),
        out_shape=jax.ShapeDtypeStruct((n_rows, half), I32),
        compiler_params=_cparams(("arbitrary",)),
        name="experts",
    )(bstart, xs, w_gate, w_up, w_down)


def _final_kernel(x1_ref, hp_ref, yr_ref, ada_ref, wgs_ref, wus_ref, wds_ref, gf_ref, o_ref):
    gt2 = ada_ref[0][5:6]
    hb = _unpack_bf16_pairs(hp_ref[...]).astype(BF16)
    g = jnp.dot(hb, wgs_ref[...], preferred_element_type=F32)
    u = jnp.dot(hb, wus_ref[...], preferred_element_type=F32)
    y = jnp.dot((jax.nn.silu(g) * u).astype(BF16), wds_ref[...], preferred_element_type=F32)
    x2 = x1_ref[...] + gt2 * (yr_ref[...] + y)
    o_ref[...] = _rms(x2, NORM_EPS) * gf_ref[...]


def _final(x1, hp, y_routed, ada3, wgs_b, wus_b, wds_b, g_final, seq):
    t, d = x1.shape
    tm = min(512, seq)
    tiles_per_seq = seq // tm
    full = lambda shape: pl.BlockSpec(shape, lambda i: (0,) * len(shape))
    row = lambda w: pl.BlockSpec((tm, w), lambda i: (i, 0))
    return pl.pallas_call(
        _final_kernel,
        grid=(t // tm,),
        in_specs=[row(d), row(d // 2), row(d),
                  pl.BlockSpec((1, 6, d), lambda i: (i // tiles_per_seq, 0, 0)),
                  full((d, D_EXPERT)), full((d, D_EXPERT)), full((D_EXPERT, d)), full((1, d))],
        out_specs=row(d),
        out_shape=jax.ShapeDtypeStruct((t, d), F32),
        compiler_params=_cparams(("arbitrary",)),
        name="final",
    )(x1, hp, y_routed, ada3, wgs_b, wus_b, wds_b, g_final)


def kernel(x, c, positions, w_ada, b_ada, g_norm1, w_in, conv_w, g_conv_out, lam_q1, lam_k1, lam_q2, lam_k2, g_subln, w_out, g_norm2, w_router, router_bias, w_gate_e, w_up_e, w_down_e, w_gate_s, w_up_s, w_down_s, g_final):
    bsz, seq, d = x.shape
    t = bsz * seq
    xf = x.reshape(t, d)
    pos = positions.reshape(1, t)
    invf = (ROPE_THETA ** (-jnp.arange(0, ROT_DIM, 2, dtype=F32) / ROT_DIM)).reshape(ROT_DIM // 2, 1)

    ada3 = _ada(c, w_ada[0], b_ada[0]).reshape(bsz, 6, d)
    yconv, qt, k, vt = _inproj(xf, pos, ada3, g_norm1[0].reshape(1, d), w_in[0].astype(BF16), conv_w[0],
                             g_conv_out[0].reshape(1, D_CONV), invf, seq)
    lamp = jnp.stack([lam_q1[0], lam_k1[0], lam_q2[0], lam_k2[0]]).astype(F32)
    yattn = _attention(qt, k, vt, lamp, g_subln[0].reshape(V_DIM, 1), bsz, seq)

    x1, hp, idx_ck, wrep, rank_ck, counts_f = _outproj_router(
        yconv, yattn, xf, ada3, g_norm2[0].reshape(1, d), w_out[0].astype(BF16),
        w_router[0].T.astype(BF16), router_bias[0].reshape(N_EXPERTS, 1), seq)

    counts = counts_f[:, 0].astype(I32)
    padded = ((counts + ROW_TILE - 1) // ROW_TILE) * ROW_TILE
    pad_end = jnp.cumsum(padded)
    pad_start = pad_end - padded
    nb = (t * TOP_K + N_EXPERTS * (ROW_TILE - 1)) // ROW_TILE
    bstart = (jnp.concatenate([pad_start, pad_end[-1:]]) // ROW_TILE).astype(I32)

    xs, dest_ck = _sc_dispatch(hp, idx_ck, rank_ck, pad_start.astype(I32), nb * ROW_TILE)
    outs = _experts(xs, bstart, w_gate_e[0], w_up_e[0], w_down_e[0])
    y_routed = _sc_combine(outs, dest_ck, wrep, t)

    out = _final(x1, hp, y_routed, ada3,
                 w_gate_s[0].astype(BF16), w_up_s[0].astype(BF16), w_down_s[0].astype(BF16),
                 g_final.reshape(1, d), seq)
    return out.reshape(bsz, seq, d)
```

```python
import functools
import math

import jax
import jax.numpy as jnp
from jax import lax
from jax.experimental import pallas as pl
from jax.experimental.pallas import tpu as pltpu
from jax.experimental.pallas import tpu_sc as plsc

F32 = jnp.float32
BF16 = jnp.bfloat16
I32 = jnp.int32

D_CONV = 512
CONV_WIDTH = 3
N_HEADS = 4
HEAD_DIM = 64
V_DIM = 2 * HEAD_DIM
D_ATTN = N_HEADS * V_DIM
D_QK = N_HEADS * 2 * HEAD_DIM
ROT_DIM = HEAD_DIM // 4
ROPE_THETA = 500000.0
N_EXPERTS = 256
TOP_K = 8
N_GROUPS = 8
GROUP_SIZE = N_EXPERTS // N_GROUPS
TOPK_GROUPS = 4
D_EXPERT = 256
ROUTED_SCALE = 2.5
NORM_EPS = 1e-6
SUBLN_EPS = 1e-5
LAMBDA_INIT = 0.8 - 0.6 * math.exp(-0.3 * 0)

LANES = 128
SUBLANES = 8
SC_CORES = 2
SC_SUBCORES = 16
SC_WORKERS = SC_CORES * SC_SUBCORES
SC_CHUNK = 128
COMBINE_SLOTS = 64
COMBINE_BUFS = 2
SC_LANES = 16

ROW_TILE = 128
EXPERT_GROUP = 4
EXPERTS_PER_STEP = 2
EXPERT_BUFS = 12
ATTN_ONES_ROWS = 16
NEG_BIG = -1e30
VMEM_LIMIT = 56 * 1024 * 1024


def _cparams(sem):
    return pltpu.CompilerParams(dimension_semantics=sem, vmem_limit_bytes=VMEM_LIMIT)


def _rms(x, eps):
    return x * lax.rsqrt(jnp.mean(x * x, axis=-1, keepdims=True) + eps)


def _pack_bf16_pairs(x):
    n = x.shape[1] // 2
    bits = lax.bitcast_convert_type(x.astype(BF16).astype(F32), I32)
    lo = lax.shift_right_logical(bits[:, :n], 16)
    return lo | bits[:, n:]


def _unpack_bf16_pairs(p):
    lo = lax.bitcast_convert_type(lax.shift_left(p, 16), F32)
    hi = lax.bitcast_convert_type(p & jnp.int32(-65536), F32)
    return jnp.concatenate([lo, hi], axis=1)


def _ada_kernel(c_ref, w_ref, b_ref, o_ref):
    ca = jax.nn.silu(c_ref[...])
    o_ref[...] = jnp.dot(ca.astype(BF16), w_ref[...].astype(BF16), preferred_element_type=F32) + b_ref[...]


def _ada(c, w_ada, b_ada):
    bsz, d = c.shape
    n = w_ada.shape[1]
    tn = n // 4
    return pl.pallas_call(
        _ada_kernel,
        grid=(n // tn,),
        in_specs=[pl.BlockSpec((bsz, d), lambda j: (0, 0)),
                  pl.BlockSpec((d, tn), lambda j: (0, j)),
                  pl.BlockSpec((1, tn), lambda j: (0, j))],
        out_specs=pl.BlockSpec((bsz, tn), lambda j: (0, j)),
        out_shape=jax.ShapeDtypeStruct((bsz, n), F32),
        compiler_params=_cparams(("arbitrary",)),
        name="ada",
    )(c, w_ada, b_ada.reshape(1, n))


def _inproj_kernel(tiles_per_seq, x_ref, pos_ref, ada_ref, g1_ref, win_ref, convw_ref, gconv_ref, invf_ref,
                   yconv_ref, qt_ref, k_ref, vt_ref, ubuf):
    i = pl.program_id(0)
    tm = x_ref.shape[0]

    @pl.when(i == 0)
    def _():
        ubuf[tm:tm + SUBLANES, :] = jnp.zeros((SUBLANES, D_CONV), F32)

    x = x_ref[...]
    ada = ada_ref[0]
    sh1, sc1 = ada[0:1], ada[1:2]
    hn = _rms(x, NORM_EPS) * g1_ref[...] * (1.0 + sc1) + sh1
    hb = hn.astype(BF16)

    pc = jnp.dot(hb, win_ref[:, 0:3 * D_CONV], preferred_element_type=F32)
    u = pc[:, 2 * D_CONV:3 * D_CONV] * pc[:, 0:D_CONV]
    first = (i % tiles_per_seq) == 0
    ubuf[0:SUBLANES, :] = jnp.where(first, 0.0, ubuf[tm:tm + SUBLANES, :])
    ubuf[SUBLANES:SUBLANES + tm, :] = u
    u1 = ubuf[SUBLANES - 1:SUBLANES - 1 + tm, :]
    u2 = ubuf[SUBLANES - 2:SUBLANES - 2 + tm, :]
    cw = convw_ref[...]
    yc = pc[:, D_CONV:2 * D_CONV] * (cw[0:1] * u2 + cw[1:2] * u1 + cw[2:3] * u)
    yconv_ref[...] = (_rms(yc, NORM_EPS) * gconv_ref[...]).astype(BF16)

    half = ROT_DIM // 2
    ang = invf_ref[...] * pos_ref[...].astype(F32)
    cos_ft = jnp.cos(ang)
    sin_ft = jnp.sin(ang)
    zero_h = jnp.zeros((half, tm), F32)
    zero_r = jnp.zeros((HEAD_DIM - ROT_DIM, tm), F32)
    lay = lambda a, b, r: jnp.concatenate([a, b, r, a, b, r], axis=0).T
    cos = lay(cos_ft, cos_ft, zero_r + 1.0)
    s_up = lay(-sin_ft, zero_h, zero_r)
    s_dn = lay(zero_h, sin_ft, zero_r)

    def rope(t):
        return t * cos + pltpu.roll(t, LANES - half, axis=1) * s_up + pltpu.roll(t, half, axis=1) * s_dn

    pq = jnp.dot(hb, win_ref[:, 3 * D_CONV:3 * D_CONV + D_QK], preferred_element_type=F32)
    pk = jnp.dot(hb, win_ref[:, 3 * D_CONV + D_QK:3 * D_CONV + 2 * D_QK], preferred_element_type=F32)
    scale = HEAD_DIM ** -0.5 * math.log2(math.e)
    qs = []
    for h in range(N_HEADS):
        sl = slice(h * LANES, (h + 1) * LANES)
        qs.append(rope(pq[:, sl]) * scale)
        k_ref[:, sl] = rope(pk[:, sl]).astype(BF16)
    qt_ref[0] = jnp.concatenate(qs, axis=1).T.astype(BF16)
    pv = jnp.dot(hb, win_ref[:, 3 * D_CONV + 2 * D_QK:], preferred_element_type=F32)
    vt_ref[0] = pv.T.astype(BF16)


def _inproj(xf, pos, ada3, g1, win_b, conv_w, g_conv, invf, seq):
    t, d = xf.shape
    tm = min(1024, seq)
    tiles_per_seq = seq // tm
    n_in = win_b.shape[1]
    full = lambda shape: pl.BlockSpec(shape, lambda i: (0,) * len(shape))
    row = lambda w: pl.BlockSpec((tm, w), lambda i: (i, 0))
    colmajor = lambda w: pl.BlockSpec((1, w, tm), lambda i: (i // tiles_per_seq, 0, i % tiles_per_seq))
    return pl.pallas_call(
        functools.partial(_inproj_kernel, tiles_per_seq),
        grid=(t // tm,),
        in_specs=[row(d), pl.BlockSpec((1, tm), lambda i: (0, i)),
                  pl.BlockSpec((1, 6, d), lambda i: (i // tiles_per_seq, 0, 0)),
                  full((1, d)), full((d, n_in)), full((CONV_WIDTH, D_CONV)), full((1, D_CONV)),
                  full((ROT_DIM // 2, 1))],
        out_specs=[row(D_CONV), colmajor(D_QK), row(D_QK), colmajor(D_ATTN)],
        out_shape=[jax.ShapeDtypeStruct((t, D_CONV), BF16), jax.ShapeDtypeStruct((t // seq, D_QK, seq), BF16),
                   jax.ShapeDtypeStruct((t, D_QK), BF16), jax.ShapeDtypeStruct((t // seq, D_ATTN, seq), BF16)],
        scratch_shapes=[pltpu.VMEM((tm + SUBLANES, D_CONV), F32)],
        compiler_params=_cparams(("arbitrary",)),
        name="inproj",
    )(xf, pos, ada3, g1, win_b, conv_w, g_conv, invf)


def _attn_steps(nq):
    off = [(j, i) for j in range(nq) for i in range(j + 1, nq)]
    diag = [(i, i) for i in range(nq)]
    return off, diag


def _attn_kernel(steps_ref, qt_ref, k_ref, vt_ref, lamp_ref, gs_ref, o_ref,
                 qq_ref, m_ref, acc_ref, s0_ref, s1_ref):
    nq = m_ref.shape[0]
    tq = s0_ref.shape[0]
    tk = tq
    off, diag = _attn_steps(nq)
    n_steps = len(off) + len(diag)
    bufs = (s0_ref, s1_ref)

    feat = lax.broadcasted_iota(I32, (LANES, tq), 0)
    for i in range(nq):
        qt = qt_ref[0, :, i * tq:(i + 1) * tq]
        zero = jnp.zeros_like(qt)
        qq_ref[:, 2 * i * tq:(2 * i + 1) * tq] = jnp.where(feat < HEAD_DIM, qt, zero)
        qq_ref[:, (2 * i + 1) * tq:(2 * i + 2) * tq] = jnp.where(feat >= HEAD_DIM, qt, zero)
    m_ref[...] = jnp.full(m_ref.shape, NEG_BIG, F32)
    acc_ref[...] = jnp.zeros(acc_ref.shape, F32)
    ones_rows = jnp.ones((ATTN_ONES_ROWS, tk), BF16)

    def scores(j, i, dst):
        kstart = pl.multiple_of(j * tk, tk)
        qstart = pl.multiple_of(i * (2 * tq), 2 * tq)
        dst[...] = jnp.dot(k_ref[pl.ds(kstart, tk), :], qq_ref[:, pl.ds(qstart, 2 * tq)],
                           preferred_element_type=F32)

    def consume(src, j, i, masked):
        kstart = pl.multiple_of(j * tk, tk)
        vtb = jnp.concatenate([vt_ref[0, :, pl.ds(kstart, tk)], ones_rows], axis=0)
        s = src[...]
        if masked:
            key = lax.broadcasted_iota(I32, s.shape, 0)
            col = lax.broadcasted_iota(I32, s.shape, 1)
            qpos = jnp.where(col >= tq, col - tq, col)
            s = jnp.where(key <= qpos, s, NEG_BIG)
        m_prev = m_ref[i]
        m_new = jnp.maximum(m_prev, jnp.max(s, axis=0, keepdims=True))
        alpha = jnp.exp2(m_prev - m_new)
        p = jnp.exp2(s - m_new)
        acc_ref[i] = alpha * acc_ref[i] + jnp.dot(vtb, p.astype(BF16), preferred_element_type=F32)
        m_ref[i] = m_new

    def at(n):
        return steps_ref[0, n], steps_ref[1, n]

    scores(*at(0), s0_ref)
    n_pairs = len(off) // 2

    def pair(pp, carry):
        n = 2 * pp
        scores(*at(n + 1), s1_ref)
        consume(s0_ref, *at(n), False)
        scores(*at(n + 2), s0_ref)
        consume(s1_ref, *at(n + 1), False)
        return carry

    lax.fori_loop(0, n_pairs, pair, 0)

    static_steps = (off + diag)[2 * n_pairs:]
    for n, (j, i) in enumerate(static_steps, start=2 * n_pairs):
        if n + 1 < n_steps:
            scores(*static_steps[n + 1 - 2 * n_pairs], bufs[(n + 1) % 2])
        consume(bufs[n % 2], j, i, masked=(j == i))

    lamp = lamp_ref[...]
    lam = (jnp.exp(jnp.sum(lamp[0:1] * lamp[1:2], axis=1, keepdims=True))
           - jnp.exp(jnp.sum(lamp[2:3] * lamp[3:4], axis=1, keepdims=True)) + LAMBDA_INIT)
    for i in range(nq):
        o = acc_ref[i, 0:V_DIM] / acc_ref[i, V_DIM:V_DIM + 1]
        od = o[:, 0:tq] - lam * o[:, tq:2 * tq]
        y = od * lax.rsqrt(jnp.mean(od * od, axis=0, keepdims=True) + SUBLN_EPS) * gs_ref[...] * (1.0 - LAMBDA_INIT)
        o_ref[i * tq:(i + 1) * tq, :] = y.T.astype(BF16)


def _attention(qt, k, vt, lamp, g_subln, bsz, seq):
    t = k.shape[0]
    tq = min(512, seq)
    nq = seq // tq
    off, diag = _attn_steps(nq)
    steps = jnp.asarray(list(zip(*(off + diag))), I32)
    return pl.pallas_call(
        _attn_kernel,
        grid_spec=pltpu.PrefetchScalarGridSpec(
            num_scalar_prefetch=1,
            grid=(bsz, N_HEADS),
            in_specs=[pl.BlockSpec((1, LANES, seq), lambda b, h, st: (b, h, 0)),
                      pl.BlockSpec((seq, LANES), lambda b, h, st: (b, h)),
                      pl.BlockSpec((1, V_DIM, seq), lambda b, h, st: (b, h, 0)),
                      pl.BlockSpec((4, HEAD_DIM), lambda b, h, st: (0, 0)),
                      pl.BlockSpec((V_DIM, 1), lambda b, h, st: (0, 0))],
            out_specs=pl.BlockSpec((seq, LANES), lambda b, h, st: (b, h)),
            scratch_shapes=[pltpu.VMEM((LANES, 2 * seq), BF16), pltpu.VMEM((nq, 1, 2 * tq), F32),
                            pltpu.VMEM((nq, V_DIM + ATTN_ONES_ROWS, 2 * tq), F32),
                            pltpu.VMEM((tq, 2 * tq), F32), pltpu.VMEM((tq, 2 * tq), F32)],
        ),
        out_shape=jax.ShapeDtypeStruct((t, D_ATTN), BF16),
        compiler_params=_cparams(("arbitrary", "arbitrary")),
        name="attention",
    )(steps, qt, k, vt, lamp, g_subln)


def _first_index(hit, idx, size, axis):
    return jnp.min(jnp.where(hit, idx, size), axis=axis, keepdims=True)


def _outproj_router_kernel(yc_ref, ya_ref, x_ref, ada_ref, g2_ref, wout_ref, wrt_ref, rb_ref,
                           x1_ref, hp_ref, idx_ref, wts_ref, rank_ref, cnt_ref, base_ref):
    i = pl.program_id(0)
    tm = x_ref.shape[0]

    @pl.when(i == 0)
    def _():
        base_ref[...] = jnp.zeros(base_ref.shape, F32)

    ada = ada_ref[0]
    gt1, sh2, sc2 = ada[2:3], ada[3:4], ada[4:5]
    mix = (jnp.dot(yc_ref[...], wout_ref[0:D_CONV, :], preferred_element_type=F32)
           + jnp.dot(ya_ref[...], wout_ref[D_CONV:, :], preferred_element_type=F32))
    x1 = x_ref[...] + gt1 * mix
    x1_ref[...] = x1
    hn2 = _rms(x1, NORM_EPS) * g2_ref[...] * (1.0 + sc2) + sh2
    hp_ref[...] = _pack_bf16_pairs(hn2)
    hb = hn2.astype(BF16)

    logits = lax.dot_general(wrt_ref[...], hb, (((1,), (1,)), ((), ())), preferred_element_type=F32)
    scores = jax.nn.sigmoid(logits)
    choice = scores + rb_ref[...]
    neg_inf = jnp.float32(-jnp.inf)

    ch3 = choice.reshape(N_GROUPS, GROUP_SIZE, tm)
    i3 = lax.broadcasted_iota(I32, ch3.shape, 1)
    m1 = jnp.max(ch3, axis=1, keepdims=True)
    f1 = _first_index(ch3 == m1, i3, GROUP_SIZE, 1)
    m2 = jnp.max(jnp.where(i3 == f1, neg_inf, ch3), axis=1, keepdims=True)
    gs = (m1 + m2).reshape(N_GROUPS, tm)

    gi = lax.broadcasted_iota(I32, gs.shape, 0)
    gkeep = jnp.zeros(gs.shape, F32)
    for _ in range(TOPK_GROUPS):
        m = jnp.max(gs, axis=0, keepdims=True)
        f = _first_index(gs == m, gi, N_GROUPS, 0)
        sel = gi == f
        gkeep = jnp.where(sel, 1.0, gkeep)
        gs = jnp.where(sel, neg_inf, gs)
    ekeep = jnp.broadcast_to(gkeep.reshape(N_GROUPS, 1, tm), (N_GROUPS, GROUP_SIZE, tm)).reshape(N_EXPERTS, tm)
    masked = jnp.where(ekeep > 0.0, choice, neg_inf)

    ei = lax.broadcasted_iota(I32, masked.shape, 0)
    idxs, ws = [], []
    for _ in range(TOP_K):
        m = jnp.max(masked, axis=0, keepdims=True)
        f = _first_index(masked == m, ei, N_EXPERTS, 0)
        sel = ei == f
        idxs.append(f)
        ws.append(jnp.sum(jnp.where(sel, scores, 0.0), axis=0, keepdims=True))
        masked = jnp.where(sel, neg_inf, masked)
    picked = jnp.where(ekeep > 0.0, jnp.where(masked == neg_inf, 1.0, 0.0), 0.0)
    wsum = ws[0]
    for wk in ws[1:]:
        wsum = wsum + wk
    denom = wsum + 1e-20
    idx_all = jnp.concatenate(idxs, axis=0)
    for cc in range(tm // SC_CHUNK):
        idx_ref[cc] = idx_all[:, cc * SC_CHUNK:(cc + 1) * SC_CHUNK]
    wrep = jnp.concatenate([jnp.broadcast_to(ws[kk] / denom * ROUTED_SCALE, (SC_LANES, tm)) for kk in range(TOP_K)],
                           axis=0)
    wts_ref[...] = wrep.T

    si = lax.broadcasted_iota(I32, (tm, tm), 0)
    ti = lax.broadcasted_iota(I32, (tm, tm), 1)
    earlier = jnp.where(si < ti, 1.0, 0.0).astype(BF16)
    before = jnp.dot(picked.astype(BF16), earlier, preferred_element_type=F32) + base_ref[...]
    rank_all = jnp.concatenate(
        [jnp.sum(jnp.where(ei == idxs[kk], before, 0.0), axis=0, keepdims=True) for kk in range(TOP_K)],
        axis=0).astype(I32)
    for cc in range(tm // SC_CHUNK):
        rank_ref[cc] = rank_all[:, cc * SC_CHUNK:(cc + 1) * SC_CHUNK]
    base_ref[...] = base_ref[...] + jnp.sum(picked, axis=1, keepdims=True)
    cnt_ref[...] = base_ref[...]


def _outproj_router(yconv, yattn, xf, ada3, g2, wout_b, wrt_b, rbias, seq):
    t, d = xf.shape
    tm = min(512, seq)
    tiles_per_seq = seq // tm
    full = lambda shape: pl.BlockSpec(shape, lambda i: (0,) * len(shape))
    row = lambda w: pl.BlockSpec((tm, w), lambda i: (i, 0))
    col = pl.BlockSpec((tm // SC_CHUNK, TOP_K, SC_CHUNK), lambda i: (i, 0, 0))
    return pl.pallas_call(
        _outproj_router_kernel,
        grid=(t // tm,),
        in_specs=[row(D_CONV), row(D_ATTN), row(d),
                  pl.BlockSpec((1, 6, d), lambda i: (i // tiles_per_seq, 0, 0)),
                  full((1, d)), full((D_CONV + D_ATTN, d)), full((N_EXPERTS, d)), full((N_EXPERTS, 1))],
        out_specs=[row(d), row(d // 2), col, row(TOP_K * SC_LANES), col, full((N_EXPERTS, 1))],
        out_shape=[jax.ShapeDtypeStruct((t, d), F32), jax.ShapeDtypeStruct((t, d // 2), I32),
                   jax.ShapeDtypeStruct((t // SC_CHUNK, TOP_K, SC_CHUNK), I32),
                   jax.ShapeDtypeStruct((t, TOP_K * SC_LANES), F32),
                   jax.ShapeDtypeStruct((t // SC_CHUNK, TOP_K, SC_CHUNK), I32),
                   jax.ShapeDtypeStruct((N_EXPERTS, 1), F32)],
        scratch_shapes=[pltpu.VMEM((N_EXPERTS, 1), F32)],
        compiler_params=_cparams(("arbitrary",)),
        name="outproj_router",
    )(yconv, yattn, xf, ada3, g2, wout_b, wrt_b, rbias)


def _sc_dispatch(hp, idx_ck, rank_ck, pad_start, n_rows):
    t, w = hp.shape
    n_chunks, nk, _ = idx_ck.shape
    per_worker = n_chunks // SC_WORKERS
    mesh = plsc.VectorSubcoreMesh(core_axis_name="c", subcore_axis_name="s")

    @functools.partial(
        pl.kernel, mesh=mesh,
        out_type=(jax.ShapeDtypeStruct((n_rows, w), hp.dtype), jax.ShapeDtypeStruct(idx_ck.shape, I32)),
        scratch_types=[pltpu.VMEM(pad_start.shape, I32), pltpu.VMEM((nk, SC_CHUNK), I32),
                       pltpu.VMEM((nk, SC_CHUNK), I32), pltpu.VMEM((nk, SC_CHUNK), I32),
                       pltpu.VMEM((SC_CHUNK, w), hp.dtype), pltpu.SemaphoreType.DMA],
        compiler_params=pltpu.CompilerParams(needs_layout_passes=False),
    )
    def k(hp_hbm, idx_hbm, rank_hbm, ps_hbm, xs_hbm, dest_hbm, ps_v, idx_v, rank_v, dest_v, rows_v, sem):
        wid = lax.axis_index("s") * SC_CORES + lax.axis_index("c")
        pltpu.sync_copy(ps_hbm, ps_v)

        @pl.loop(0, per_worker)
        def _(j):
            c = wid * per_worker + j
            pltpu.sync_copy(idx_hbm.at[c], idx_v)
            pltpu.sync_copy(rank_hbm.at[c], rank_v)
            pltpu.sync_copy(hp_hbm.at[pl.ds(c * SC_CHUNK, SC_CHUNK)], rows_v)
            for kk in range(nk):
                for v in range(SC_CHUNK // SC_LANES):
                    lanes = pl.ds(v * SC_LANES, SC_LANES)
                    dest_v[kk, lanes] = plsc.load_gather(ps_v, [idx_v[kk, lanes]]) + rank_v[kk, lanes]
            for kk in range(nk):
                pltpu.async_copy(rows_v, xs_hbm.at[dest_v.at[kk]], sem)
            pltpu.sync_copy(dest_v, dest_hbm.at[c])
            for kk in range(nk):
                pltpu.make_async_copy(rows_v, xs_hbm.at[dest_v.at[kk]], sem).wait()

    return k(hp, idx_ck, rank_ck, pad_start)


def _sc_combine(src, dest_ck, wrep, n_tokens):
    half = src.shape[1]
    lanes = wrep.shape[1] // TOP_K
    ctok = COMBINE_SLOTS // TOP_K
    per_worker = n_tokens // ctok // SC_WORKERS
    steps_per_chunk = SC_CHUNK // ctok
    chunks_per_worker = per_worker // steps_per_chunk
    mesh = plsc.VectorSubcoreMesh(core_axis_name="c", subcore_axis_name="s")

    @functools.partial(
        pl.kernel, mesh=mesh,
        out_type=jax.ShapeDtypeStruct((n_tokens, 2 * half), F32),
        scratch_types=[pltpu.VMEM((chunks_per_worker, TOP_K, SC_CHUNK), I32),
                       pltpu.VMEM((COMBINE_BUFS, COMBINE_SLOTS, half), I32),
                       pltpu.VMEM((COMBINE_BUFS, ctok, TOP_K * lanes), F32),
                       pltpu.VMEM((COMBINE_BUFS, ctok, 2 * half), F32),
                       pltpu.SemaphoreType.DMA((COMBINE_BUFS,)), pltpu.SemaphoreType.DMA((COMBINE_BUFS,)),
                       pltpu.SemaphoreType.DMA((COMBINE_BUFS,))],
        compiler_params=pltpu.CompilerParams(needs_layout_passes=False),
    )
    def k(src_hbm, idx_hbm, w_hbm, y_hbm, idx_v, rows_v, w_v, out_v, sem_g, sem_w, sem_o):
        wid = lax.axis_index("s") * SC_CORES + lax.axis_index("c")
        base = wid * per_worker
        pltpu.sync_copy(idx_hbm.at[pl.ds(wid * chunks_per_worker, chunks_per_worker)], idx_v)

        class _Gather:
            def __init__(self, c, slot):
                chunk = c // steps_per_chunk
                first = (c % steps_per_chunk) * ctok
                self.copies = [
                    pltpu.make_async_copy(src_hbm.at[idx_v.at[chunk, kk, pl.ds(first, ctok)]],
                                          rows_v.at[slot, pl.ds(kk * ctok, ctok)], sem_g.at[slot])
                    for kk in range(TOP_K)]

            def start(self):
                for cp in self.copies:
                    cp.start()

            def wait(self):
                for cp in self.copies:
                    cp.wait()

        gather = _Gather

        def wload(c, slot):
            return pltpu.make_async_copy(w_hbm.at[pl.ds((base + c) * ctok, ctok)], w_v.at[slot], sem_w.at[slot])

        def store(c, slot):
            return pltpu.make_async_copy(out_v.at[slot], y_hbm.at[pl.ds((base + c) * ctok, ctok)], sem_o.at[slot])

        nbuf = COMBINE_BUFS
        for c in range(nbuf - 1):
            gather(c, c).start()
            wload(c, c).start()

        @pl.loop(0, pl.cdiv(per_worker, nbuf) * nbuf, step=nbuf)
        def _(c0):
            for slot in range(nbuf):
                c = c0 + slot

                @pl.when(c < per_worker)
                def _():
                    gather(c, slot).wait()
                    wload(c, slot).wait()

                    @pl.when(c + nbuf - 1 < per_worker)
                    def _():
                        gather(c + nbuf - 1, (slot + nbuf - 1) % nbuf).start()
                        wload(c + nbuf - 1, (slot + nbuf - 1) % nbuf).start()

                    @pl.when(c >= nbuf)
                    def _():
                        store(c - nbuf, slot).wait()

                    @pl.loop(0, ctok)
                    def _(tt):
                        ws = [w_v[slot, tt, pl.ds(kk * lanes, lanes)] for kk in range(TOP_K)]

                        @plsc.parallel_loop(0, half // lanes, unroll=2)
                        def _(v):
                            lo = jnp.zeros((lanes,), F32)
                            hi = jnp.zeros((lanes,), F32)
                            for kk in range(TOP_K):
                                x = rows_v[slot, kk * ctok + tt, pl.ds(v * lanes, lanes)]
                                x_lo, x_hi = plsc.unpack(plsc.bitcast(x, BF16), format=plsc.PackFormat.INTERLEAVED)
                                lo = lo + x_lo * ws[kk]
                                hi = hi + x_hi * ws[kk]
                            out_v[slot, tt, pl.ds(v * lanes, lanes)] = lo
                            out_v[slot, tt, pl.ds(half + v * lanes, lanes)] = hi

                    store(c, slot).start()

        for c in range(per_worker - nbuf, per_worker):
            store(c, c % nbuf).wait()

    return k(src, dest_ck, wrep)


def _experts_kernel(bstart_ref, xs_hbm, wg_ref, wu_ref, wd_ref, out_hbm,
                    wgu_b, wd_b, xbuf, obuf, sem_in, sem_out):
    step = pl.program_id(0)
    n_used = bstart_ref[N_EXPERTS]
    nbuf = EXPERT_BUFS

    def in_copy(g, slot):
        return pltpu.make_async_copy(xs_hbm.at[pl.ds(g * ROW_TILE, ROW_TILE)], xbuf.at[slot], sem_in.at[slot])

    def out_copy(g, slot):
        return pltpu.make_async_copy(obuf.at[slot], out_hbm.at[pl.ds(g * ROW_TILE, ROW_TILE)], sem_out.at[slot])

    def fetch(g):
        @pl.when(g < n_used)
        def _():
            in_copy(g, lax.rem(g, nbuf)).start()

    def run(blocks):
        slots = [lax.rem(g, nbuf) for g in blocks]
        for g, slot in zip(blocks, slots):
            in_copy(g, slot).wait()
        fetch(blocks[0] + nbuf - 1)
        for g, slot in zip(blocks, slots):
            @pl.when(g >= nbuf)
            def _():
                out_copy(g - nbuf, slot).wait()
        xs_rows = [_unpack_bf16_pairs(xbuf[slot]).astype(BF16) for slot in slots]
        xb = xs_rows[0] if len(slots) == 1 else jnp.concatenate(xs_rows, axis=0)
        gu = jnp.dot(xb, wgu_b[...], preferred_element_type=F32)
        act = (jax.nn.silu(gu[:, 0:D_EXPERT]) * gu[:, D_EXPERT:2 * D_EXPERT]).astype(BF16)
        out = _pack_bf16_pairs(jnp.dot(act, wd_b[...], preferred_element_type=F32))
        for n, slot in enumerate(slots):
            obuf[slot] = out[n * ROW_TILE:(n + 1) * ROW_TILE]
        for g, slot in zip(blocks, slots):
            out_copy(g, slot).start()
        for g in blocks[:-1]:
            fetch(g + nbuf)

    @pl.when(step == 0)
    def _():
        for g in range(nbuf - 1):
            @pl.when(g < n_used)
            def _():
                in_copy(g, g).start()

    for sub in range(EXPERTS_PER_STEP):
        e = step * EXPERTS_PER_STEP + sub
        b0 = bstart_ref[e]
        b1 = bstart_ref[e + 1]

        @pl.when(b1 > b0)
        def _():
            wgu_b[:, 0:D_EXPERT] = wg_ref[sub].astype(BF16)
            wgu_b[:, D_EXPERT:2 * D_EXPERT] = wu_ref[sub].astype(BF16)
            wd_b[...] = wd_ref[sub].astype(BF16)
            group = EXPERT_GROUP

            def full_group(jj, carry):
                g = b0 + group * jj
                run([g + n for n in range(group)])
                return carry

            n_full = (b1 - b0) // group
            lax.fori_loop(0, n_full, full_group, 0)
            rest = (b1 - b0) - n_full * group
            tail = b0 + n_full * group
            for size in range(1, group):
                @pl.when(rest == size)
                def _():
                    run([tail + n for n in range(size)])

    @pl.when(step == pl.num_programs(0) - 1)
    def _():
        for back in range(1, nbuf + 1):
            @pl.when(n_used >= back)
            def _():
                g = n_used - back
                out_copy(g, lax.rem(g, nbuf)).wait()


def _experts(xs, bstart, w_gate, w_up, w_down):
    n_rows, half = xs.shape
    d = 2 * half
    weights = lambda shape: pl.BlockSpec((EXPERTS_PER_STEP,) + shape, lambda s, bs: (s, 0, 0))
    return pl.pallas_call(
        _experts_kernel,
        grid_spec=pltpu.PrefetchScalarGridSpec(
            num_scalar_prefetch=1,
            grid=(N_EXPERTS // EXPERTS_PER_STEP,),
            in_specs=[pl.BlockSpec(memory_space=pl.ANY),
                      weights((d, D_EXPERT)), weights((d, D_EXPERT)), weights((D_EXPERT, d))],
            out_specs=pl.BlockSpec(memory_space=pl.ANY),
            scratch_shapes=[pltpu.VMEM((d, 2 * D_EXPERT), BF16), pltpu.VMEM((D_EXPERT, d), BF16),
                            pltpu.VMEM((EXPERT_BUFS, ROW_TILE, half), I32),
                            pltpu.VMEM((EXPERT_BUFS, ROW_TILE, half), I32),
                            pltpu.SemaphoreType.DMA((EXPERT_BUFS,)), pltpu.SemaphoreType.DMA((EXPERT_BUFS,))],
        ),
        out_shape=jax.ShapeDtypeStruct((n_rows, half), I32),
        compiler_params=_cparams(("arbitrary",)),
        name="experts",
    )(bstart, xs, w_gate, w_up, w_down)


def _final_kernel(x1_ref, hp_ref, yr_ref, ada_ref, wgs_ref, wus_ref, wds_ref, gf_ref, o_ref):
    gt2 = ada_ref[0][5:6]
    hb = _unpack_bf16_pairs(hp_ref[...]).astype(BF16)
    g = jnp.dot(hb, wgs_ref[...], preferred_element_type=F32)
    u = jnp.dot(hb, wus_ref[...], preferred_element_type=F32)
    y = jnp.dot((jax.nn.silu(g) * u).astype(BF16), wds_ref[...], preferred_element_type=F32)
    x2 = x1_ref[...] + gt2 * (yr_ref[...] + y)
    o_ref[...] = _rms(x2, NORM_EPS) * gf_ref[...]


def _final(x1, hp, y_routed, ada3, wgs_b, wus_b, wds_b, g_final, seq):
    t, d = x1.shape
    tm = min(512, seq)
    tiles_per_seq = seq // tm
    full = lambda shape: pl.BlockSpec(shape, lambda i: (0,) * len(shape))
    row = lambda w: pl.BlockSpec((tm, w), lambda i: (i, 0))
    return pl.pallas_call(
        _final_kernel,
        grid=(t // tm,),
        in_specs=[row(d), row(d // 2), row(d),
                  pl.BlockSpec((1, 6, d), lambda i: (i // tiles_per_seq, 0, 0)),
                  full((d, D_EXPERT)), full((d, D_EXPERT)), full((D_EXPERT, d)), full((1, d))],
        out_specs=row(d),
        out_shape=jax.ShapeDtypeStruct((t, d), F32),
        compiler_params=_cparams(("arbitrary",)),
        name="final",
    )(x1, hp, y_routed, ada3, wgs_b, wus_b, wds_b, g_final)


def kernel(x, c, positions, w_ada, b_ada, g_norm1, w_in, conv_w, g_conv_out, lam_q1, lam_k1, lam_q2, lam_k2, g_subln, w_out, g_norm2, w_router, router_bias, w_gate_e, w_up_e, w_down_e, w_gate_s, w_up_s, w_down_s, g_final):
    bsz, seq, d = x.shape
    t = bsz * seq
    xf = x.reshape(t, d)
    pos = positions.reshape(1, t)
    invf = (ROPE_THETA ** (-jnp.arange(0, ROT_DIM, 2, dtype=F32) / ROT_DIM)).reshape(ROT_DIM // 2, 1)

    ada3 = _ada(c, w_ada[0], b_ada[0]).reshape(bsz, 6, d)
    yconv, qt, k, vt = _inproj(xf, pos, ada3, g_norm1[0].reshape(1, d), w_in[0].astype(BF16), conv_w[0],
                             g_conv_out[0].reshape(1, D_CONV), invf, seq)
    lamp = jnp.stack([lam_q1[0], lam_k1[0], lam_q2[0], lam_k2[0]]).astype(F32)
    yattn = _attention(qt, k, vt, lamp, g_subln[0].reshape(V_DIM, 1), bsz, seq)

    x1, hp, idx_ck, wrep, rank_ck, counts_f = _outproj_router(
        yconv, yattn, xf, ada3, g_norm2[0].reshape(1, d), w_out[0].astype(BF16),
        w_router[0].T.astype(BF16), router_bias[0].reshape(N_EXPERTS, 1), seq)

    counts = counts_f[:, 0].astype(I32)
    padded = ((counts + ROW_TILE - 1) // ROW_TILE) * ROW_TILE
    pad_end = jnp.cumsum(padded)
    pad_start = pad_end - padded
    nb = (t * TOP_K + N_EXPERTS * (ROW_TILE - 1)) // ROW_TILE
    bstart = (jnp.concatenate([pad_start, pad_end[-1:]]) // ROW_TILE).astype(I32)

    xs, dest_ck = _sc_dispatch(hp, idx_ck, rank_ck, pad_start.astype(I32), nb * ROW_TILE)
    outs = _experts(xs, bstart, w_gate_e[0], w_up_e[0], w_down_e[0])
    y_routed = _sc_combine(outs, dest_ck, wrep, t)

    out = _final(x1, hp, y_routed, ada3,
                 w_gate_s[0].astype(BF16), w_up_s[0].astype(BF16), w_down_s[0].astype(BF16),
                 g_final.reshape(1, d), seq)
    return out.reshape(bsz, seq, d)
```

```python
import functools
import math

import jax
import jax.numpy as jnp
from jax import lax
from jax.experimental import pallas as pl
from jax.experimental.pallas import tpu as pltpu
from jax.experimental.pallas import tpu_sc as plsc

F32 = jnp.float32
BF16 = jnp.bfloat16
I32 = jnp.int32

D_CONV = 512
CONV_WIDTH = 3
N_HEADS = 4
HEAD_DIM = 64
V_DIM = 2 * HEAD_DIM
D_ATTN = N_HEADS * V_DIM
D_QK = N_HEADS * 2 * HEAD_DIM
ROT_DIM = HEAD_DIM // 4
ROPE_THETA = 500000.0
N_EXPERTS = 256
TOP_K = 8
N_GROUPS = 8
GROUP_SIZE = N_EXPERTS // N_GROUPS
TOPK_GROUPS = 4
D_EXPERT = 256
ROUTED_SCALE = 2.5
NORM_EPS = 1e-6
SUBLN_EPS = 1e-5
LAMBDA_INIT = 0.8 - 0.6 * math.exp(-0.3 * 0)

LANES = 128
SUBLANES = 8
SC_CORES = 2
SC_SUBCORES = 16
SC_WORKERS = SC_CORES * SC_SUBCORES
SC_CHUNK = 128
COMBINE_SLOTS = 64
COMBINE_BUFS = 2
SC_LANES = 16

ROW_TILE = 128
EXPERT_GROUP = 4
EXPERTS_PER_STEP = 2
EXPERT_WEIGHT_BUFS = 4
EXPERT_BUFS = 12
ATTN_ONES_ROWS = 16
NEG_BIG = -1e30
VMEM_LIMIT = 56 * 1024 * 1024


def _cparams(sem):
    return pltpu.CompilerParams(dimension_semantics=sem, vmem_limit_bytes=VMEM_LIMIT)


def _rms(x, eps):
    return x * lax.rsqrt(jnp.mean(x * x, axis=-1, keepdims=True) + eps)


def _pack_bf16_pairs(x):
    n = x.shape[1] // 2
    bits = lax.bitcast_convert_type(x.astype(BF16).astype(F32), I32)
    lo = lax.shift_right_logical(bits[:, :n], 16)
    return lo | bits[:, n:]


def _unpack_bf16_pairs(p):
    lo = lax.bitcast_convert_type(lax.shift_left(p, 16), F32)
    hi = lax.bitcast_convert_type(p & jnp.int32(-65536), F32)
    return jnp.concatenate([lo, hi], axis=1)


def _ada_kernel(c_ref, w_ref, b_ref, o_ref):
    ca = jax.nn.silu(c_ref[...])
    o_ref[...] = jnp.dot(ca.astype(BF16), w_ref[...].astype(BF16), preferred_element_type=F32) + b_ref[...]


def _ada(c, w_ada, b_ada):
    bsz, d = c.shape
    n = w_ada.shape[1]
    tn = n // 4
    return pl.pallas_call(
        _ada_kernel,
        grid=(n // tn,),
        in_specs=[pl.BlockSpec((bsz, d), lambda j: (0, 0)),
                  pl.BlockSpec((d, tn), lambda j: (0, j)),
                  pl.BlockSpec((1, tn), lambda j: (0, j))],
        out_specs=pl.BlockSpec((bsz, tn), lambda j: (0, j)),
        out_shape=jax.ShapeDtypeStruct((bsz, n), F32),
        compiler_params=_cparams(("arbitrary",)),
        name="ada",
    )(c, w_ada, b_ada.reshape(1, n))


def _inproj_kernel(tiles_per_seq, x_ref, pos_ref, ada_ref, g1_ref, win_ref, convw_ref, gconv_ref, invf_ref,
                   yconv_ref, qt_ref, k_ref, vt_ref, ubuf):
    i = pl.program_id(0)
    tm = x_ref.shape[0]

    @pl.when(i == 0)
    def _():
        ubuf[tm:tm + SUBLANES, :] = jnp.zeros((SUBLANES, D_CONV), F32)

    x = x_ref[...]
    ada = ada_ref[0]
    sh1, sc1 = ada[0:1], ada[1:2]
    hn = _rms(x, NORM_EPS) * g1_ref[...] * (1.0 + sc1) + sh1
    hb = hn.astype(BF16)

    pc = jnp.dot(hb, win_ref[:, 0:3 * D_CONV], preferred_element_type=F32)
    u = pc[:, 2 * D_CONV:3 * D_CONV] * pc[:, 0:D_CONV]
    first = (i % tiles_per_seq) == 0
    ubuf[0:SUBLANES, :] = jnp.where(first, 0.0, ubuf[tm:tm + SUBLANES, :])
    ubuf[SUBLANES:SUBLANES + tm, :] = u
    u1 = ubuf[SUBLANES - 1:SUBLANES - 1 + tm, :]
    u2 = ubuf[SUBLANES - 2:SUBLANES - 2 + tm, :]
    cw = convw_ref[...]
    yc = pc[:, D_CONV:2 * D_CONV] * (cw[0:1] * u2 + cw[1:2] * u1 + cw[2:3] * u)
    yconv_ref[...] = (_rms(yc, NORM_EPS) * gconv_ref[...]).astype(BF16)

    half = ROT_DIM // 2
    ang = invf_ref[...] * pos_ref[...].astype(F32)
    cos_ft = jnp.cos(ang)
    sin_ft = jnp.sin(ang)
    zero_h = jnp.zeros((half, tm), F32)
    zero_r = jnp.zeros((HEAD_DIM - ROT_DIM, tm), F32)
    lay = lambda a, b, r: jnp.concatenate([a, b, r, a, b, r], axis=0).T
    cos = lay(cos_ft, cos_ft, zero_r + 1.0)
    s_up = lay(-sin_ft, zero_h, zero_r)
    s_dn = lay(zero_h, sin_ft, zero_r)

    def rope(t):
        return t * cos + pltpu.roll(t, LANES - half, axis=1) * s_up + pltpu.roll(t, half, axis=1) * s_dn

    pq = jnp.dot(hb, win_ref[:, 3 * D_CONV:3 * D_CONV + D_QK], preferred_element_type=F32)
    pk = jnp.dot(hb, win_ref[:, 3 * D_CONV + D_QK:3 * D_CONV + 2 * D_QK], preferred_element_type=F32)
    scale = HEAD_DIM ** -0.5 * math.log2(math.e)
    qs = []
    for h in range(N_HEADS):
        sl = slice(h * LANES, (h + 1) * LANES)
        qs.append(rope(pq[:, sl]) * scale)
        k_ref[:, sl] = rope(pk[:, sl]).astype(BF16)
    qt_ref[0] = jnp.concatenate(qs, axis=1).T.astype(BF16)
    pv = jnp.dot(hb, win_ref[:, 3 * D_CONV + 2 * D_QK:], preferred_element_type=F32)
    vt_ref[0] = pv.T.astype(BF16)


def _inproj(xf, pos, ada3, g1, win_b, conv_w, g_conv, invf, seq):
    t, d = xf.shape
    tm = min(1024, seq)
    tiles_per_seq = seq // tm
    n_in = win_b.shape[1]
    full = lambda shape: pl.BlockSpec(shape, lambda i: (0,) * len(shape))
    row = lambda w: pl.BlockSpec((tm, w), lambda i: (i, 0))
    colmajor = lambda w: pl.BlockSpec((1, w, tm), lambda i: (i // tiles_per_seq, 0, i % tiles_per_seq))
    return pl.pallas_call(
        functools.partial(_inproj_kernel, tiles_per_seq),
        grid=(t // tm,),
        in_specs=[row(d), pl.BlockSpec((1, tm), lambda i: (0, i)),
                  pl.BlockSpec((1, 6, d), lambda i: (i // tiles_per_seq, 0, 0)),
                  full((1, d)), full((d, n_in)), full((CONV_WIDTH, D_CONV)), full((1, D_CONV)),
                  full((ROT_DIM // 2, 1))],
        out_specs=[row(D_CONV), colmajor(D_QK), row(D_QK), colmajor(D_ATTN)],
        out_shape=[jax.ShapeDtypeStruct((t, D_CONV), BF16), jax.ShapeDtypeStruct((t // seq, D_QK, seq), BF16),
                   jax.ShapeDtypeStruct((t, D_QK), BF16), jax.ShapeDtypeStruct((t // seq, D_ATTN, seq), BF16)],
        scratch_shapes=[pltpu.VMEM((tm + SUBLANES, D_CONV), F32)],
        compiler_params=_cparams(("arbitrary",)),
        name="inproj",
    )(xf, pos, ada3, g1, win_b, conv_w, g_conv, invf)


def _attn_steps(nq):
    off = [(j, i) for j in range(nq) for i in range(j + 1, nq)]
    diag = [(i, i) for i in range(nq)]
    return off, diag


def _attn_kernel(steps_ref, qt_ref, k_ref, vt_ref, lamp_ref, gs_ref, o_ref,
                 qq_ref, m_ref, acc_ref, s0_ref, s1_ref):
    nq = m_ref.shape[0]
    tq = s0_ref.shape[0]
    tk = tq
    off, diag = _attn_steps(nq)
    n_steps = len(off) + len(diag)
    bufs = (s0_ref, s1_ref)

    feat = lax.broadcasted_iota(I32, (LANES, tq), 0)
    for i in range(nq):
        qt = qt_ref[0, :, i * tq:(i + 1) * tq]
        zero = jnp.zeros_like(qt)
        qq_ref[:, 2 * i * tq:(2 * i + 1) * tq] = jnp.where(feat < HEAD_DIM, qt, zero)
        qq_ref[:, (2 * i + 1) * tq:(2 * i + 2) * tq] = jnp.where(feat >= HEAD_DIM, qt, zero)
    m_ref[...] = jnp.full(m_ref.shape, NEG_BIG, F32)
    acc_ref[...] = jnp.zeros(acc_ref.shape, F32)
    ones_rows = jnp.ones((ATTN_ONES_ROWS, tk), BF16)

    def scores(j, i, dst):
        kstart = pl.multiple_of(j * tk, tk)
        qstart = pl.multiple_of(i * (2 * tq), 2 * tq)
        dst[...] = jnp.dot(k_ref[pl.ds(kstart, tk), :], qq_ref[:, pl.ds(qstart, 2 * tq)],
                           preferred_element_type=F32)

    def consume(src, j, i, masked):
        kstart = pl.multiple_of(j * tk, tk)
        vtb = jnp.concatenate([vt_ref[0, :, pl.ds(kstart, tk)], ones_rows], axis=0)
        s = src[...]
        if masked:
            key = lax.broadcasted_iota(I32, s.shape, 0)
            col = lax.broadcasted_iota(I32, s.shape, 1)
            qpos = jnp.where(col >= tq, col - tq, col)
            s = jnp.where(key <= qpos, s, NEG_BIG)
        m_prev = m_ref[i]
        m_new = jnp.maximum(m_prev, jnp.max(s, axis=0, keepdims=True))
        alpha = jnp.exp2(m_prev - m_new)
        p = jnp.exp2(s - m_new)
        acc_ref[i] = alpha * acc_ref[i] + jnp.dot(vtb, p.astype(BF16), preferred_element_type=F32)
        m_ref[i] = m_new

    def at(n):
        return steps_ref[0, n], steps_ref[1, n]

    scores(*at(0), s0_ref)
    n_pairs = len(off) // 2

    def pair(pp, carry):
        n = 2 * pp
        scores(*at(n + 1), s1_ref)
        consume(s0_ref, *at(n), False)
        scores(*at(n + 2), s0_ref)
        consume(s1_ref, *at(n + 1), False)
        return carry

    lax.fori_loop(0, n_pairs, pair, 0)

    static_steps = (off + diag)[2 * n_pairs:]
    for n, (j, i) in enumerate(static_steps, start=2 * n_pairs):
        if n + 1 < n_steps:
            scores(*static_steps[n + 1 - 2 * n_pairs], bufs[(n + 1) % 2])
        consume(bufs[n % 2], j, i, masked=(j == i))

    lamp = lamp_ref[...]
    lam = (jnp.exp(jnp.sum(lamp[0:1] * lamp[1:2], axis=1, keepdims=True))
           - jnp.exp(jnp.sum(lamp[2:3] * lamp[3:4], axis=1, keepdims=True)) + LAMBDA_INIT)
    for i in range(nq):
        o = acc_ref[i, 0:V_DIM] / acc_ref[i, V_DIM:V_DIM + 1]
        od = o[:, 0:tq] - lam * o[:, tq:2 * tq]
        y = od * lax.rsqrt(jnp.mean(od * od, axis=0, keepdims=True) + SUBLN_EPS) * gs_ref[...] * (1.0 - LAMBDA_INIT)
        o_ref[i * tq:(i + 1) * tq, :] = y.T.astype(BF16)


def _attention(qt, k, vt, lamp, g_subln, bsz, seq):
    t = k.shape[0]
    tq = min(512, seq)
    nq = seq // tq
    off, diag = _attn_steps(nq)
    steps = jnp.asarray(list(zip(*(off + diag))), I32)
    return pl.pallas_call(
        _attn_kernel,
        grid_spec=pltpu.PrefetchScalarGridSpec(
            num_scalar_prefetch=1,
            grid=(bsz, N_HEADS),
            in_specs=[pl.BlockSpec((1, LANES, seq), lambda b, h, st: (b, h, 0)),
                      pl.BlockSpec((seq, LANES), lambda b, h, st: (b, h)),
                      pl.BlockSpec((1, V_DIM, seq), lambda b, h, st: (b, h, 0)),
                      pl.BlockSpec((4, HEAD_DIM), lambda b, h, st: (0, 0)),
                      pl.BlockSpec((V_DIM, 1), lambda b, h, st: (0, 0))],
            out_specs=pl.BlockSpec((seq, LANES), lambda b, h, st: (b, h)),
            scratch_shapes=[pltpu.VMEM((LANES, 2 * seq), BF16), pltpu.VMEM((nq, 1, 2 * tq), F32),
                            pltpu.VMEM((nq, V_DIM + ATTN_ONES_ROWS, 2 * tq), F32),
                            pltpu.VMEM((tq, 2 * tq), F32), pltpu.VMEM((tq, 2 * tq), F32)],
        ),
        out_shape=jax.ShapeDtypeStruct((t, D_ATTN), BF16),
        compiler_params=_cparams(("arbitrary", "arbitrary")),
        name="attention",
    )(steps, qt, k, vt, lamp, g_subln)


def _first_index(hit, idx, size, axis):
    return jnp.min(jnp.where(hit, idx, size), axis=axis, keepdims=True)


def _outproj_router_kernel(yc_ref, ya_ref, x_ref, ada_ref, g2_ref, wout_ref, wrt_ref, rb_ref,
                           x1_ref, hp_ref, idx_ref, wts_ref, rank_ref, cnt_ref, base_ref):
    i = pl.program_id(0)
    tm = x_ref.shape[0]

    @pl.when(i == 0)
    def _():
        base_ref[...] = jnp.zeros(base_ref.shape, F32)

    ada = ada_ref[0]
    gt1, sh2, sc2 = ada[2:3], ada[3:4], ada[4:5]
    mix = (jnp.dot(yc_ref[...], wout_ref[0:D_CONV, :], preferred_element_type=F32)
           + jnp.dot(ya_ref[...], wout_ref[D_CONV:, :], preferred_element_type=F32))
    x1 = x_ref[...] + gt1 * mix
    x1_ref[...] = x1
    hn2 = _rms(x1, NORM_EPS) * g2_ref[...] * (1.0 + sc2) + sh2
    hp_ref[...] = _pack_bf16_pairs(hn2)
    hb = hn2.astype(BF16)

    logits = lax.dot_general(wrt_ref[...], hb, (((1,), (1,)), ((), ())), preferred_element_type=F32)
    scores = jax.nn.sigmoid(logits)
    choice = scores + rb_ref[...]
    neg_inf = jnp.float32(-jnp.inf)

    ch3 = choice.reshape(N_GROUPS, GROUP_SIZE, tm)
    i3 = lax.broadcasted_iota(I32, ch3.shape, 1)
    m1 = jnp.max(ch3, axis=1, keepdims=True)
    f1 = _first_index(ch3 == m1, i3, GROUP_SIZE, 1)
    m2 = jnp.max(jnp.where(i3 == f1, neg_inf, ch3), axis=1, keepdims=True)
    gs = (m1 + m2).reshape(N_GROUPS, tm)

    gi = lax.broadcasted_iota(I32, gs.shape, 0)
    gkeep = jnp.zeros(gs.shape, F32)
    for _ in range(TOPK_GROUPS):
        m = jnp.max(gs, axis=0, keepdims=True)
        f = _first_index(gs == m, gi, N_GROUPS, 0)
        sel = gi == f
        gkeep = jnp.where(sel, 1.0, gkeep)
        gs = jnp.where(sel, neg_inf, gs)
    ekeep = jnp.broadcast_to(gkeep.reshape(N_GROUPS, 1, tm), (N_GROUPS, GROUP_SIZE, tm)).reshape(N_EXPERTS, tm)
    masked = jnp.where(ekeep > 0.0, choice, neg_inf)

    ei = lax.broadcasted_iota(I32, masked.shape, 0)
    idxs, ws = [], []
    for _ in range(TOP_K):
        m = jnp.max(masked, axis=0, keepdims=True)
        f = _first_index(masked == m, ei, N_EXPERTS, 0)
        sel = ei == f
        idxs.append(f)
        ws.append(jnp.sum(jnp.where(sel, scores, 0.0), axis=0, keepdims=True))
        masked = jnp.where(sel, neg_inf, masked)
    picked = jnp.where(ekeep > 0.0, jnp.where(masked == neg_inf, 1.0, 0.0), 0.0)
    wsum = ws[0]
    for wk in ws[1:]:
        wsum = wsum + wk
    denom = wsum + 1e-20
    idx_all = jnp.concatenate(idxs, axis=0)
    for cc in range(tm // SC_CHUNK):
        idx_ref[cc] = idx_all[:, cc * SC_CHUNK:(cc + 1) * SC_CHUNK]
    wrep = jnp.concatenate([jnp.broadcast_to(ws[kk] / denom * ROUTED_SCALE, (SC_LANES, tm)) for kk in range(TOP_K)],
                           axis=0)
    wts_ref[...] = wrep.T

    si = lax.broadcasted_iota(I32, (tm, tm), 0)
    ti = lax.broadcasted_iota(I32, (tm, tm), 1)
    earlier = jnp.where(si < ti, 1.0, 0.0).astype(BF16)
    before = jnp.dot(picked.astype(BF16), earlier, preferred_element_type=F32) + base_ref[...]
    rank_all = jnp.concatenate(
        [jnp.sum(jnp.where(ei == idxs[kk], before, 0.0), axis=0, keepdims=True) for kk in range(TOP_K)],
        axis=0).astype(I32)
    for cc in range(tm // SC_CHUNK):
        rank_ref[cc] = rank_all[:, cc * SC_CHUNK:(cc + 1) * SC_CHUNK]
    base_ref[...] = base_ref[...] + jnp.sum(picked, axis=1, keepdims=True)
    cnt_ref[...] = base_ref[...]


def _outproj_router(yconv, yattn, xf, ada3, g2, wout_b, wrt_b, rbias, seq):
    t, d = xf.shape
    tm = min(512, seq)
    tiles_per_seq = seq // tm
    full = lambda shape: pl.BlockSpec(shape, lambda i: (0,) * len(shape))
    row = lambda w: pl.BlockSpec((tm, w), lambda i: (i, 0))
    col = pl.BlockSpec((tm // SC_CHUNK, TOP_K, SC_CHUNK), lambda i: (i, 0, 0))
    return pl.pallas_call(
        _outproj_router_kernel,
        grid=(t // tm,),
        in_specs=[row(D_CONV), row(D_ATTN), row(d),
                  pl.BlockSpec((1, 6, d), lambda i: (i // tiles_per_seq, 0, 0)),
                  full((1, d)), full((D_CONV + D_ATTN, d)), full((N_EXPERTS, d)), full((N_EXPERTS, 1))],
        out_specs=[row(d), row(d // 2), col, row(TOP_K * SC_LANES), col, full((N_EXPERTS, 1))],
        out_shape=[jax.ShapeDtypeStruct((t, d), F32), jax.ShapeDtypeStruct((t, d // 2), I32),
                   jax.ShapeDtypeStruct((t // SC_CHUNK, TOP_K, SC_CHUNK), I32),
                   jax.ShapeDtypeStruct((t, TOP_K * SC_LANES), F32),
                   jax.ShapeDtypeStruct((t // SC_CHUNK, TOP_K, SC_CHUNK), I32),
                   jax.ShapeDtypeStruct((N_EXPERTS, 1), F32)],
        scratch_shapes=[pltpu.VMEM((N_EXPERTS, 1), F32)],
        compiler_params=_cparams(("arbitrary",)),
        name="outproj_router",
    )(yconv, yattn, xf, ada3, g2, wout_b, wrt_b, rbias)


def _sc_dispatch(hp, idx_ck, rank_ck, pad_start, n_rows):
    t, w = hp.shape
    n_chunks, nk, _ = idx_ck.shape
    per_worker = n_chunks // SC_WORKERS
    mesh = plsc.VectorSubcoreMesh(core_axis_name="c", subcore_axis_name="s")

    @functools.partial(
        pl.kernel, mesh=mesh,
        out_type=(jax.ShapeDtypeStruct((n_rows, w), hp.dtype), jax.ShapeDtypeStruct(idx_ck.shape, I32)),
        scratch_types=[pltpu.VMEM(pad_start.shape, I32), pltpu.VMEM((nk, SC_CHUNK), I32),
                       pltpu.VMEM((nk, SC_CHUNK), I32), pltpu.VMEM((nk, SC_CHUNK), I32),
                       pltpu.VMEM((SC_CHUNK, w), hp.dtype), pltpu.SemaphoreType.DMA],
        compiler_params=pltpu.CompilerParams(needs_layout_passes=False),
    )
    def k(hp_hbm, idx_hbm, rank_hbm, ps_hbm, xs_hbm, dest_hbm, ps_v, idx_v, rank_v, dest_v, rows_v, sem):
        wid = lax.axis_index("s") * SC_CORES + lax.axis_index("c")
        pltpu.sync_copy(ps_hbm, ps_v)

        @pl.loop(0, per_worker)
        def _(j):
            c = wid * per_worker + j
            pltpu.sync_copy(idx_hbm.at[c], idx_v)
            pltpu.sync_copy(rank_hbm.at[c], rank_v)
            pltpu.sync_copy(hp_hbm.at[pl.ds(c * SC_CHUNK, SC_CHUNK)], rows_v)
            for kk in range(nk):
                for v in range(SC_CHUNK // SC_LANES):
                    lanes = pl.ds(v * SC_LANES, SC_LANES)
                    dest_v[kk, lanes] = plsc.load_gather(ps_v, [idx_v[kk, lanes]]) + rank_v[kk, lanes]
            for kk in range(nk):
                pltpu.async_copy(rows_v, xs_hbm.at[dest_v.at[kk]], sem)
            pltpu.sync_copy(dest_v, dest_hbm.at[c])
            for kk in range(nk):
                pltpu.make_async_copy(rows_v, xs_hbm.at[dest_v.at[kk]], sem).wait()

    return k(hp, idx_ck, rank_ck, pad_start)


def _sc_combine(src, dest_ck, wrep, n_tokens):
    half = src.shape[1]
    lanes = wrep.shape[1] // TOP_K
    ctok = COMBINE_SLOTS // TOP_K
    per_worker = n_tokens // ctok // SC_WORKERS
    steps_per_chunk = SC_CHUNK // ctok
    chunks_per_worker = per_worker // steps_per_chunk
    mesh = plsc.VectorSubcoreMesh(core_axis_name="c", subcore_axis_name="s")

    @functools.partial(
        pl.kernel, mesh=mesh,
        out_type=jax.ShapeDtypeStruct((n_tokens, 2 * half), F32),
        scratch_types=[pltpu.VMEM((chunks_per_worker, TOP_K, SC_CHUNK), I32),
                       pltpu.VMEM((COMBINE_BUFS, COMBINE_SLOTS, half), I32),
                       pltpu.VMEM((COMBINE_BUFS, ctok, TOP_K * lanes), F32),
                       pltpu.VMEM((COMBINE_BUFS, ctok, 2 * half), F32),
                       pltpu.SemaphoreType.DMA((COMBINE_BUFS,)), pltpu.SemaphoreType.DMA((COMBINE_BUFS,)),
                       pltpu.SemaphoreType.DMA((COMBINE_BUFS,))],
        compiler_params=pltpu.CompilerParams(needs_layout_passes=False),
    )
    def k(src_hbm, idx_hbm, w_hbm, y_hbm, idx_v, rows_v, w_v, out_v, sem_g, sem_w, sem_o):
        wid = lax.axis_index("s") * SC_CORES + lax.axis_index("c")
        base = wid * per_worker
        pltpu.sync_copy(idx_hbm.at[pl.ds(wid * chunks_per_worker, chunks_per_worker)], idx_v)

        class _Gather:
            def __init__(self, c, slot):
                chunk = c // steps_per_chunk
                first = (c % steps_per_chunk) * ctok
                self.copies = [
                    pltpu.make_async_copy(src_hbm.at[idx_v.at[chunk, kk, pl.ds(first, ctok)]],
                                          rows_v.at[slot, pl.ds(kk * ctok, ctok)], sem_g.at[slot])
                    for kk in range(TOP_K)]

            def start(self):
                for cp in self.copies:
                    cp.start()

            def wait(self):
                for cp in self.copies:
                    cp.wait()

        gather = _Gather

        def wload(c, slot):
            return pltpu.make_async_copy(w_hbm.at[pl.ds((base + c) * ctok, ctok)], w_v.at[slot], sem_w.at[slot])

        def store(c, slot):
            return pltpu.make_async_copy(out_v.at[slot], y_hbm.at[pl.ds((base + c) * ctok, ctok)], sem_o.at[slot])

        nbuf = COMBINE_BUFS
        for c in range(nbuf - 1):
            gather(c, c).start()
            wload(c, c).start()

        @pl.loop(0, pl.cdiv(per_worker, nbuf) * nbuf, step=nbuf)
        def _(c0):
            for slot in range(nbuf):
                c = c0 + slot

                @pl.when(c < per_worker)
                def _():
                    gather(c, slot).wait()
                    wload(c, slot).wait()

                    @pl.when(c + nbuf - 1 < per_worker)
                    def _():
                        gather(c + nbuf - 1, (slot + nbuf - 1) % nbuf).start()
                        wload(c + nbuf - 1, (slot + nbuf - 1) % nbuf).start()

                    @pl.when(c >= nbuf)
                    def _():
                        store(c - nbuf, slot).wait()

                    @pl.loop(0, ctok)
                    def _(tt):
                        ws = [w_v[slot, tt, pl.ds(kk * lanes, lanes)] for kk in range(TOP_K)]

                        @plsc.parallel_loop(0, half // lanes, unroll=2)
                        def _(v):
                            lo = jnp.zeros((lanes,), F32)
                            hi = jnp.zeros((lanes,), F32)
                            for kk in range(TOP_K):
                                x = rows_v[slot, kk * ctok + tt, pl.ds(v * lanes, lanes)]
                                x_lo, x_hi = plsc.unpack(plsc.bitcast(x, BF16), format=plsc.PackFormat.INTERLEAVED)
                                lo = lo + x_lo * ws[kk]
                                hi = hi + x_hi * ws[kk]
                            out_v[slot, tt, pl.ds(v * lanes, lanes)] = lo
                            out_v[slot, tt, pl.ds(half + v * lanes, lanes)] = hi

                    store(c, slot).start()

        for c in range(per_worker - nbuf, per_worker):
            store(c, c % nbuf).wait()

    return k(src, dest_ck, wrep)


def _experts_kernel(bstart_ref, xs_hbm, wg_hbm, wu_hbm, wd_hbm, out_hbm,
                    wgu_b, wd_b, xbuf, obuf, sem_in, sem_out, wg_f, wu_f, wd_f, sem_w):
    step = pl.program_id(0)
    n_used = bstart_ref[N_EXPERTS]
    nbuf = EXPERT_BUFS

    def in_copy(g, slot):
        return pltpu.make_async_copy(xs_hbm.at[pl.ds(g * ROW_TILE, ROW_TILE)], xbuf.at[slot], sem_in.at[slot])

    def out_copy(g, slot):
        return pltpu.make_async_copy(obuf.at[slot], out_hbm.at[pl.ds(g * ROW_TILE, ROW_TILE)], sem_out.at[slot])

    def fetch(g):
        @pl.when(g < n_used)
        def _():
            in_copy(g, lax.rem(g, nbuf)).start()

    def run(blocks):
        slots = [lax.rem(g, nbuf) for g in blocks]
        for g, slot in zip(blocks, slots):
            in_copy(g, slot).wait()
        fetch(blocks[0] + nbuf - 1)
        for g, slot in zip(blocks, slots):
            @pl.when(g >= nbuf)
            def _():
                out_copy(g - nbuf, slot).wait()
        xs_rows = [_unpack_bf16_pairs(xbuf[slot]).astype(BF16) for slot in slots]
        xb = xs_rows[0] if len(slots) == 1 else jnp.concatenate(xs_rows, axis=0)
        gu = jnp.dot(xb, wgu_b[...], preferred_element_type=F32)
        act = (jax.nn.silu(gu[:, 0:D_EXPERT]) * gu[:, D_EXPERT:2 * D_EXPERT]).astype(BF16)
        out = _pack_bf16_pairs(jnp.dot(act, wd_b[...], preferred_element_type=F32))
        for n, slot in enumerate(slots):
            obuf[slot] = out[n * ROW_TILE:(n + 1) * ROW_TILE]
        for g, slot in zip(blocks, slots):
            out_copy(g, slot).start()
        for g in blocks[:-1]:
            fetch(g + nbuf)

    @pl.when(step == 0)
    def _():
        for g in range(nbuf - 1):
            @pl.when(g < n_used)
            def _():
                in_copy(g, g).start()

    nwb = EXPERT_WEIGHT_BUFS

    def weight_copies(e, slot):
        return [pltpu.make_async_copy(src.at[e], dst.at[slot], sem_w.at[slot])
                for src, dst in ((wg_hbm, wg_f), (wu_hbm, wu_f), (wd_hbm, wd_f))]

    @pl.when(step == 0)
    def _():
        for e0 in range(nwb - 1):
            for cp in weight_copies(e0, e0):
                cp.start()

    for sub in range(EXPERTS_PER_STEP):
        e = step * EXPERTS_PER_STEP + sub
        b0 = bstart_ref[e]
        b1 = bstart_ref[e + 1]
        wslot = lax.rem(e, nwb)
        for cp in weight_copies(e, wslot):
            cp.wait()

        @pl.when(e + nwb - 1 < N_EXPERTS)
        def _():
            for cp in weight_copies(e + nwb - 1, lax.rem(e + nwb - 1, nwb)):
                cp.start()

        @pl.when(b1 > b0)
        def _():
            wgu_b[:, 0:D_EXPERT] = wg_f[wslot].astype(BF16)
            wgu_b[:, D_EXPERT:2 * D_EXPERT] = wu_f[wslot].astype(BF16)
            wd_b[...] = wd_f[wslot].astype(BF16)
            group = EXPERT_GROUP

            def full_group(jj, carry):
                g = b0 + group * jj
                run([g + n for n in range(group)])
                return carry

            n_full = (b1 - b0) // group
            lax.fori_loop(0, n_full, full_group, 0)
            rest = (b1 - b0) - n_full * group
            tail = b0 + n_full * group
            for size in range(1, group):
                @pl.when(rest == size)
                def _():
                    run([tail + n for n in range(size)])

    @pl.when(step == pl.num_programs(0) - 1)
    def _():
        for back in range(1, nbuf + 1):
            @pl.when(n_used >= back)
            def _():
                g = n_used - back
                out_copy(g, lax.rem(g, nbuf)).wait()


def _experts(xs, bstart, w_gate, w_up, w_down):
    n_rows, half = xs.shape
    d = 2 * half
    return pl.pallas_call(
        _experts_kernel,
        grid_spec=pltpu.PrefetchScalarGridSpec(
            num_scalar_prefetch=1,
            grid=(N_EXPERTS // EXPERTS_PER_STEP,),
            in_specs=[pl.BlockSpec(memory_space=pl.ANY)] * 4,
            out_specs=pl.BlockSpec(memory_space=pl.ANY),
            scratch_shapes=[pltpu.VMEM((d, 2 * D_EXPERT), BF16), pltpu.VMEM((D_EXPERT, d), BF16),
                            pltpu.VMEM((EXPERT_BUFS, ROW_TILE, half), I32),
                            pltpu.VMEM((EXPERT_BUFS, ROW_TILE, half), I32),
                            pltpu.SemaphoreType.DMA((EXPERT_BUFS,)), pltpu.SemaphoreType.DMA((EXPERT_BUFS,)),
                            pltpu.VMEM((EXPERT_WEIGHT_BUFS, d, D_EXPERT), F32),
                            pltpu.VMEM((EXPERT_WEIGHT_BUFS, d, D_EXPERT), F32),
                            pltpu.VMEM((EXPERT_WEIGHT_BUFS, D_EXPERT, d), F32),
                            pltpu.SemaphoreType.DMA((EXPERT_WEIGHT_BUFS,))],
        ),
        out_shape=jax.ShapeDtypeStruct((n_rows, half), I32),
        compiler_params=_cparams(("arbitrary",)),
        name="experts",
    )(bstart, xs, w_gate, w_up, w_down)


def _final_kernel(x1_ref, hp_ref, yr_ref, ada_ref, wgs_ref, wus_ref, wds_ref, gf_ref, o_ref):
    gt2 = ada_ref[0][5:6]
    hb = _unpack_bf16_pairs(hp_ref[...]).astype(BF16)
    g = jnp.dot(hb, wgs_ref[...], preferred_element_type=F32)
    u = jnp.dot(hb, wus_ref[...], preferred_element_type=F32)
    y = jnp.dot((jax.nn.silu(g) * u).astype(BF16), wds_ref[...], preferred_element_type=F32)
    x2 = x1_ref[...] + gt2 * (yr_ref[...] + y)
    o_ref[...] = _rms(x2, NORM_EPS) * gf_ref[...]


def _final(x1, hp, y_routed, ada3, wgs_b, wus_b, wds_b, g_final, seq):
    t, d = x1.shape
    tm = min(512, seq)
    tiles_per_seq = seq // tm
    full = lambda shape: pl.BlockSpec(shape, lambda i: (0,) * len(shape))
    row = lambda w: pl.BlockSpec((tm, w), lambda i: (i, 0))
    return pl.pallas_call(
        _final_kernel,
        grid=(t // tm,),
        in_specs=[row(d), row(d // 2), row(d),
                  pl.BlockSpec((1, 6, d), lambda i: (i // tiles_per_seq, 0, 0)),
                  full((d, D_EXPERT)), full((d, D_EXPERT)), full((D_EXPERT, d)), full((1, d))],
        out_specs=row(d),
        out_shape=jax.ShapeDtypeStruct((t, d), F32),
        compiler_params=_cparams(("arbitrary",)),
        name="final",
    )(x1, hp, y_routed, ada3, wgs_b, wus_b, wds_b, g_final)


def kernel(x, c, positions, w_ada, b_ada, g_norm1, w_in, conv_w, g_conv_out, lam_q1, lam_k1, lam_q2, lam_k2, g_subln, w_out, g_norm2, w_router, router_bias, w_gate_e, w_up_e, w_down_e, w_gate_s, w_up_s, w_down_s, g_final):
    bsz, seq, d = x.shape
    t = bsz * seq
    xf = x.reshape(t, d)
    pos = positions.reshape(1, t)
    invf = (ROPE_THETA ** (-jnp.arange(0, ROT_DIM, 2, dtype=F32) / ROT_DIM)).reshape(ROT_DIM // 2, 1)

    ada3 = _ada(c, w_ada[0], b_ada[0]).reshape(bsz, 6, d)
    yconv, qt, k, vt = _inproj(xf, pos, ada3, g_norm1[0].reshape(1, d), w_in[0].astype(BF16), conv_w[0],
                             g_conv_out[0].reshape(1, D_CONV), invf, seq)
    lamp = jnp.stack([lam_q1[0], lam_k1[0], lam_q2[0], lam_k2[0]]).astype(F32)
    yattn = _attention(qt, k, vt, lamp, g_subln[0].reshape(V_DIM, 1), bsz, seq)

    x1, hp, idx_ck, wrep, rank_ck, counts_f = _outproj_router(
        yconv, yattn, xf, ada3, g_norm2[0].reshape(1, d), w_out[0].astype(BF16),
        w_router[0].T.astype(BF16), router_bias[0].reshape(N_EXPERTS, 1), seq)

    counts = counts_f[:, 0].astype(I32)
    padded = ((counts + ROW_TILE - 1) // ROW_TILE) * ROW_TILE
    pad_end = jnp.cumsum(padded)
    pad_start = pad_end - padded
    nb = (t * TOP_K + N_EXPERTS * (ROW_TILE - 1)) // ROW_TILE
    bstart = (jnp.concatenate([pad_start, pad_end[-1:]]) // ROW_TILE).astype(I32)

    xs, dest_ck = _sc_dispatch(hp, idx_ck, rank_ck, pad_start.astype(I32), nb * ROW_TILE)
    outs = _experts(xs, bstart, w_gate_e[0], w_up_e[0], w_down_e[0])
    y_routed = _sc_combine(outs, dest_ck, wrep, t)

    out = _final(x1, hp, y_routed, ada3,
                 w_gate_s[0].astype(BF16), w_up_s[0].astype(BF16), w_down_s[0].astype(BF16),
                 g_final.reshape(1, d), seq)
    return out.reshape(bsz, seq, d)
```
